```python
import math
import jax, jax.numpy as jnp
from jax import lax
import numpy as np

D_MODEL = 2048
BATCH = 4
SEQ = 4096
DEPTH = 1

HEAD_DIM = 128
NSA_HEADS = 8
NSA_KV_GROUPS = 2
NSA_REP = NSA_HEADS // NSA_KV_GROUPS
NSA_CMP_LEN = 32
NSA_CMP_STRIDE = 16
NSA_SEL_LEN = 64
NSA_TOP_N = 16
NSA_WINDOW = 512
NSA_SEL_QBLOCK = 64
DIL_CONFIGS = ((128, 1), (512, 4), (2048, 16))
DIL_HEADS_PER_GROUP = 4
DIL_HEADS = DIL_HEADS_PER_GROUP * len(DIL_CONFIGS)
DIL_OUT = DIL_HEADS_PER_GROUP * HEAD_DIM
ATTN_QBLOCK = 128
REL_BUCKETS = 32
REL_MAX_DIST = 2048
N_BIAS_HEADS = NSA_HEADS + DIL_HEADS
PEER_HEADS = 8
PEER_N_KEYS = 128
PEER_N_EXPERTS = PEER_N_KEYS * PEER_N_KEYS
PEER_TOPK = 16
PEER_HALF = 128
PEER_TBLOCK = 128

EPS = 1e-6
NEG = -1e30
SEL_FORCE = 1e4

NSA_Q_COLS = NSA_HEADS * HEAD_DIM
NSA_KV_COLS = 3 * 2 * NSA_KV_GROUPS * HEAD_DIM
NSA_GATE_COLS = 3 * NSA_HEADS
DIL_COLS = 3 * DIL_HEADS * HEAD_DIM
MERGE_COLS = 2 * D_MODEL
OFF_KV = NSA_Q_COLS
OFF_GATE = OFF_KV + NSA_KV_COLS
OFF_DIL = OFF_GATE + NSA_GATE_COLS
OFF_MERGE = OFF_DIL + DIL_COLS
IN_COLS = OFF_MERGE + MERGE_COLS

kernel_name = "hybrid_nsa_dilated_peer_block"


def rms_norm(x, g):
    xf = x.astype(jnp.float32)
    y = xf * lax.rsqrt(jnp.mean(xf * xf, axis=-1, keepdims=True) + EPS)
    return (y * g.astype(jnp.float32)).astype(x.dtype)


def t5_bucket(dist):
    n = jnp.maximum(dist, 0)
    max_exact = REL_BUCKETS // 2
    nf = jnp.maximum(n, max_exact).astype(jnp.float32)
    large = max_exact + (jnp.log(nf / max_exact) / math.log(REL_MAX_DIST / max_exact)
                         * (REL_BUCKETS - max_exact)).astype(jnp.int32)
    large = jnp.minimum(large, REL_BUCKETS - 1)
    return jnp.where(n < max_exact, n, large)


def to_heads(t, n_heads):
    b, s, _ = t.shape
    return t.reshape(b, s, n_heads, HEAD_DIM).transpose(0, 2, 1, 3)


def banded_attention(q, k, v, span, dist_scale, tbl):
    b, hq, L, dh = q.shape
    hk = k.shape[1]
    rep = hq // hk
    qb = ATTN_QBLOCK
    nb = -(-L // qb)
    lp = nb * qb
    pad_end = lp - L
    n_prev = -(-span // qb)
    w = (n_prev + 1) * qb
    q = jnp.pad(q, ((0, 0), (0, 0), (0, pad_end), (0, 0)))
    kp = jnp.pad(k, ((0, 0), (0, 0), (n_prev * qb, pad_end), (0, 0))).reshape(b, hk, nb + n_prev, qb, dh)
    vp = jnp.pad(v, ((0, 0), (0, 0), (n_prev * qb, pad_end), (0, 0))).reshape(b, hk, nb + n_prev, qb, dh)
    kw = jnp.concatenate([kp[:, :, j:j + nb] for j in range(n_prev + 1)], axis=3)
    vw = jnp.concatenate([vp[:, :, j:j + nb] for j in range(n_prev + 1)], axis=3)
    qr = q.reshape(b, hk, rep, nb, qb, dh)
    logits = jnp.einsum('bgrnqd,bgnkd->bgrnqk', qr, kw).astype(jnp.float32) * (dh ** -0.5)
    dist = jnp.arange(qb)[:, None] + n_prev * qb - jnp.arange(w)[None, :]
    kpos = (jnp.arange(nb)[:, None] - n_prev) * qb + jnp.arange(w)[None, :]
    valid = ((dist >= 0) & (dist <= span))[None, :, :] & (kpos >= 0)[:, None, :]
    bias = jnp.take(tbl.astype(jnp.float32), t5_bucket(dist * dist_scale), axis=1)
    bias = bias.reshape(hk, rep, 1, qb, w)
    logits = jnp.where(valid, logits + bias, NEG)
    lse = jax.nn.logsumexp(logits, axis=-1)
    p = jnp.exp(logits - lse[..., None])
    out = jnp.einsum('bgrnqk,bgnkd->bgrnqd', p.astype(v.dtype), vw)
    out = out.reshape(b, hq, lp, dh)[:, :, :L]
    lse = lse.reshape(b, hq, lp)[:, :, :L]
    return out, lse


def nsa_mixer(q, kv, gates, k_gain, cmp_pos, cmp_w1, cmp_w2, tbl):
    b, nh, s, dh = q.shape
    G, rep = NSA_KV_GROUPS, NSA_REP
    scale = dh ** -0.5
    k_cmp, v_cmp, k_sel, v_sel, k_win, v_win = [kv[:, i] for i in range(6)]
    k_sel = rms_norm(k_sel, k_gain[1])
    k_win = rms_norm(k_win, k_gain[2])
    tbl = tbl.astype(jnp.float32)

    nc = (s - NSA_CMP_LEN) // NSA_CMP_STRIDE + 1
    cidx = jnp.arange(nc)[:, None] * NSA_CMP_STRIDE + jnp.arange(NSA_CMP_LEN)[None, :]

    def compress(t, j):
        blk = t[:, :, cidx] + cmp_pos[j]
        hid = jax.nn.gelu(blk.reshape(b, G, nc, NSA_CMP_LEN * dh) @ cmp_w1[j])
        return hid @ cmp_w2[j]

    kc = rms_norm(compress(k_cmp, 0), k_gain[0])
    vc = compress(v_cmp, 1)
    qg = q.reshape(b, G, rep, s, dh)
    logits = jnp.einsum('bgrsd,bgcd->bgrsc', qg, kc).astype(jnp.float32) * scale
    cend = jnp.arange(nc) * NSA_CMP_STRIDE + NSA_CMP_LEN - 1
    cdist = jnp.arange(s)[:, None] - cend[None, :]
    cvalid = cdist >= 0
    cbias = jnp.take(tbl, t5_bucket(cdist), axis=1).reshape(G, rep, s, nc)
    p_cmp = jax.nn.softmax(jnp.where(cvalid, logits + cbias, NEG), axis=-1) * cvalid
    o_cmp = jnp.einsum('bgrsc,bgcd->bgrsd', p_cmp.astype(vc.dtype), vc).reshape(b, nh, s, dh)

    ns = s // NSA_SEL_LEN
    cstart = jnp.arange(nc) * NSA_CMP_STRIDE
    sstart = jnp.arange(ns) * NSA_SEL_LEN
    overlap = ((cstart[:, None] < sstart[None, :] + NSA_SEL_LEN)
               & (cstart[:, None] + NSA_CMP_LEN > sstart[None, :])).astype(jnp.float32)
    imp = jnp.einsum('bgrsc,cj->bgsj', p_cmp, overlap)
    cur = (jnp.arange(s) // NSA_SEL_LEN)[:, None]
    jb = jnp.arange(ns)[None, :]
    forced = (jb == 0) | (jb == cur) | (jb == cur - 1)
    score = jnp.where(forced, SEL_FORCE, jnp.where(jb > cur, -SEL_FORCE, imp))
    n_sel = min(NSA_TOP_N, ns)
    _, sel_idx = lax.top_k(score, n_sel)

    kb = k_sel.reshape(b, G, ns, NSA_SEL_LEN, dh)
    vb = v_sel.reshape(b, G, ns, NSA_SEL_LEN, dh)
    gather = jax.vmap(jax.vmap(lambda blocks, idx: blocks[idx]))
    tbl_g = tbl.reshape(G, rep, REL_BUCKETS)
    g_ix = jnp.arange(G)[:, None, None, None]
    r_ix = jnp.arange(rep)[:, None, None]
    qs = NSA_SEL_QBLOCK
    nqb = s // qs
    m = n_sel * NSA_SEL_LEN

    def sel_block(args):
        qblk, iblk, q0 = args
        ks = gather(kb, iblk).reshape(b, G, qs, m, dh)
        vs = gather(vb, iblk).reshape(b, G, qs, m, dh)
        kpos = (iblk[..., None] * NSA_SEL_LEN + jnp.arange(NSA_SEL_LEN)).reshape(b, G, qs, m)
        dist = (q0 + jnp.arange(qs))[None, None, :, None] - kpos
        bias = tbl_g[g_ix, r_ix, t5_bucket(dist)[:, :, None]]
        lg = jnp.einsum('bgrqd,bgqmd->bgrqm', qblk, ks).astype(jnp.float32) * scale
        lg = jnp.where((dist >= 0)[:, :, None], lg + bias, NEG)
        p = jax.nn.softmax(lg, axis=-1)
        return jnp.einsum('bgrqm,bgqmd->bgrqd', p.astype(vs.dtype), vs)

    q_blocks = qg.reshape(b, G, rep, nqb, qs, dh).transpose(3, 0, 1, 2, 4, 5)
    i_blocks = sel_idx.reshape(b, G, nqb, qs, n_sel).transpose(2, 0, 1, 3, 4)
    starts = jnp.arange(nqb, dtype=jnp.int32) * qs
    o_sel = lax.map(sel_block, (q_blocks, i_blocks, starts))
    o_sel = o_sel.transpose(1, 2, 3, 0, 4, 5).reshape(b, nh, s, dh)

    o_win, _ = banded_attention(q, k_win, v_win, NSA_WINDOW - 1, 1, tbl)

    g = gates.transpose(0, 2, 1, 3)
    o = g[..., 0:1] * o_cmp + g[..., 1:2] * o_sel + g[..., 2:3] * o_win
    return o.transpose(0, 2, 1, 3).reshape(b, s, nh * dh)


def dilated_mixer(q, k, v, tbl):
    b, _, s, dh = q.shape
    P = DIL_HEADS_PER_GROUP
    outs, lses = [], []
    for gi, (window, dil) in enumerate(DIL_CONFIGS):
        lo, hi = gi * P, (gi + 1) * P
        L = s // dil

        def split(t):
            return t[:, lo:hi].reshape(b, P, L, dil, dh).transpose(0, 1, 3, 2, 4).reshape(b, P * dil, L, dh)

        o, lse = banded_attention(split(q), split(k), split(v), window // dil, dil,
                                  jnp.repeat(tbl[lo:hi], dil, axis=0))
        outs.append(o.reshape(b, P, dil, L, dh).transpose(0, 1, 3, 2, 4).reshape(b, P, s, dh))
        lses.append(lse.reshape(b, P, dil, L).transpose(0, 1, 3, 2).reshape(b, P, s))
    w = jax.nn.softmax(jnp.stack(lses, axis=0), axis=0)
    o = jnp.sum(w[..., None].astype(q.dtype) * jnp.stack(outs, axis=0), axis=0)
    return o.transpose(0, 2, 1, 3).reshape(b, s, P * dh)


def token_mixer(h, w_in, nsa_q_gain, nsa_k_gain, cmp_pos, cmp_w1, cmp_w2, dil_q_gain, dil_k_gain,
                w_br_nsa, w_br_dil, w_out, tbl_nsa, tbl_dil):
    b, s, _ = h.shape
    proj = h @ w_in
    q_n = rms_norm(to_heads(proj[..., :OFF_KV], NSA_HEADS), nsa_q_gain)
    kv_n = to_heads(proj[..., OFF_KV:OFF_GATE], 6 * NSA_KV_GROUPS).reshape(b, 6, NSA_KV_GROUPS, s, HEAD_DIM)
    gates_n = jax.nn.sigmoid(proj[..., OFF_GATE:OFF_DIL].reshape(b, s, NSA_HEADS, 3))
    qkv_d = to_heads(proj[..., OFF_DIL:OFF_MERGE], 3 * DIL_HEADS).reshape(b, 3, DIL_HEADS, s, HEAD_DIM)
    merge_g = jax.nn.sigmoid(proj[..., OFF_MERGE:])
    y_nsa = nsa_mixer(q_n, kv_n, gates_n, nsa_k_gain, cmp_pos, cmp_w1, cmp_w2, tbl_nsa)
    y_dil = dilated_mixer(rms_norm(qkv_d[:, 0], dil_q_gain), rms_norm(qkv_d[:, 1], dil_k_gain),
                          qkv_d[:, 2], tbl_dil)
    merged = merge_g[..., :D_MODEL] * (y_nsa @ w_br_nsa) + merge_g[..., D_MODEL:] * (y_dil @ w_br_dil)
    return merged @ w_out


def peer_ffn(h, w_q, q_gain, sub_keys, u, v):
    b, s, d = h.shape
    t = b * s
    xt = h.reshape(t, d)
    q = rms_norm((xt @ w_q).reshape(t, PEER_HEADS, 2, PEER_HALF), q_gain)
    sc = jnp.einsum('thpk,hpnk->thpn', q, sub_keys).astype(jnp.float32)
    s1, i1 = lax.top_k(sc[:, :, 0], PEER_TOPK)
    s2, i2 = lax.top_k(sc[:, :, 1], PEER_TOPK)
    cand = (s1[..., :, None] + s2[..., None, :]).reshape(t, PEER_HEADS, PEER_TOPK * PEER_TOPK)
    cand_idx = (i1[..., :, None] * PEER_N_KEYS + i2[..., None, :]).reshape(t, PEER_HEADS, PEER_TOPK * PEER_TOPK)
    top_s, pos = lax.top_k(cand, PEER_TOPK)
    expert = jnp.take_along_axis(cand_idx, pos, axis=-1)
    gate = jax.nn.softmax(top_s, axis=-1).astype(h.dtype)
    nb = t // PEER_TBLOCK
    hk = PEER_HEADS * PEER_TOPK

    def block(args):
        xb, eb, gb = args
        act = jax.nn.gelu(jnp.einsum('td,ted->te', xb, u[eb]))
        return jnp.einsum('te,ted->td', gb * act, v[eb])

    out = lax.map(block, (xt.reshape(nb, PEER_TBLOCK, d), expert.reshape(nb, PEER_TBLOCK, hk),
                          gate.reshape(nb, PEER_TBLOCK, hk)))
    return out.reshape(b, s, d)


def _normal(key, shape, scale):
    return jax.random.normal(key, shape, jnp.float32) * scale


def setup_inputs(seed: int = 0) -> dict:
    key = jax.random.key(seed)
    ks = jax.random.split(key, 24)
    D, L, dh = D_MODEL, DEPTH, HEAD_DIM
    return {
        "x": _normal(ks[0], (BATCH, SEQ, D), 1.0),
        "c": _normal(ks[1], (BATCH, D), 1.0),
        "w_ada": _normal(ks[2], (L, D, 6 * D), 0.5 * D ** -0.5),
        "b_ada": _normal(ks[3], (L, 6 * D), 0.01),
        "norm1_g": 1.0 + _normal(ks[4], (L, D), 0.01),
        "w_in": _normal(ks[5], (L, D, IN_COLS), D ** -0.5),
        "nsa_q_gain": 1.0 + _normal(ks[6], (L, dh), 0.01),
        "nsa_k_gain": 1.0 + _normal(ks[7], (L, 3, dh), 0.01),
        "cmp_pos": _normal(ks[8], (L, 2, NSA_CMP_LEN, dh), 0.02),
        "cmp_w1": _normal(ks[9], (L, 2, NSA_CMP_LEN * dh, dh), (NSA_CMP_LEN * dh) ** -0.5),
        "cmp_w2": _normal(ks[10], (L, 2, dh, dh), dh ** -0.5),
        "dil_q_gain": 1.0 + _normal(ks[11], (L, dh), 0.01),
        "dil_k_gain": 1.0 + _normal(ks[12], (L, dh), 0.01),
        "w_br_nsa": _normal(ks[13], (L, NSA_HEADS * dh, D), (NSA_HEADS * dh) ** -0.5),
        "w_br_dil": _normal(ks[14], (L, DIL_OUT, D), DIL_OUT ** -0.5),
        "w_out": _normal(ks[15], (L, D, D), D ** -0.5),
        "norm2_g": 1.0 + _normal(ks[16], (L, D), 0.01),
        "peer_w_q": _normal(ks[17], (L, D, PEER_HEADS * 2 * PEER_HALF), D ** -0.5),
        "peer_q_gain": 1.0 + _normal(ks[18], (L, PEER_HEADS, 2, PEER_HALF), 0.01),
        "peer_sub_keys": _normal(ks[19], (L, PEER_HEADS, 2, PEER_N_KEYS, PEER_HALF), PEER_HALF ** -0.5),
        "peer_u": _normal(ks[20], (L, PEER_N_EXPERTS, D), D ** -0.5),
        "peer_v": _normal(ks[21], (L, PEER_N_EXPERTS, D), 0.5),
        "rel_bias": _normal(ks[22], (N_BIAS_HEADS, REL_BUCKETS), 0.1),
    }


def reference(x, c, w_ada, b_ada, norm1_g, w_in, nsa_q_gain, nsa_k_gain, cmp_pos, cmp_w1, cmp_w2,
              dil_q_gain, dil_k_gain, w_br_nsa, w_br_dil, w_out, norm2_g, peer_w_q, peer_q_gain,
              peer_sub_keys, peer_u, peer_v, rel_bias):
    tbl_nsa = rel_bias[:NSA_HEADS]
    tbl_dil = rel_bias[NSA_HEADS:]
    cond = jax.nn.silu(c)
    for layer in range(DEPTH):
        mod = cond @ w_ada[layer] + b_ada[layer]
        sh1, sc1, g1, sh2, sc2, g2 = [m[:, None, :] for m in jnp.split(mod, 6, axis=-1)]
        h = rms_norm(x, norm1_g[layer]) * (1.0 + sc1) + sh1
        x = x + g1 * token_mixer(h, w_in[layer], nsa_q_gain[layer], nsa_k_gain[layer], cmp_pos[layer],
                                 cmp_w1[layer], cmp_w2[layer], dil_q_gain[layer], dil_k_gain[layer],
                                 w_br_nsa[layer], w_br_dil[layer], w_out[layer], tbl_nsa, tbl_dil)
        h = rms_norm(x, norm2_g[layer]) * (1.0 + sc2) + sh2
        x = x + g2 * peer_ffn(h, peer_w_q[layer], peer_q_gain[layer], peer_sub_keys[layer],
                              peer_u[layer], peer_v[layer])
    return x
```

```python
import functools
import math

import numpy as np
import jax
import jax.numpy as jnp
from jax import lax
from jax.experimental import pallas as pl
from jax.experimental.pallas import tpu as pltpu

HEAD_DIM = 128
LANES = 128
NSA_HEADS = 8
NSA_KV_GROUPS = 2
NSA_REP = NSA_HEADS // NSA_KV_GROUPS
NSA_CMP_LEN = 32
NSA_CMP_STRIDE = 16
NSA_SEL_LEN = 64
NSA_TOP_N = 16
NSA_WINDOW = 512
DIL_CONFIGS = ((128, 1), (512, 4), (2048, 16))
DIL_HEADS_PER_GROUP = 4
DIL_HEADS = DIL_HEADS_PER_GROUP * len(DIL_CONFIGS)
REL_BUCKETS = 32
REL_MAX_DIST = 2048
PEER_HEADS = 8
PEER_N_KEYS = 128
PEER_TOPK = 16
PEER_HALF = 128
EPS = 1e-6
NEG = -1e30
SEL_FORCE = 1e4
MASKED_BUCKET = REL_BUCKETS

QT = 128
VMEM_LIMIT = 48 * 1024 * 1024

CB_MERGE = 0
def _layout(d_model):
    nmerge = 2 * d_model // LANES
    cb_q = nmerge
    cb_kv = cb_q + NSA_HEADS
    cb_dil = cb_kv + 6 * NSA_KV_GROUPS
    cb_gate = cb_dil + 3 * DIL_HEADS
    n_used = cb_gate + 1
    n_blocks = -(-n_used // 4) * 4
    return dict(q=cb_q, kv=cb_kv, dil=cb_dil, gate=cb_gate, n=n_blocks)


def _cparams(sem):
    return pltpu.CompilerParams(dimension_semantics=sem, vmem_limit_bytes=VMEM_LIMIT)


def _t5_bucket_np(dist):
    n = np.maximum(dist, 0)
    max_exact = REL_BUCKETS // 2
    nf = np.maximum(n, max_exact).astype(np.float32)
    large = max_exact + (np.log(nf / np.float32(max_exact)) / np.float32(math.log(REL_MAX_DIST / max_exact))
                         * np.float32(REL_BUCKETS - max_exact)).astype(np.int32)
    large = np.minimum(large, REL_BUCKETS - 1)
    return np.where(n < max_exact, n, large).astype(np.int32)


def _gelu(x):
    return jax.nn.gelu(x)


def _sigmoid(x):
    return 1.0 / (1.0 + jnp.exp(-x))


def _mod_kernel(c_ref, w_ref, b_ref, o_ref):
    c = c_ref[...]
    cond = c * _sigmoid(c)
    o_ref[...] = jnp.dot(cond.astype(jnp.bfloat16), w_ref[...].astype(jnp.bfloat16),
                         preferred_element_type=jnp.float32) + b_ref[...]


def _modulation(c, w_ada, b_ada):
    b, d = c.shape
    n = w_ada.shape[1]
    rows = 8
    cp = jnp.zeros((rows, d), jnp.float32).at[:b].set(c)
    tn = 512
    out = pl.pallas_call(
        _mod_kernel,
        grid=(n // tn,),
        in_specs=[pl.BlockSpec((rows, d), lambda j: (0, 0)),
                  pl.BlockSpec((d, tn), lambda j: (0, j)),
                  pl.BlockSpec((1, tn), lambda j: (0, j))],
        out_specs=pl.BlockSpec((rows, tn), lambda j: (0, j)),
        out_shape=jax.ShapeDtypeStruct((rows, n), jnp.float32),
        compiler_params=_cparams(("arbitrary",)),
        name="adaln_mod",
    )(cp, w_ada, b_ada.reshape(1, n))
    return out[:b]


def _proj_kernel(x_ref, g_ref, sc_ref, sh_ref, w_ref, flag_ref, gain_ref, o_ref, h_ref):
    @pl.when(pl.program_id(1) == 0)
    def _():
        x = x_ref[...]
        y = x * lax.rsqrt(jnp.mean(x * x, axis=-1, keepdims=True) + EPS) * g_ref[...]
        h_ref[...] = (y * (1.0 + sc_ref[0]) + sh_ref[0]).astype(jnp.bfloat16)

    acc = jnp.dot(h_ref[...], w_ref[...], preferred_element_type=jnp.float32)
    tn = acc.shape[1]
    for k in range(tn // LANES):
        a = acc[:, k * LANES:(k + 1) * LANES]
        f = flag_ref[:, k * LANES:(k + 1) * LANES]
        r = lax.rsqrt(jnp.mean(a * a, axis=-1, keepdims=True) + EPS)
        scale = f * r + (1.0 - f)
        o_ref[:, k * LANES:(k + 1) * LANES] = (a * scale * gain_ref[:, k * LANES:(k + 1) * LANES]).astype(o_ref.dtype)


def _projection(x2d, norm_g, sc, sh, w_bf16, flags, gains, seq):
    t, d = x2d.shape
    n = w_bf16.shape[1]
    tm = min(1024, seq)
    tn = 512
    per_batch = seq // tm
    return pl.pallas_call(
        _proj_kernel,
        grid=(t // tm, n // tn),
        in_specs=[pl.BlockSpec((tm, d), lambda i, j: (i, 0)),
                  pl.BlockSpec((1, d), lambda i, j: (0, 0)),
                  pl.BlockSpec((1, 1, d), lambda i, j: (i // per_batch, 0, 0)),
                  pl.BlockSpec((1, 1, d), lambda i, j: (i // per_batch, 0, 0)),
                  pl.BlockSpec((d, tn), lambda i, j: (0, j)),
                  pl.BlockSpec((1, tn), lambda i, j: (0, j)),
                  pl.BlockSpec((1, tn), lambda i, j: (0, j))],
        out_specs=pl.BlockSpec((tm, tn), lambda i, j: (i, j)),
        out_shape=jax.ShapeDtypeStruct((t, n), jnp.bfloat16),
        scratch_shapes=[pltpu.VMEM((tm, d), jnp.bfloat16)],
        compiler_params=_cparams(("arbitrary", "arbitrary")),
        name="norm_in_proj",
    )(x2d, norm_g.reshape(1, d), sc, sh, w_bf16, flags, gains)


def _bias_kernel(tbl_ref, idx_ref, o_ref, *, head0):
    h = pl.program_id(0) + head0
    b = idx_ref[0]
    out = jnp.full(b.shape, NEG, jnp.float32)
    for k in range(REL_BUCKETS):
        out = jnp.where(b == k, tbl_ref[h * REL_BUCKETS + k], out)
    o_ref[0, 0] = out


def _bias_table(rel_bias_flat, bucket_idx, head0, n_heads):
    nt, r, c = bucket_idx.shape
    return pl.pallas_call(
        functools.partial(_bias_kernel, head0=head0),
        grid=(n_heads, nt),
        in_specs=[pl.BlockSpec(memory_space=pltpu.SMEM),
                  pl.BlockSpec((1, r, c), lambda h, t: (t, 0, 0))],
        out_specs=pl.BlockSpec((1, 1, r, c), lambda h, t: (h, t, 0, 0)),
        out_shape=jax.ShapeDtypeStruct((n_heads, nt, r, c), jnp.float32),
        compiler_params=_cparams(("arbitrary", "arbitrary")),
        name="rel_bias_table",
    )(rel_bias_flat, jnp.asarray(bucket_idx))


def _band_buckets(n_prev, span, dist_scale):
    i = np.arange(QT)[:, None]
    c = np.arange((n_prev + 1) * QT)[None, :]
    dist = n_prev * QT + i - c
    ok = (dist >= 0) & (dist <= span)
    return np.where(ok, _t5_bucket_np(dist * dist_scale), MASKED_BUCKET).astype(np.int32)[None]


def _causal_buckets(n_tiles):
    i = np.arange(QT)[:, None]
    j = np.arange(QT)[None, :]
    out = []
    for o in range(n_tiles):
        dist = QT * o + i - j
        out.append(np.where(dist >= 0, _t5_bucket_np(dist), MASKED_BUCKET))
    return np.stack(out).astype(np.int32)


def _n_causal_tiles(seq):
    first_last = int(np.argmax(_t5_bucket_np(np.arange(4 * REL_MAX_DIST)) == REL_BUCKETS - 1))
    o_const = -(-(first_last + QT - 1) // QT)
    return min(seq // QT, o_const + 1)


def _cmp_buckets(seq, n_cmp_pad):
    nq = seq // QT
    n_cmp = (seq - NSA_CMP_LEN) // NSA_CMP_STRIDE + 1
    i = np.arange(QT)[None, :, None]
    n = np.arange(nq)[:, None, None]
    c = np.arange(n_cmp_pad)[None, None, :]
    dist = n * QT + i - (c * NSA_CMP_STRIDE + NSA_CMP_LEN - 1)
    ok = (dist >= 0) & (c < n_cmp)
    return np.where(ok, _t5_bucket_np(dist), MASKED_BUCKET).astype(np.int32)


def _compress_kernel(t_ref, pos_ref, w1_ref, w2_ref, gain_ref, o_ref):
    half = t_ref.shape[3]
    a = t_ref[0, 0].astype(jnp.float32)
    lo = (a + pos_ref[0, :, :half]).astype(jnp.bfloat16)
    hi = (a + pos_ref[0, :, half:]).astype(jnp.bfloat16)
    p = jnp.dot(lo, w1_ref[0, :half, :], preferred_element_type=jnp.float32)
    q = jnp.dot(hi, w1_ref[0, half:, :], preferred_element_type=jnp.float32)
    n = q.shape[0]
    q_next = jnp.concatenate([q[1:], q[:1]], axis=0)
    hid = _gelu(p + q_next)
    out = jnp.dot(hid.astype(jnp.bfloat16), w2_ref[0], preferred_element_type=jnp.float32)
    is_key = pl.program_id(1) < NSA_KV_GROUPS
    r = lax.rsqrt(jnp.mean(out * out, axis=-1, keepdims=True) + EPS)
    normed = out * r * gain_ref[...]
    o_ref[0, 0] = jnp.where(is_key, normed, out).astype(o_ref.dtype)


def _compress(t_chunks, cmp_pos, cmp_w1, cmp_w2, k_gain0):
    b, nh, nchunk, width = t_chunks.shape
    dh = HEAD_DIM
    g = NSA_KV_GROUPS
    return pl.pallas_call(
        _compress_kernel,
        grid=(b, nh),
        in_specs=[pl.BlockSpec((1, 1, nchunk, width), lambda i, h: (i, h, 0, 0)),
                  pl.BlockSpec((1, 1, 2 * width), lambda i, h: (h // g, 0, 0)),
                  pl.BlockSpec((1, 2 * width, dh), lambda i, h: (h // g, 0, 0)),
                  pl.BlockSpec((1, dh, dh), lambda i, h: (h // g, 0, 0)),
                  pl.BlockSpec((1, dh), lambda i, h: (0, 0))],
        out_specs=pl.BlockSpec((1, 1, nchunk, dh), lambda i, h: (i, h, 0, 0)),
        out_shape=jax.ShapeDtypeStruct((b, nh, nchunk, dh), jnp.bfloat16),
        compiler_params=_cparams(("arbitrary", "arbitrary")),
        name="nsa_compress",
    )(t_chunks, cmp_pos.reshape(2, 1, 2 * width), cmp_w1.astype(jnp.bfloat16),
      cmp_w2.astype(jnp.bfloat16), k_gain0.reshape(1, dh))


def _stack_heads(q):
    return jnp.concatenate([q[:, r * HEAD_DIM:(r + 1) * HEAD_DIM] for r in range(NSA_REP)], axis=0)


def _cmp_select_kernel(q_ref, kc_ref, vc_ref, bias_ref, ov_ref, o_ref, sel_ref, *, n_sel):
    tq = q_ref.shape[1]
    n = pl.program_id(2)
    scale = HEAD_DIM ** -0.5
    qs = _stack_heads(q_ref[0])
    s = lax.dot_general(qs, kc_ref[0, 0], (((1,), (1,)), ((), ())),
                        preferred_element_type=jnp.float32) * scale
    bias = jnp.concatenate([bias_ref[r, 0] for r in range(NSA_REP)], axis=0)
    valid = bias > 0.5 * NEG
    s = jnp.where(valid, s + bias, NEG)
    m = jnp.max(s, axis=-1, keepdims=True)
    e = jnp.exp(s - m)
    p = e / jnp.sum(e, axis=-1, keepdims=True) * valid.astype(jnp.float32)
    o = jnp.dot(p.astype(jnp.bfloat16), vc_ref[0, 0], preferred_element_type=jnp.float32)
    for r in range(NSA_REP):
        o_ref[0, :, r * HEAD_DIM:(r + 1) * HEAD_DIM] = o[r * tq:(r + 1) * tq].astype(o_ref.dtype)

    psum = p[0:tq]
    for r in range(1, NSA_REP):
        psum = psum + p[r * tq:(r + 1) * tq]
    imp = jnp.dot(psum, ov_ref[...], preferred_element_type=jnp.float32,
                  precision=lax.Precision.HIGHEST)
    n_blocks = sel_ref.shape[3]
    jb = lax.broadcasted_iota(jnp.int32, (tq, n_blocks), 1).astype(jnp.float32)
    row = lax.broadcasted_iota(jnp.int32, (tq, n_blocks), 0)
    cur = ((n * tq + row) // NSA_SEL_LEN).astype(jnp.float32)
    forced = (jb == 0.0) | (jb == cur) | (jb == cur - 1.0)
    score = jnp.where(forced, SEL_FORCE, jnp.where(jb > cur, -SEL_FORCE, imp))
    chosen = jnp.zeros((tq, n_blocks), jnp.float32)
    for _ in range(n_sel):
        mx = jnp.max(score, axis=-1, keepdims=True)
        first = jnp.min(jnp.where(score == mx, jb, float(n_blocks)), axis=-1, keepdims=True)
        hit = jb == first
        chosen = jnp.where(hit, 1.0, chosen)
        score = jnp.where(hit, -jnp.inf, score)
    sel_ref[0, 0] = chosen


def _cmp_select(proj3, kcvc, cbias, overlap, lay, seq):
    b = proj3.shape[0]
    g = NSA_KV_GROUPS
    ncp = kcvc.shape[2]
    ns = seq // NSA_SEL_LEN
    n_sel = min(NSA_TOP_N, ns)
    qw = NSA_REP * HEAD_DIM
    return pl.pallas_call(
        functools.partial(_cmp_select_kernel, n_sel=n_sel),
        grid=(b, g, seq // QT),
        in_specs=[pl.BlockSpec((1, QT, qw), lambda i, j, n: (i, n, lay["q"] // NSA_REP + j)),
                  pl.BlockSpec((1, 1, ncp, HEAD_DIM), lambda i, j, n: (i, j, 0, 0)),
                  pl.BlockSpec((1, 1, ncp, HEAD_DIM), lambda i, j, n: (i, g + j, 0, 0)),
                  pl.BlockSpec((NSA_REP, 1, QT, ncp), lambda i, j, n: (j, n, 0, 0)),
                  pl.BlockSpec((ncp, ns), lambda i, j, n: (0, 0))],
        out_specs=[pl.BlockSpec((1, QT, qw), lambda i, j, n: (i, n, j)),
                   pl.BlockSpec((1, 1, QT, ns), lambda i, j, n: (i, j, n, 0))],
        out_shape=[jax.ShapeDtypeStruct((b, seq, NSA_HEADS * HEAD_DIM), jnp.bfloat16),
                   jax.ShapeDtypeStruct((b, g, seq, ns), jnp.float32)],
        compiler_params=_cparams(("arbitrary", "arbitrary", "arbitrary")),
        name="nsa_cmp_select",
    )(proj3, kcvc, kcvc, cbias, overlap)


def _sel_attn_kernel(q_ref, k_ref, v_ref, sel_ref, exp_ref, bias_ref, o_ref, kmask_ref):
    tq = q_ref.shape[1]
    n = pl.program_id(2)
    scale = HEAD_DIM ** -0.5
    n_bias = bias_ref.shape[1]
    qs = _stack_heads(q_ref[0])
    kmask_ref[...] = jnp.dot(sel_ref[0, 0].astype(jnp.bfloat16), exp_ref[...],
                             preferred_element_type=jnp.float32)

    def body(kt, carry):
        m, l, acc = carry
        start = pl.multiple_of(kt * QT, QT)
        k = k_ref[0, pl.ds(start, QT), :]
        v = v_ref[0, pl.ds(start, QT), :]
        s = lax.dot_general(qs, k, (((1,), (1,)), ((), ())), preferred_element_type=jnp.float32) * scale
        o = jnp.minimum(n - kt, n_bias - 1)
        bias = jnp.concatenate([bias_ref[r, o] for r in range(NSA_REP)], axis=0)
        km = kmask_ref[:, pl.ds(start, QT)]
        km = jnp.concatenate([km] * NSA_REP, axis=0)
        s = jnp.where((km > 0.5) & (bias > 0.5 * NEG), s + bias, NEG)
        m_new = jnp.maximum(m, jnp.max(s, axis=-1, keepdims=True))
        alpha = jnp.exp(m - m_new)
        p = jnp.exp(s - m_new)
        l = alpha * l + jnp.sum(p, axis=-1, keepdims=True)
        acc = alpha * acc + jnp.dot(p.astype(jnp.bfloat16), v, preferred_element_type=jnp.float32)
        return m_new, l, acc

    rows = NSA_REP * tq
    init = (jnp.full((rows, 1), NEG, jnp.float32), jnp.zeros((rows, 1), jnp.float32),
            jnp.zeros((rows, HEAD_DIM), jnp.float32))
    m, l, acc = lax.fori_loop(0, n + 1, body, init)
    o = acc / l
    for r in range(NSA_REP):
        o_ref[0, :, r * HEAD_DIM:(r + 1) * HEAD_DIM] = o[r * tq:(r + 1) * tq].astype(o_ref.dtype)


def _sel_attention(proj3, sel, expand, bias_sel, lay, seq):
    b = proj3.shape[0]
    g = NSA_KV_GROUPS
    ns = seq // NSA_SEL_LEN
    qw = NSA_REP * HEAD_DIM
    nb = bias_sel.shape[1]
    cb_k = lay["kv"] + 2 * g
    cb_v = lay["kv"] + 3 * g
    return pl.pallas_call(
        _sel_attn_kernel,
        grid=(b, g, seq // QT),
        in_specs=[pl.BlockSpec((1, QT, qw), lambda i, j, n: (i, n, lay["q"] // NSA_REP + j)),
                  pl.BlockSpec((1, seq, HEAD_DIM), lambda i, j, n: (i, 0, cb_k + j)),
                  pl.BlockSpec((1, seq, HEAD_DIM), lambda i, j, n: (i, 0, cb_v + j)),
                  pl.BlockSpec((1, 1, QT, ns), lambda i, j, n: (i, j, n, 0)),
                  pl.BlockSpec((ns, seq), lambda i, j, n: (0, 0)),
                  pl.BlockSpec((NSA_REP, nb, QT, QT), lambda i, j, n: (j, 0, 0, 0))],
        out_specs=pl.BlockSpec((1, QT, qw), lambda i, j, n: (i, n, j)),
        out_shape=jax.ShapeDtypeStruct((b, seq, NSA_HEADS * HEAD_DIM), jnp.bfloat16),
        scratch_shapes=[pltpu.VMEM((QT, seq), jnp.float32)],
        compiler_params=_cparams(("arbitrary", "arbitrary", "arbitrary")),
        name="nsa_selected",
    )(proj3, proj3, proj3, sel, expand, bias_sel)


def _band_softmax(qs, k_ref, v_ref, bias, n, n_prev, scale):
    logits = []
    for j in range(n_prev + 1):
        kb = n - n_prev + j
        start = pl.multiple_of(jnp.maximum(kb, 0) * QT, QT)
        k = k_ref[0, pl.ds(start, QT), :]
        s = lax.dot_general(qs, k, (((1,), (1,)), ((), ())), preferred_element_type=jnp.float32) * scale
        bj = jnp.where(kb >= 0, bias[:, j * QT:(j + 1) * QT], NEG)
        logits.append(jnp.where(bj > 0.5 * NEG, s + bj, NEG))
    m = logits[0].max(axis=-1, keepdims=True)
    for s in logits[1:]:
        m = jnp.maximum(m, s.max(axis=-1, keepdims=True))
    l = jnp.zeros_like(m)
    acc = jnp.zeros((qs.shape[0], HEAD_DIM), jnp.float32)
    for j, s in enumerate(logits):
        kb = n - n_prev + j
        start = pl.multiple_of(jnp.maximum(kb, 0) * QT, QT)
        p = jnp.exp(s - m)
        l = l + jnp.sum(p, axis=-1, keepdims=True)
        acc = acc + jnp.dot(p.astype(jnp.bfloat16), v_ref[0, pl.ds(start, QT), :],
                            preferred_element_type=jnp.float32)
    return acc, m, l


def _win_attn_kernel(q_ref, k_ref, v_ref, bias_ref, o_ref, *, n_prev):
    tq = q_ref.shape[1]
    n = pl.program_id(2)
    qs = _stack_heads(q_ref[0])
    bias = jnp.concatenate([bias_ref[r, 0] for r in range(NSA_REP)], axis=0)
    acc, _, l = _band_softmax(qs, k_ref, v_ref, bias, n, n_prev, HEAD_DIM ** -0.5)
    o = acc / l
    for r in range(NSA_REP):
        o_ref[0, :, r * HEAD_DIM:(r + 1) * HEAD_DIM] = o[r * tq:(r + 1) * tq].astype(o_ref.dtype)


def _win_attention(proj3, bias_win, lay, seq):
    b = proj3.shape[0]
    g = NSA_KV_GROUPS
    qw = NSA_REP * HEAD_DIM
    n_prev = -(-(NSA_WINDOW - 1) // QT)
    cb_k = lay["kv"] + 4 * g
    cb_v = lay["kv"] + 5 * g
    return pl.pallas_call(
        functools.partial(_win_attn_kernel, n_prev=n_prev),
        grid=(b, g, seq // QT),
        in_specs=[pl.BlockSpec((1, QT, qw), lambda i, j, n: (i, n, lay["q"] // NSA_REP + j)),
                  pl.BlockSpec((1, seq, HEAD_DIM), lambda i, j, n: (i, 0, cb_k + j)),
                  pl.BlockSpec((1, seq, HEAD_DIM), lambda i, j, n: (i, 0, cb_v + j)),
                  pl.BlockSpec((NSA_REP, 1, QT, (n_prev + 1) * QT), lambda i, j, n: (j, 0, 0, 0))],
        out_specs=pl.BlockSpec((1, QT, qw), lambda i, j, n: (i, n, j)),
        out_shape=jax.ShapeDtypeStruct((b, seq, NSA_HEADS * HEAD_DIM), jnp.bfloat16),
        compiler_params=_cparams(("arbitrary", "arbitrary", "arbitrary")),
        name="nsa_window",
    )(proj3, proj3, proj3, bias_win)


def _dil_attn_kernel(q_ref, k_ref, v_ref, bias_ref, o_ref, lse_ref, *, n_prev):
    n = pl.program_id(3)
    acc, m, l = _band_softmax(q_ref[0], k_ref, v_ref, bias_ref[0, 0], n, n_prev, HEAD_DIM ** -0.5)
    o_ref[0] = (acc / l).astype(o_ref.dtype)
    lse_ref[0] = jnp.broadcast_to(m + jnp.log(l), lse_ref.shape[1:])


def _dil_attention(proj3, bias_dil, lay, seq, gi):
    b = proj3.shape[0]
    window, dil = DIL_CONFIGS[gi]
    p_heads = DIL_HEADS_PER_GROUP
    length = seq // dil
    n_prev = -(-(window // dil) // QT)
    nblk = lay["n"]
    view = proj3.reshape(b, length, dil * nblk * LANES)
    cb_q = lay["dil"] + gi * p_heads
    cb_k = cb_q + DIL_HEADS
    cb_v = cb_k + DIL_HEADS
    out_w = p_heads * HEAD_DIM
    o, lse = pl.pallas_call(
        functools.partial(_dil_attn_kernel, n_prev=n_prev),
        grid=(b, p_heads, dil, length // QT),
        in_specs=[pl.BlockSpec((1, QT, HEAD_DIM), lambda i, p, r, n: (i, n, r * nblk + cb_q + p)),
                  pl.BlockSpec((1, length, HEAD_DIM), lambda i, p, r, n: (i, 0, r * nblk + cb_k + p)),
                  pl.BlockSpec((1, length, HEAD_DIM), lambda i, p, r, n: (i, 0, r * nblk + cb_v + p)),
                  pl.BlockSpec((1, 1, QT, (n_prev + 1) * QT), lambda i, p, r, n: (p, 0, 0, 0))],
        out_specs=[pl.BlockSpec((1, QT, HEAD_DIM), lambda i, p, r, n: (i, n, r * p_heads + p)),
                   pl.BlockSpec((1, QT, HEAD_DIM), lambda i, p, r, n: (i, n, r * p_heads + p))],
        out_shape=[jax.ShapeDtypeStruct((b, length, dil * out_w), jnp.bfloat16),
                   jax.ShapeDtypeStruct((b, length, dil * out_w), jnp.float32)],
        compiler_params=_cparams(("arbitrary",) * 4),
        name=f"dilated_attn_{gi}",
    )(view, view, view, bias_dil)
    return o.reshape(b, seq, out_w), lse.reshape(b, seq, out_w)


def _post_kernel(ocmp_ref, osel_ref, owin_ref, gate_ref, od0_ref, od1_ref, od2_ref, l0_ref, l1_ref, l2_ref,
                 mg1_ref, mg2_ref, x_ref, g1_ref, sc2_ref, sh2_ref, n2_ref, wn_ref, wd_ref, wo_ref,
                 x1_ref, h2_ref):
    gates = _sigmoid(gate_ref[...].astype(jnp.float32))
    parts = []
    for h in range(NSA_HEADS):
        sl = slice(h * HEAD_DIM, (h + 1) * HEAD_DIM)
        y = (gates[:, 3 * h:3 * h + 1] * ocmp_ref[:, sl].astype(jnp.float32)
             + gates[:, 3 * h + 1:3 * h + 2] * osel_ref[:, sl].astype(jnp.float32)
             + gates[:, 3 * h + 2:3 * h + 3] * owin_ref[:, sl].astype(jnp.float32))
        parts.append(y.astype(jnp.bfloat16))
    y_nsa = jnp.concatenate(parts, axis=1)

    l0, l1, l2 = l0_ref[...], l1_ref[...], l2_ref[...]
    mx = jnp.maximum(jnp.maximum(l0, l1), l2)
    e0, e1, e2 = jnp.exp(l0 - mx), jnp.exp(l1 - mx), jnp.exp(l2 - mx)
    den = e0 + e1 + e2
    y_dil = ((e0 / den) * od0_ref[...].astype(jnp.float32) + (e1 / den) * od1_ref[...].astype(jnp.float32)
             + (e2 / den) * od2_ref[...].astype(jnp.float32)).astype(jnp.bfloat16)

    a = jnp.dot(y_nsa, wn_ref[...], preferred_element_type=jnp.float32)
    bb = jnp.dot(y_dil, wd_ref[...], preferred_element_type=jnp.float32)
    merged = (_sigmoid(mg1_ref[...].astype(jnp.float32)) * a
              + _sigmoid(mg2_ref[...].astype(jnp.float32)) * bb).astype(jnp.bfloat16)
    mix = jnp.dot(merged, wo_ref[...], preferred_element_type=jnp.float32)
    x1 = x_ref[...] + g1_ref[0] * mix
    x1_ref[...] = x1
    y = x1 * lax.rsqrt(jnp.mean(x1 * x1, axis=-1, keepdims=True) + EPS) * n2_ref[...]
    h2_ref[...] = y * (1.0 + sc2_ref[0]) + sh2_ref[0]


def _post(ocmp, osel, owin, proj2d, odil, lsedil, x2d, g1, sc2, sh2, norm2_g, wn, wd, wo, lay, seq):
    t, d = x2d.shape
    tm = 256
    per_batch = seq // tm
    nw = NSA_HEADS * HEAD_DIM
    dw = DIL_HEADS_PER_GROUP * HEAD_DIM
    row = lambda w: pl.BlockSpec((tm, w), lambda i: (i, 0))
    mod = pl.BlockSpec((1, 1, d), lambda i: (i // per_batch, 0, 0))
    full = lambda a: pl.BlockSpec(a.shape, lambda i: (0, 0))
    return pl.pallas_call(
        _post_kernel,
        grid=(t // tm,),
        in_specs=[row(nw), row(nw), row(nw),
                  pl.BlockSpec((tm, LANES), lambda i: (i, lay["gate"])),
                  row(dw), row(dw), row(dw), row(dw), row(dw), row(dw),
                  pl.BlockSpec((tm, d), lambda i: (i, 0)),
                  pl.BlockSpec((tm, d), lambda i: (i, 1)),
                  row(d), mod, mod, mod,
                  pl.BlockSpec((1, d), lambda i: (0, 0)),
                  full(wn), full(wd), full(wo)],
        out_specs=[row(d), row(d)],
        out_shape=[jax.ShapeDtypeStruct((t, d), jnp.float32), jax.ShapeDtypeStruct((t, d), jnp.float32)],
        compiler_params=_cparams(("arbitrary",)),
        name="merge_out_proj",
    )(ocmp, osel, owin, proj2d, odil[0], odil[1], odil[2], lsedil[0], lsedil[1], lsedil[2],
      proj2d, proj2d, x2d, g1, sc2, sh2, norm2_g.reshape(1, d), wn, wd, wo)


def _topk_rows(sc, k, payload=None):
    n = sc.shape[0]
    row = lax.broadcasted_iota(jnp.int32, sc.shape, 0).astype(jnp.float32)
    vals, picks = [], []
    for _ in range(k):
        mx = jnp.max(sc, axis=0, keepdims=True)
        first = jnp.min(jnp.where(sc == mx, row, float(n)), axis=0, keepdims=True)
        hit = row == first
        vals.append(mx)
        if payload is None:
            picks.append(first)
        else:
            picks.append(jnp.sum(jnp.where(hit, payload, 0.0), axis=0, keepdims=True))
        sc = jnp.where(hit, -jnp.inf, sc)
    return jnp.concatenate(vals, axis=0), jnp.concatenate(picks, axis=0)


def _route_kernel(h_ref, wq_ref, gain_ref, keys_ref, idx_ref, gate_ref):
    q = jnp.dot(h_ref[...].astype(jnp.bfloat16), wq_ref[...], preferred_element_type=jnp.float32)
    k = PEER_TOPK
    for head in range(PEER_HEADS):
        tops = []
        for part in range(2):
            hp = head * 2 + part
            a = q[:, hp * PEER_HALF:(hp + 1) * PEER_HALF]
            qn = (a * lax.rsqrt(jnp.mean(a * a, axis=-1, keepdims=True) + EPS)
                  * gain_ref[hp:hp + 1, :]).astype(jnp.bfloat16)
            sc = lax.dot_general(keys_ref[hp], qn, (((1,), (1,)), ((), ())),
                                 preferred_element_type=jnp.float32)
            tops.append(_topk_rows(sc, k))
        (s1, i1), (s2, i2) = tops
        cand = jnp.concatenate([s1[a:a + 1] + s2 for a in range(k)], axis=0)
        cand_idx = jnp.concatenate([i1[a:a + 1] * float(PEER_N_KEYS) + i2 for a in range(k)], axis=0)
        top_s, expert = _topk_rows(cand, k, payload=cand_idx)
        e = jnp.exp(top_s - top_s[0:1])
        gate = e / jnp.sum(e, axis=0, keepdims=True)
        idx_ref[head * k:(head + 1) * k, :] = expert.astype(jnp.int32)
        gate_ref[head * k:(head + 1) * k, :] = gate


def _route(h2, wq_bf16, q_gain, sub_keys_bf16):
    t, d = h2.shape
    tm = 256
    hk = PEER_HEADS * PEER_TOPK
    nq = wq_bf16.shape[1]
    return pl.pallas_call(
        _route_kernel,
        grid=(t // tm,),
        in_specs=[pl.BlockSpec((tm, d), lambda i: (i, 0)),
                  pl.BlockSpec((d, nq), lambda i: (0, 0)),
                  pl.BlockSpec((2 * PEER_HEADS, PEER_HALF), lambda i: (0, 0)),
                  pl.BlockSpec((2 * PEER_HEADS, PEER_N_KEYS, PEER_HALF), lambda i: (0, 0, 0))],
        out_specs=[pl.BlockSpec((hk, tm), lambda i: (0, i)),
                   pl.BlockSpec((hk, tm), lambda i: (0, i))],
        out_shape=[jax.ShapeDtypeStruct((hk, t), jnp.int32), jax.ShapeDtypeStruct((hk, t), jnp.float32)],
        compiler_params=_cparams(("arbitrary",)),
        name="peer_route",
    )(h2, wq_bf16, q_gain.reshape(2 * PEER_HEADS, PEER_HALF), sub_keys_bf16)


PEER_TB = 32
PEER_DCHUNK = 512


def _expert_kernel(idx_ref, h_ref, gate_ref, x1_ref, g2_ref, tab_ref, o_ref, stage_ref, act_ref, sem):
    tb, d = h_ref.shape
    k = PEER_TOPK
    dc = min(PEER_DCHUNK, d)

    def row_copy(head, buf, t, j):
        e = idx_ref[0, 0, t * (PEER_HEADS * k) + head * k + j]
        return pltpu.make_async_copy(tab_ref.at[pl.ds(e, 1), :],
                                     stage_ref.at[buf, pl.ds(j * tb + t, 1), :], sem.at[buf])

    def issue(head, buf):
        def body(t, c):
            for j in range(k):
                row_copy(head, buf, t, j).start()
            return c
        lax.fori_loop(0, tb, body, 0)

    def drain(head, buf):
        def body(t, c):
            for j in range(k):
                row_copy(head, buf, t, j).wait()
            return c
        lax.fori_loop(0, tb, body, 0)

    hi_mask = jnp.uint32(0xFFFF0000)
    issue(0, 0)
    for head in range(PEER_HEADS):
        buf = head % 2
        if head + 1 < PEER_HEADS:
            issue(head + 1, 1 - buf)
        drain(head, buf)
        pres = []
        for j in range(k):
            pre = jnp.zeros((tb, 1), jnp.float32)
            for c in range(d // dc):
                cs = slice(c * dc, (c + 1) * dc)
                w = stage_ref[buf, j * tb:(j + 1) * tb, cs]
                u = lax.bitcast_convert_type(w & hi_mask, jnp.float32)
                pre = pre + jnp.sum(h_ref[:, cs] * u, axis=-1, keepdims=True)
            pres.append(pre)
        pre = jnp.concatenate(pres, axis=1)
        act_ref[...] = gate_ref[:, head * k:(head + 1) * k] * _gelu(pre)
        for c in range(d // dc):
            cs = slice(c * dc, (c + 1) * dc)
            acc = jnp.zeros((tb, dc), jnp.float32)
            for j in range(k):
                w = stage_ref[buf, j * tb:(j + 1) * tb, cs]
                v = lax.bitcast_convert_type(w << 16, jnp.float32)
                acc = acc + act_ref[:, j:j + 1] * v
            if head == 0:
                o_ref[:, cs] = acc
            else:
                o_ref[:, cs] = o_ref[:, cs] + acc
    o_ref[...] = x1_ref[...] + g2_ref[0] * o_ref[...]


def _experts(idx_tok, gate_tok, h2, x1, g2, table, seq):
    t, d = h2.shape
    tb = PEER_TB
    hk = PEER_HEADS * PEER_TOPK
    per_batch = seq // tb
    return pl.pallas_call(
        _expert_kernel,
        grid=(t // tb,),
        in_specs=[pl.BlockSpec((1, 1, tb * hk), lambda i: (i, 0, 0), memory_space=pltpu.SMEM),
                  pl.BlockSpec((tb, d), lambda i: (i, 0)),
                  pl.BlockSpec((tb, hk), lambda i: (i, 0)),
                  pl.BlockSpec((tb, d), lambda i: (i, 0)),
                  pl.BlockSpec((1, 1, d), lambda i: (i // per_batch, 0, 0)),
                  pl.BlockSpec(memory_space=pl.ANY)],
        out_specs=pl.BlockSpec((tb, d), lambda i: (i, 0)),
        out_shape=jax.ShapeDtypeStruct((t, d), jnp.float32),
        scratch_shapes=[pltpu.VMEM((2, PEER_TOPK * tb, d), jnp.uint32),
                        pltpu.VMEM((tb, PEER_TOPK), jnp.float32),
                        pltpu.SemaphoreType.DMA((2,))],
        compiler_params=_cparams(("arbitrary",)),
        name="peer_experts",
    )(idx_tok.reshape(t // tb, 1, tb * hk), h2, gate_tok, x1, g2, table)


def _pack_expert_table(u, v):
    ub = lax.bitcast_convert_type(u.astype(jnp.bfloat16), jnp.uint16).astype(jnp.uint32)
    vb = lax.bitcast_convert_type(v.astype(jnp.bfloat16), jnp.uint16).astype(jnp.uint32)
    return (ub << 16) | vb


def _proj_weights(w_in, nsa_q_gain, nsa_k_gain, dil_q_gain, dil_k_gain, d_model, lay):
    dh = HEAD_DIM
    off_kv = NSA_HEADS * dh
    off_gate = off_kv + 6 * NSA_KV_GROUPS * dh
    off_dil = off_gate + 3 * NSA_HEADS
    off_merge = off_dil + 3 * DIL_HEADS * dh
    n_cols = lay["n"] * LANES
    pad = n_cols - (lay["gate"] * LANES + 3 * NSA_HEADS)
    w = jnp.concatenate([w_in[:, off_merge:], w_in[:, :off_kv], w_in[:, off_kv:off_gate],
                         w_in[:, off_dil:off_merge], w_in[:, off_gate:off_dil],
                         jnp.zeros((d_model, pad), w_in.dtype)], axis=1).astype(jnp.bfloat16)
    ones = jnp.ones((dh,), jnp.float32)
    gains, flags = [], []

    def add(gain, flag, count=1):
        for _ in range(count):
            gains.append(gain)
            flags.append(jnp.full((dh,), flag, jnp.float32))

    add(ones, 0.0, lay["q"])
    add(nsa_q_gain, 1.0, NSA_HEADS)
    for i in range(6):
        if i == 2:
            add(nsa_k_gain[1], 1.0, NSA_KV_GROUPS)
        elif i == 4:
            add(nsa_k_gain[2], 1.0, NSA_KV_GROUPS)
        else:
            add(ones, 0.0, NSA_KV_GROUPS)
    add(dil_q_gain, 1.0, DIL_HEADS)
    add(dil_k_gain, 1.0, DIL_HEADS)
    add(ones, 0.0, DIL_HEADS)
    add(ones, 0.0, lay["n"] - lay["gate"])
    return w, jnp.concatenate(flags).reshape(1, n_cols), jnp.concatenate(gains).reshape(1, n_cols)


def _token_mixer_and_norm2(x, mod, norm1_g, w_in, nsa_q_gain, nsa_k_gain, cmp_pos, cmp_w1, cmp_w2,
                           dil_q_gain, dil_k_gain, w_br_nsa, w_br_dil, w_out, norm2_g, rel_bias):
    b, seq, d = x.shape
    t = b * seq
    lay = _layout(d)
    sh1, sc1, g1, sh2, sc2, g2 = [m.reshape(b, 1, d) for m in jnp.split(mod, 6, axis=-1)]
    x2d = x.reshape(t, d)

    w, flags, gains = _proj_weights(w_in, nsa_q_gain, nsa_k_gain, dil_q_gain, dil_k_gain, d, lay)
    proj2d = _projection(x2d, norm1_g, sc1, sh1, w, flags, gains, seq)
    proj3 = proj2d.reshape(b, seq, lay["n"] * LANES)

    rel_flat = rel_bias.reshape(-1)
    nchunk = seq // NSA_CMP_STRIDE
    ns = seq // NSA_SEL_LEN
    bias_win = _bias_table(rel_flat, _band_buckets(-(-(NSA_WINDOW - 1) // QT), NSA_WINDOW - 1, 1), 0, NSA_HEADS)
    bias_sel = _bias_table(rel_flat, _causal_buckets(_n_causal_tiles(seq)), 0, NSA_HEADS)
    bias_cmp = _bias_table(rel_flat, _cmp_buckets(seq, nchunk), 0, NSA_HEADS)
    bias_dil = [_bias_table(rel_flat, _band_buckets(-(-(wnd // dil) // QT), wnd // dil, dil),
                            NSA_HEADS + gi * DIL_HEADS_PER_GROUP, DIL_HEADS_PER_GROUP)
                for gi, (wnd, dil) in enumerate(DIL_CONFIGS)]

    g = NSA_KV_GROUPS
    c0 = lay["kv"] * LANES
    tc = proj3[:, :, c0:c0 + 2 * g * HEAD_DIM].reshape(b, nchunk, NSA_CMP_STRIDE, 2 * g, HEAD_DIM)
    tc = tc.transpose(0, 3, 1, 2, 4).reshape(b, 2 * g, nchunk, NSA_CMP_STRIDE * HEAD_DIM)
    kcvc = _compress(tc, cmp_pos, cmp_w1, cmp_w2, nsa_k_gain[0])

    cstart = np.arange(nchunk) * NSA_CMP_STRIDE
    sstart = np.arange(ns) * NSA_SEL_LEN
    n_cmp = (seq - NSA_CMP_LEN) // NSA_CMP_STRIDE + 1
    overlap = ((cstart[:, None] < sstart[None, :] + NSA_SEL_LEN) & (cstart[:, None] + NSA_CMP_LEN > sstart[None, :])
               & (np.arange(nchunk)[:, None] < n_cmp)).astype(np.float32)
    expand = (np.arange(seq)[None, :] // NSA_SEL_LEN == np.arange(ns)[:, None]).astype(np.float32)

    ocmp, sel = _cmp_select(proj3, kcvc, bias_cmp, jnp.asarray(overlap), lay, seq)
    osel = _sel_attention(proj3, sel, jnp.asarray(expand, jnp.bfloat16), bias_sel, lay, seq)
    owin = _win_attention(proj3, bias_win, lay, seq)
    odil, lsedil = zip(*[_dil_attention(proj3, bias_dil[gi], lay, seq, gi) for gi in range(len(DIL_CONFIGS))])

    x1, h2 = _post(ocmp.reshape(t, -1), osel.reshape(t, -1), owin.reshape(t, -1), proj2d,
                   [o.reshape(t, -1) for o in odil], [l.reshape(t, -1) for l in lsedil],
                   x2d, g1, sc2, sh2, norm2_g, w_br_nsa.astype(jnp.bfloat16), w_br_dil.astype(jnp.bfloat16),
                   w_out.astype(jnp.bfloat16), lay, seq)
    return x1, h2, g2


def kernel(x, c, w_ada, b_ada, norm1_g, w_in, nsa_q_gain, nsa_k_gain, cmp_pos, cmp_w1, cmp_w2,
           dil_q_gain, dil_k_gain, w_br_nsa, w_br_dil, w_out, norm2_g, peer_w_q, peer_q_gain,
           peer_sub_keys, peer_u, peer_v, rel_bias):
    b, seq, d = x.shape
    depth = w_ada.shape[0]
    for layer in range(depth):
        mod = _modulation(c, w_ada[layer], b_ada[layer])
        x1, h2, g2 = _token_mixer_and_norm2(
            x, mod, norm1_g[layer], w_in[layer], nsa_q_gain[layer], nsa_k_gain[layer], cmp_pos[layer],
            cmp_w1[layer], cmp_w2[layer], dil_q_gain[layer], dil_k_gain[layer], w_br_nsa[layer],
            w_br_dil[layer], w_out[layer], norm2_g[layer], rel_bias)
        idx, gate = _route(h2, peer_w_q[layer].astype(jnp.bfloat16), peer_q_gain[layer],
                           peer_sub_keys[layer].reshape(2 * PEER_HEADS, PEER_N_KEYS, PEER_HALF).astype(jnp.bfloat16))
        table = _pack_expert_table(peer_u[layer], peer_v[layer])
        out = _experts(idx.T, gate.T, h2, x1, g2, table, seq)
        x = out.reshape(b, seq, d)
    return x
```

```python
import functools
import math

import numpy as np
import jax
import jax.numpy as jnp
from jax import lax
from jax.experimental import pallas as pl
from jax.experimental.pallas import tpu as pltpu

HEAD_DIM = 128
LANES = 128
NSA_HEADS = 8
NSA_KV_GROUPS = 2
NSA_REP = NSA_HEADS // NSA_KV_GROUPS
NSA_CMP_LEN = 32
NSA_CMP_STRIDE = 16
NSA_SEL_LEN = 64
NSA_TOP_N = 16
NSA_WINDOW = 512
DIL_CONFIGS = ((128, 1), (512, 4), (2048, 16))
DIL_HEADS_PER_GROUP = 4
DIL_HEADS = DIL_HEADS_PER_GROUP * len(DIL_CONFIGS)
REL_BUCKETS = 32
REL_MAX_DIST = 2048
PEER_HEADS = 8
PEER_N_KEYS = 128
PEER_TOPK = 16
PEER_HALF = 128
EPS = 1e-6
NEG = -1e30
SEL_FORCE = 1e4
MASKED_BUCKET = REL_BUCKETS

QT = 128
VMEM_LIMIT = 48 * 1024 * 1024

CB_MERGE = 0
def _layout(d_model):
    nmerge = 2 * d_model // LANES
    cb_q = nmerge
    cb_kv = cb_q + NSA_HEADS
    cb_dil = cb_kv + 6 * NSA_KV_GROUPS
    cb_gate = cb_dil + 3 * DIL_HEADS
    n_used = cb_gate + 1
    n_blocks = -(-n_used // 4) * 4
    return dict(q=cb_q, kv=cb_kv, dil=cb_dil, gate=cb_gate, n=n_blocks)


def _cparams(sem):
    return pltpu.CompilerParams(dimension_semantics=sem, vmem_limit_bytes=VMEM_LIMIT)


def _t5_bucket_np(dist):
    n = np.maximum(dist, 0)
    max_exact = REL_BUCKETS // 2
    nf = np.maximum(n, max_exact).astype(np.float32)
    large = max_exact + (np.log(nf / np.float32(max_exact)) / np.float32(math.log(REL_MAX_DIST / max_exact))
                         * np.float32(REL_BUCKETS - max_exact)).astype(np.int32)
    large = np.minimum(large, REL_BUCKETS - 1)
    return np.where(n < max_exact, n, large).astype(np.int32)


def _gelu(x):
    return jax.nn.gelu(x)


def _sigmoid(x):
    return 1.0 / (1.0 + jnp.exp(-x))


def _mod_kernel(c_ref, w_ref, b_ref, o_ref):
    c = c_ref[...]
    cond = c * _sigmoid(c)
    o_ref[...] = jnp.dot(cond.astype(jnp.bfloat16), w_ref[...].astype(jnp.bfloat16),
                         preferred_element_type=jnp.float32) + b_ref[...]


def _modulation(c, w_ada, b_ada):
    b, d = c.shape
    n = w_ada.shape[1]
    rows = 8
    cp = jnp.zeros((rows, d), jnp.float32).at[:b].set(c)
    tn = 512
    out = pl.pallas_call(
        _mod_kernel,
        grid=(n // tn,),
        in_specs=[pl.BlockSpec((rows, d), lambda j: (0, 0)),
                  pl.BlockSpec((d, tn), lambda j: (0, j)),
                  pl.BlockSpec((1, tn), lambda j: (0, j))],
        out_specs=pl.BlockSpec((rows, tn), lambda j: (0, j)),
        out_shape=jax.ShapeDtypeStruct((rows, n), jnp.float32),
        compiler_params=_cparams(("arbitrary",)),
        name="adaln_mod",
    )(cp, w_ada, b_ada.reshape(1, n))
    return out[:b]


def _proj_kernel(x_ref, g_ref, sc_ref, sh_ref, w_ref, flag_ref, gain_ref, o_ref, h_ref):
    @pl.when(pl.program_id(1) == 0)
    def _():
        x = x_ref[...]
        y = x * lax.rsqrt(jnp.mean(x * x, axis=-1, keepdims=True) + EPS) * g_ref[...]
        h_ref[...] = (y * (1.0 + sc_ref[0]) + sh_ref[0]).astype(jnp.bfloat16)

    acc = jnp.dot(h_ref[...], w_ref[...], preferred_element_type=jnp.float32)
    tn = acc.shape[1]
    for k in range(tn // LANES):
        a = acc[:, k * LANES:(k + 1) * LANES]
        f = flag_ref[:, k * LANES:(k + 1) * LANES]
        r = lax.rsqrt(jnp.mean(a * a, axis=-1, keepdims=True) + EPS)
        scale = f * r + (1.0 - f)
        o_ref[:, k * LANES:(k + 1) * LANES] = (a * scale * gain_ref[:, k * LANES:(k + 1) * LANES]).astype(o_ref.dtype)


def _projection(x2d, norm_g, sc, sh, w_bf16, flags, gains, seq):
    t, d = x2d.shape
    n = w_bf16.shape[1]
    tm = min(1024, seq)
    tn = 512
    per_batch = seq // tm
    return pl.pallas_call(
        _proj_kernel,
        grid=(t // tm, n // tn),
        in_specs=[pl.BlockSpec((tm, d), lambda i, j: (i, 0)),
                  pl.BlockSpec((1, d), lambda i, j: (0, 0)),
                  pl.BlockSpec((1, 1, d), lambda i, j: (i // per_batch, 0, 0)),
                  pl.BlockSpec((1, 1, d), lambda i, j: (i // per_batch, 0, 0)),
                  pl.BlockSpec((d, tn), lambda i, j: (0, j)),
                  pl.BlockSpec((1, tn), lambda i, j: (0, j)),
                  pl.BlockSpec((1, tn), lambda i, j: (0, j))],
        out_specs=pl.BlockSpec((tm, tn), lambda i, j: (i, j)),
        out_shape=jax.ShapeDtypeStruct((t, n), jnp.bfloat16),
        scratch_shapes=[pltpu.VMEM((tm, d), jnp.bfloat16)],
        compiler_params=_cparams(("arbitrary", "arbitrary")),
        name="norm_in_proj",
    )(x2d, norm_g.reshape(1, d), sc, sh, w_bf16, flags, gains)


def _bias_kernel(tbl_ref, idx_ref, o_ref, *, head0):
    h = pl.program_id(0) + head0
    b = idx_ref[0]
    out = jnp.full(b.shape, NEG, jnp.float32)
    for k in range(REL_BUCKETS):
        out = jnp.where(b == k, tbl_ref[h * REL_BUCKETS + k], out)
    o_ref[0, 0] = out


def _bias_table(rel_bias_flat, bucket_idx, head0, n_heads):
    nt, r, c = bucket_idx.shape
    return pl.pallas_call(
        functools.partial(_bias_kernel, head0=head0),
        grid=(n_heads, nt),
        in_specs=[pl.BlockSpec(memory_space=pltpu.SMEM),
                  pl.BlockSpec((1, r, c), lambda h, t: (t, 0, 0))],
        out_specs=pl.BlockSpec((1, 1, r, c), lambda h, t: (h, t, 0, 0)),
        out_shape=jax.ShapeDtypeStruct((n_heads, nt, r, c), jnp.float32),
        compiler_params=_cparams(("arbitrary", "arbitrary")),
        name="rel_bias_table",
    )(rel_bias_flat, jnp.asarray(bucket_idx))


def _band_buckets(n_prev, span, dist_scale):
    i = np.arange(QT)[:, None]
    c = np.arange((n_prev + 1) * QT)[None, :]
    dist = n_prev * QT + i - c
    ok = (dist >= 0) & (dist <= span)
    return np.where(ok, _t5_bucket_np(dist * dist_scale), MASKED_BUCKET).astype(np.int32)[None]


def _causal_buckets(n_tiles):
    i = np.arange(QT)[:, None]
    j = np.arange(QT)[None, :]
    out = []
    for o in range(n_tiles):
        dist = QT * o + i - j
        out.append(np.where(dist >= 0, _t5_bucket_np(dist), MASKED_BUCKET))
    return np.stack(out).astype(np.int32)


def _n_causal_tiles(seq):
    first_last = int(np.argmax(_t5_bucket_np(np.arange(4 * REL_MAX_DIST)) == REL_BUCKETS - 1))
    o_const = -(-(first_last + QT - 1) // QT)
    return min(seq // QT, o_const + 1)


def _cmp_buckets(seq, n_cmp_pad):
    nq = seq // QT
    n_cmp = (seq - NSA_CMP_LEN) // NSA_CMP_STRIDE + 1
    i = np.arange(QT)[None, :, None]
    n = np.arange(nq)[:, None, None]
    c = np.arange(n_cmp_pad)[None, None, :]
    dist = n * QT + i - (c * NSA_CMP_STRIDE + NSA_CMP_LEN - 1)
    ok = (dist >= 0) & (c < n_cmp)
    return np.where(ok, _t5_bucket_np(dist), MASKED_BUCKET).astype(np.int32)


def _compress_kernel(t_ref, pos_ref, w1_ref, w2_ref, gain_ref, o_ref):
    half = t_ref.shape[3]
    a = t_ref[0, 0].astype(jnp.float32)
    lo = (a + pos_ref[0, :, :half]).astype(jnp.bfloat16)
    hi = (a + pos_ref[0, :, half:]).astype(jnp.bfloat16)
    p = jnp.dot(lo, w1_ref[0, :half, :], preferred_element_type=jnp.float32)
    q = jnp.dot(hi, w1_ref[0, half:, :], preferred_element_type=jnp.float32)
    n = q.shape[0]
    q_next = jnp.concatenate([q[1:], q[:1]], axis=0)
    hid = _gelu(p + q_next)
    out = jnp.dot(hid.astype(jnp.bfloat16), w2_ref[0], preferred_element_type=jnp.float32)
    is_key = pl.program_id(1) < NSA_KV_GROUPS
    r = lax.rsqrt(jnp.mean(out * out, axis=-1, keepdims=True) + EPS)
    normed = out * r * gain_ref[...]
    o_ref[0, 0] = jnp.where(is_key, normed, out).astype(o_ref.dtype)


def _compress(t_chunks, cmp_pos, cmp_w1, cmp_w2, k_gain0):
    b, nh, nchunk, width = t_chunks.shape
    dh = HEAD_DIM
    g = NSA_KV_GROUPS
    return pl.pallas_call(
        _compress_kernel,
        grid=(b, nh),
        in_specs=[pl.BlockSpec((1, 1, nchunk, width), lambda i, h: (i, h, 0, 0)),
                  pl.BlockSpec((1, 1, 2 * width), lambda i, h: (h // g, 0, 0)),
                  pl.BlockSpec((1, 2 * width, dh), lambda i, h: (h // g, 0, 0)),
                  pl.BlockSpec((1, dh, dh), lambda i, h: (h // g, 0, 0)),
                  pl.BlockSpec((1, dh), lambda i, h: (0, 0))],
        out_specs=pl.BlockSpec((1, 1, nchunk, dh), lambda i, h: (i, h, 0, 0)),
        out_shape=jax.ShapeDtypeStruct((b, nh, nchunk, dh), jnp.bfloat16),
        compiler_params=_cparams(("arbitrary", "arbitrary")),
        name="nsa_compress",
    )(t_chunks, cmp_pos.reshape(2, 1, 2 * width), cmp_w1.astype(jnp.bfloat16),
      cmp_w2.astype(jnp.bfloat16), k_gain0.reshape(1, dh))


def _stack_heads(q):
    return jnp.concatenate([q[:, r * HEAD_DIM:(r + 1) * HEAD_DIM] for r in range(NSA_REP)], axis=0)


def _cmp_select_kernel(q_ref, kc_ref, vc_ref, bias_ref, ov_ref, o_ref, sel_ref, *, n_sel):
    tq = q_ref.shape[1]
    n = pl.program_id(2)
    scale = HEAD_DIM ** -0.5
    qs = _stack_heads(q_ref[0])
    s = lax.dot_general(qs, kc_ref[0, 0], (((1,), (1,)), ((), ())),
                        preferred_element_type=jnp.float32) * scale
    bias = jnp.concatenate([bias_ref[r, 0] for r in range(NSA_REP)], axis=0)
    valid = bias > 0.5 * NEG
    s = jnp.where(valid, s + bias, NEG)
    m = jnp.max(s, axis=-1, keepdims=True)
    e = jnp.exp(s - m)
    p = e / jnp.sum(e, axis=-1, keepdims=True) * valid.astype(jnp.float32)
    o = jnp.dot(p.astype(jnp.bfloat16), vc_ref[0, 0], preferred_element_type=jnp.float32)
    for r in range(NSA_REP):
        o_ref[0, :, r * HEAD_DIM:(r + 1) * HEAD_DIM] = o[r * tq:(r + 1) * tq].astype(o_ref.dtype)

    psum = p[0:tq]
    for r in range(1, NSA_REP):
        psum = psum + p[r * tq:(r + 1) * tq]
    imp = jnp.dot(psum, ov_ref[...], preferred_element_type=jnp.float32,
                  precision=lax.Precision.HIGHEST)
    n_blocks = sel_ref.shape[3]
    jb = lax.broadcasted_iota(jnp.int32, (tq, n_blocks), 1).astype(jnp.float32)
    row = lax.broadcasted_iota(jnp.int32, (tq, n_blocks), 0)
    cur = ((n * tq + row) // NSA_SEL_LEN).astype(jnp.float32)
    forced = (jb == 0.0) | (jb == cur) | (jb == cur - 1.0)
    score = jnp.where(forced, SEL_FORCE, jnp.where(jb > cur, -SEL_FORCE, imp))
    chosen = jnp.zeros((tq, n_blocks), jnp.float32)
    for _ in range(n_sel):
        mx = jnp.max(score, axis=-1, keepdims=True)
        first = jnp.min(jnp.where(score == mx, jb, float(n_blocks)), axis=-1, keepdims=True)
        hit = jb == first
        chosen = jnp.where(hit, 1.0, chosen)
        score = jnp.where(hit, -jnp.inf, score)
    sel_ref[0, 0] = chosen


def _cmp_select(proj3, kcvc, cbias, overlap, lay, seq):
    b = proj3.shape[0]
    g = NSA_KV_GROUPS
    ncp = kcvc.shape[2]
    ns = seq // NSA_SEL_LEN
    n_sel = min(NSA_TOP_N, ns)
    qw = NSA_REP * HEAD_DIM
    return pl.pallas_call(
        functools.partial(_cmp_select_kernel, n_sel=n_sel),
        grid=(b, g, seq // QT),
        in_specs=[pl.BlockSpec((1, QT, qw), lambda i, j, n: (i, n, lay["q"] // NSA_REP + j)),
                  pl.BlockSpec((1, 1, ncp, HEAD_DIM), lambda i, j, n: (i, j, 0, 0)),
                  pl.BlockSpec((1, 1, ncp, HEAD_DIM), lambda i, j, n: (i, g + j, 0, 0)),
                  pl.BlockSpec((NSA_REP, 1, QT, ncp), lambda i, j, n: (j, n, 0, 0)),
                  pl.BlockSpec((ncp, ns), lambda i, j, n: (0, 0))],
        out_specs=[pl.BlockSpec((1, QT, qw), lambda i, j, n: (i, n, j)),
                   pl.BlockSpec((1, 1, QT, ns), lambda i, j, n: (i, j, n, 0))],
        out_shape=[jax.ShapeDtypeStruct((b, seq, NSA_HEADS * HEAD_DIM), jnp.bfloat16),
                   jax.ShapeDtypeStruct((b, g, seq, ns), jnp.float32)],
        compiler_params=_cparams(("arbitrary", "arbitrary", "arbitrary")),
        name="nsa_cmp_select",
    )(proj3, kcvc, kcvc, cbias, overlap)


def _sel_attn_kernel(q_ref, k_ref, v_ref, sel_ref, exp_ref, bias_ref, o_ref, kmask_ref):
    tq = q_ref.shape[1]
    n = pl.program_id(2)
    scale = HEAD_DIM ** -0.5
    n_bias = bias_ref.shape[1]
    qs = _stack_heads(q_ref[0])
    kmask_ref[...] = jnp.dot(sel_ref[0, 0].astype(jnp.bfloat16), exp_ref[...],
                             preferred_element_type=jnp.float32)

    def body(kt, carry):
        m, l, acc = carry
        start = pl.multiple_of(kt * QT, QT)
        k = k_ref[0, pl.ds(start, QT), :]
        v = v_ref[0, pl.ds(start, QT), :]
        s = lax.dot_general(qs, k, (((1,), (1,)), ((), ())), preferred_element_type=jnp.float32) * scale
        o = jnp.minimum(n - kt, n_bias - 1)
        bias = jnp.concatenate([bias_ref[r, o] for r in range(NSA_REP)], axis=0)
        km = kmask_ref[:, pl.ds(start, QT)]
        km = jnp.concatenate([km] * NSA_REP, axis=0)
        s = jnp.where((km > 0.5) & (bias > 0.5 * NEG), s + bias, NEG)
        m_new = jnp.maximum(m, jnp.max(s, axis=-1, keepdims=True))
        alpha = jnp.exp(m - m_new)
        p = jnp.exp(s - m_new)
        l = alpha * l + jnp.sum(p, axis=-1, keepdims=True)
        acc = alpha * acc + jnp.dot(p.astype(jnp.bfloat16), v, preferred_element_type=jnp.float32)
        return m_new, l, acc

    rows = NSA_REP * tq
    init = (jnp.full((rows, 1), NEG, jnp.float32), jnp.zeros((rows, 1), jnp.float32),
            jnp.zeros((rows, HEAD_DIM), jnp.float32))
    m, l, acc = lax.fori_loop(0, n + 1, body, init)
    o = acc / l
    for r in range(NSA_REP):
        o_ref[0, :, r * HEAD_DIM:(r + 1) * HEAD_DIM] = o[r * tq:(r + 1) * tq].astype(o_ref.dtype)


def _sel_attention(proj3, sel, expand, bias_sel, lay, seq):
    b = proj3.shape[0]
    g = NSA_KV_GROUPS
    ns = seq // NSA_SEL_LEN
    qw = NSA_REP * HEAD_DIM
    nb = bias_sel.shape[1]
    cb_k = lay["kv"] + 2 * g
    cb_v = lay["kv"] + 3 * g
    return pl.pallas_call(
        _sel_attn_kernel,
        grid=(b, g, seq // QT),
        in_specs=[pl.BlockSpec((1, QT, qw), lambda i, j, n: (i, n, lay["q"] // NSA_REP + j)),
                  pl.BlockSpec((1, seq, HEAD_DIM), lambda i, j, n: (i, 0, cb_k + j)),
                  pl.BlockSpec((1, seq, HEAD_DIM), lambda i, j, n: (i, 0, cb_v + j)),
                  pl.BlockSpec((1, 1, QT, ns), lambda i, j, n: (i, j, n, 0)),
                  pl.BlockSpec((ns, seq), lambda i, j, n: (0, 0)),
                  pl.BlockSpec((NSA_REP, nb, QT, QT), lambda i, j, n: (j, 0, 0, 0))],
        out_specs=pl.BlockSpec((1, QT, qw), lambda i, j, n: (i, n, j)),
        out_shape=jax.ShapeDtypeStruct((b, seq, NSA_HEADS * HEAD_DIM), jnp.bfloat16),
        scratch_shapes=[pltpu.VMEM((QT, seq), jnp.float32)],
        compiler_params=_cparams(("arbitrary", "arbitrary", "arbitrary")),
        name="nsa_selected",
    )(proj3, proj3, proj3, sel, expand, bias_sel)


def _band_softmax(qs, k_ref, v_ref, bias, n, n_prev, scale):
    logits = []
    for j in range(n_prev + 1):
        kb = n - n_prev + j
        start = pl.multiple_of(jnp.maximum(kb, 0) * QT, QT)
        k = k_ref[0, pl.ds(start, QT), :]
        s = lax.dot_general(qs, k, (((1,), (1,)), ((), ())), preferred_element_type=jnp.float32) * scale
        bj = jnp.where(kb >= 0, bias[:, j * QT:(j + 1) * QT], NEG)
        logits.append(jnp.where(bj > 0.5 * NEG, s + bj, NEG))
    m = logits[0].max(axis=-1, keepdims=True)
    for s in logits[1:]:
        m = jnp.maximum(m, s.max(axis=-1, keepdims=True))
    l = jnp.zeros_like(m)
    acc = jnp.zeros((qs.shape[0], HEAD_DIM), jnp.float32)
    for j, s in enumerate(logits):
        kb = n - n_prev + j
        start = pl.multiple_of(jnp.maximum(kb, 0) * QT, QT)
        p = jnp.exp(s - m)
        l = l + jnp.sum(p, axis=-1, keepdims=True)
        acc = acc + jnp.dot(p.astype(jnp.bfloat16), v_ref[0, pl.ds(start, QT), :],
                            preferred_element_type=jnp.float32)
    return acc, m, l


def _win_attn_kernel(q_ref, k_ref, v_ref, bias_ref, o_ref, *, n_prev):
    tq = q_ref.shape[1]
    n = pl.program_id(2)
    qs = _stack_heads(q_ref[0])
    bias = jnp.concatenate([bias_ref[r, 0] for r in range(NSA_REP)], axis=0)
    acc, _, l = _band_softmax(qs, k_ref, v_ref, bias, n, n_prev, HEAD_DIM ** -0.5)
    o = acc / l
    for r in range(NSA_REP):
        o_ref[0, :, r * HEAD_DIM:(r + 1) * HEAD_DIM] = o[r * tq:(r + 1) * tq].astype(o_ref.dtype)


def _win_attention(proj3, bias_win, lay, seq):
    b = proj3.shape[0]
    g = NSA_KV_GROUPS
    qw = NSA_REP * HEAD_DIM
    n_prev = -(-(NSA_WINDOW - 1) // QT)
    cb_k = lay["kv"] + 4 * g
    cb_v = lay["kv"] + 5 * g
    return pl.pallas_call(
        functools.partial(_win_attn_kernel, n_prev=n_prev),
        grid=(b, g, seq // QT),
        in_specs=[pl.BlockSpec((1, QT, qw), lambda i, j, n: (i, n, lay["q"] // NSA_REP + j)),
                  pl.BlockSpec((1, seq, HEAD_DIM), lambda i, j, n: (i, 0, cb_k + j)),
                  pl.BlockSpec((1, seq, HEAD_DIM), lambda i, j, n: (i, 0, cb_v + j)),
                  pl.BlockSpec((NSA_REP, 1, QT, (n_prev + 1) * QT), lambda i, j, n: (j, 0, 0, 0))],
        out_specs=pl.BlockSpec((1, QT, qw), lambda i, j, n: (i, n, j)),
        out_shape=jax.ShapeDtypeStruct((b, seq, NSA_HEADS * HEAD_DIM), jnp.bfloat16),
        compiler_params=_cparams(("arbitrary", "arbitrary", "arbitrary")),
        name="nsa_window",
    )(proj3, proj3, proj3, bias_win)


def _dil_attn_kernel(q_ref, k_ref, v_ref, bias_ref, o_ref, lse_ref, *, n_prev):
    n = pl.program_id(3)
    acc, m, l = _band_softmax(q_ref[0], k_ref, v_ref, bias_ref[0, 0], n, n_prev, HEAD_DIM ** -0.5)
    o_ref[0] = (acc / l).astype(o_ref.dtype)
    lse_ref[0] = jnp.broadcast_to(m + jnp.log(l), lse_ref.shape[1:])


def _dil_attention(proj3, bias_dil, lay, seq, gi):
    b = proj3.shape[0]
    window, dil = DIL_CONFIGS[gi]
    p_heads = DIL_HEADS_PER_GROUP
    length = seq // dil
    n_prev = -(-(window // dil) // QT)
    nblk = lay["n"]
    view = proj3.reshape(b, length, dil * nblk * LANES)
    cb_q = lay["dil"] + gi * p_heads
    cb_k = cb_q + DIL_HEADS
    cb_v = cb_k + DIL_HEADS
    out_w = p_heads * HEAD_DIM
    o, lse = pl.pallas_call(
        functools.partial(_dil_attn_kernel, n_prev=n_prev),
        grid=(b, p_heads, dil, length // QT),
        in_specs=[pl.BlockSpec((1, QT, HEAD_DIM), lambda i, p, r, n: (i, n, r * nblk + cb_q + p)),
                  pl.BlockSpec((1, length, HEAD_DIM), lambda i, p, r, n: (i, 0, r * nblk + cb_k + p)),
                  pl.BlockSpec((1, length, HEAD_DIM), lambda i, p, r, n: (i, 0, r * nblk + cb_v + p)),
                  pl.BlockSpec((1, 1, QT, (n_prev + 1) * QT), lambda i, p, r, n: (p, 0, 0, 0))],
        out_specs=[pl.BlockSpec((1, QT, HEAD_DIM), lambda i, p, r, n: (i, n, r * p_heads + p)),
                   pl.BlockSpec((1, QT, HEAD_DIM), lambda i, p, r, n: (i, n, r * p_heads + p))],
        out_shape=[jax.ShapeDtypeStruct((b, length, dil * out_w), jnp.bfloat16),
                   jax.ShapeDtypeStruct((b, length, dil * out_w), jnp.float32)],
        compiler_params=_cparams(("arbitrary",) * 4),
        name=f"dilated_attn_{gi}",
    )(view, view, view, bias_dil)
    return o.reshape(b, seq, out_w), lse.reshape(b, seq, out_w)


def _post_kernel(ocmp_ref, osel_ref, owin_ref, gate_ref, od0_ref, od1_ref, od2_ref, l0_ref, l1_ref, l2_ref,
                 mg1_ref, mg2_ref, x_ref, g1_ref, sc2_ref, sh2_ref, n2_ref, wn_ref, wd_ref, wo_ref,
                 x1_ref, h2_ref):
    gates = _sigmoid(gate_ref[...].astype(jnp.float32))
    parts = []
    for h in range(NSA_HEADS):
        sl = slice(h * HEAD_DIM, (h + 1) * HEAD_DIM)
        y = (gates[:, 3 * h:3 * h + 1] * ocmp_ref[:, sl].astype(jnp.float32)
             + gates[:, 3 * h + 1:3 * h + 2] * osel_ref[:, sl].astype(jnp.float32)
             + gates[:, 3 * h + 2:3 * h + 3] * owin_ref[:, sl].astype(jnp.float32))
        parts.append(y.astype(jnp.bfloat16))
    y_nsa = jnp.concatenate(parts, axis=1)

    l0, l1, l2 = l0_ref[...], l1_ref[...], l2_ref[...]
    mx = jnp.maximum(jnp.maximum(l0, l1), l2)
    e0, e1, e2 = jnp.exp(l0 - mx), jnp.exp(l1 - mx), jnp.exp(l2 - mx)
    den = e0 + e1 + e2
    y_dil = ((e0 / den) * od0_ref[...].astype(jnp.float32) + (e1 / den) * od1_ref[...].astype(jnp.float32)
             + (e2 / den) * od2_ref[...].astype(jnp.float32)).astype(jnp.bfloat16)

    a = jnp.dot(y_nsa, wn_ref[...], preferred_element_type=jnp.float32)
    bb = jnp.dot(y_dil, wd_ref[...], preferred_element_type=jnp.float32)
    merged = (_sigmoid(mg1_ref[...].astype(jnp.float32)) * a
              + _sigmoid(mg2_ref[...].astype(jnp.float32)) * bb).astype(jnp.bfloat16)
    mix = jnp.dot(merged, wo_ref[...], preferred_element_type=jnp.float32)
    x1 = x_ref[...] + g1_ref[0] * mix
    x1_ref[...] = x1
    y = x1 * lax.rsqrt(jnp.mean(x1 * x1, axis=-1, keepdims=True) + EPS) * n2_ref[...]
    h2_ref[...] = y * (1.0 + sc2_ref[0]) + sh2_ref[0]


def _post(ocmp, osel, owin, proj2d, odil, lsedil, x2d, g1, sc2, sh2, norm2_g, wn, wd, wo, lay, seq):
    t, d = x2d.shape
    tm = 256
    per_batch = seq // tm
    nw = NSA_HEADS * HEAD_DIM
    dw = DIL_HEADS_PER_GROUP * HEAD_DIM
    row = lambda w: pl.BlockSpec((tm, w), lambda i: (i, 0))
    mod = pl.BlockSpec((1, 1, d), lambda i: (i // per_batch, 0, 0))
    full = lambda a: pl.BlockSpec(a.shape, lambda i: (0, 0))
    return pl.pallas_call(
        _post_kernel,
        grid=(t // tm,),
        in_specs=[row(nw), row(nw), row(nw),
                  pl.BlockSpec((tm, LANES), lambda i: (i, lay["gate"])),
                  row(dw), row(dw), row(dw), row(dw), row(dw), row(dw),
                  pl.BlockSpec((tm, d), lambda i: (i, 0)),
                  pl.BlockSpec((tm, d), lambda i: (i, 1)),
                  row(d), mod, mod, mod,
                  pl.BlockSpec((1, d), lambda i: (0, 0)),
                  full(wn), full(wd), full(wo)],
        out_specs=[row(d), row(d)],
        out_shape=[jax.ShapeDtypeStruct((t, d), jnp.float32), jax.ShapeDtypeStruct((t, d), jnp.float32)],
        compiler_params=_cparams(("arbitrary",)),
        name="merge_out_proj",
    )(ocmp, osel, owin, proj2d, odil[0], odil[1], odil[2], lsedil[0], lsedil[1], lsedil[2],
      proj2d, proj2d, x2d, g1, sc2, sh2, norm2_g.reshape(1, d), wn, wd, wo)


def _topk_rows(sc, k, payload=None):
    n = sc.shape[0]
    row = lax.broadcasted_iota(jnp.int32, sc.shape, 0).astype(jnp.float32)
    vals, picks = [], []
    for _ in range(k):
        mx = jnp.max(sc, axis=0, keepdims=True)
        first = jnp.min(jnp.where(sc == mx, row, float(n)), axis=0, keepdims=True)
        hit = row == first
        vals.append(mx)
        if payload is None:
            picks.append(first)
        else:
            picks.append(jnp.sum(jnp.where(hit, payload, 0.0), axis=0, keepdims=True))
        sc = jnp.where(hit, -jnp.inf, sc)
    return jnp.concatenate(vals, axis=0), jnp.concatenate(picks, axis=0)


def _route_kernel(h_ref, wq_ref, gain_ref, keys_ref, idx_ref, gate_ref):
    q = jnp.dot(h_ref[...].astype(jnp.bfloat16), wq_ref[...], preferred_element_type=jnp.float32)
    k = PEER_TOPK
    for head in range(PEER_HEADS):
        tops = []
        for part in range(2):
            hp = head * 2 + part
            a = q[:, hp * PEER_HALF:(hp + 1) * PEER_HALF]
            qn = (a * lax.rsqrt(jnp.mean(a * a, axis=-1, keepdims=True) + EPS)
                  * gain_ref[hp:hp + 1, :]).astype(jnp.bfloat16)
            sc = lax.dot_general(keys_ref[hp], qn, (((1,), (1,)), ((), ())),
                                 preferred_element_type=jnp.float32)
            tops.append(_topk_rows(sc, k))
        (s1, i1), (s2, i2) = tops
        cand = jnp.concatenate([s1[a:a + 1] + s2 for a in range(k)], axis=0)
        cand_idx = jnp.concatenate([i1[a:a + 1] * float(PEER_N_KEYS) + i2 for a in range(k)], axis=0)
        top_s, expert = _topk_rows(cand, k, payload=cand_idx)
        e = jnp.exp(top_s - top_s[0:1])
        gate = e / jnp.sum(e, axis=0, keepdims=True)
        idx_ref[head * k:(head + 1) * k, :] = expert.astype(jnp.int32)
        gate_ref[head * k:(head + 1) * k, :] = gate


def _route(h2, wq_bf16, q_gain, sub_keys_bf16):
    t, d = h2.shape
    tm = 256
    hk = PEER_HEADS * PEER_TOPK
    nq = wq_bf16.shape[1]
    return pl.pallas_call(
        _route_kernel,
        grid=(t // tm,),
        in_specs=[pl.BlockSpec((tm, d), lambda i: (i, 0)),
                  pl.BlockSpec((d, nq), lambda i: (0, 0)),
                  pl.BlockSpec((2 * PEER_HEADS, PEER_HALF), lambda i: (0, 0)),
                  pl.BlockSpec((2 * PEER_HEADS, PEER_N_KEYS, PEER_HALF), lambda i: (0, 0, 0))],
        out_specs=[pl.BlockSpec((hk, tm), lambda i: (0, i)),
                   pl.BlockSpec((hk, tm), lambda i: (0, i))],
        out_shape=[jax.ShapeDtypeStruct((hk, t), jnp.int32), jax.ShapeDtypeStruct((hk, t), jnp.float32)],
        compiler_params=_cparams(("arbitrary",)),
        name="peer_route",
    )(h2, wq_bf16, q_gain.reshape(2 * PEER_HEADS, PEER_HALF), sub_keys_bf16)


PEER_TB = 32
SUBLANES = 8


def _expert_kernel(idx_ref, nidx_ref, h_ref, gate_ref, x1_ref, g2_ref, tab_ref, o_ref, stage_a, stage_b, sem):
    tb, d = h_ref.shape
    k = PEER_TOPK
    hk = PEER_HEADS * k
    nr, nc = tb // SUBLANES, d // LANES
    stages = (stage_a, stage_b)
    step = pl.program_id(0)
    hi_mask = jnp.uint32(0xFFFF0000)

    def start_rows(ids, head, j, r):
        for s in range(SUBLANES):
            e = ids[0, 0, (r * SUBLANES + s) * hk + head * k + j]
            pltpu.make_async_copy(tab_ref.at[e], stages[head % 2].at[j, r, :, pl.ds(s, 1), :],
                                  sem.at[head % 2]).start()

    def start_head(ids, head):
        for j in range(k):
            for r in range(nr):
                start_rows(ids, head, j, r)

    def wait_head(head):
        buf = stages[head % 2]
        pltpu.make_async_copy(buf, buf, sem.at[head % 2]).wait()

    @pl.when(step == 0)
    def _():
        start_head(idx_ref, 0)

    lane = lax.broadcasted_iota(jnp.int32, (SUBLANES, LANES), 1)
    for head in range(PEER_HEADS):
        stage = stages[head % 2]
        wait_head(head)
        if head + 1 == PEER_HEADS:
            @pl.when(step + 1 < pl.num_programs(0))
            def _():
                start_head(nidx_ref, 0)
        acts = []
        for r in range(nr):
            rows = slice(r * SUBLANES, (r + 1) * SUBLANES)
            pre = jnp.zeros((SUBLANES, LANES), jnp.float32)
            for j in range(k):
                if head + 1 < PEER_HEADS:
                    start_rows(idx_ref, head + 1, j, r)
                acc = None
                for c in range(nc):
                    u = lax.bitcast_convert_type(stage[j, r, c] & hi_mask, jnp.float32)
                    prod = h_ref[rows, c * LANES:(c + 1) * LANES] * u
                    acc = prod if acc is None else acc + prod
                pre = jnp.where(lane == head * k + j, jnp.sum(acc, axis=-1, keepdims=True), pre)
            acts.append(gate_ref[rows, :] * _gelu(pre))
        for r in range(nr):
            rows = slice(r * SUBLANES, (r + 1) * SUBLANES)
            accs = [None] * nc
            for j in range(k):
                a = jnp.broadcast_to(acts[r][:, head * k + j:head * k + j + 1], (SUBLANES, LANES))
                for c in range(nc):
                    term = a * lax.bitcast_convert_type(stage[j, r, c] << 16, jnp.float32)
                    accs[c] = term if accs[c] is None else accs[c] + term
            for c in range(nc):
                cs = slice(c * LANES, (c + 1) * LANES)
                if head == 0:
                    o_ref[rows, cs] = accs[c]
                else:
                    o_ref[rows, cs] = o_ref[rows, cs] + accs[c]
    o_ref[...] = x1_ref[...] + g2_ref[0] * o_ref[...]


def _experts(idx_tok, gate_tok, h2, x1, g2, table, seq):
    t, d = h2.shape
    tb = PEER_TB
    hk = PEER_HEADS * PEER_TOPK
    per_batch = seq // tb
    nsteps = t // tb
    ids = idx_tok.reshape(nsteps, 1, tb * hk)
    stage = pltpu.VMEM((PEER_TOPK, tb // SUBLANES, d // LANES, SUBLANES, LANES), jnp.uint32)
    return pl.pallas_call(
        _expert_kernel,
        grid=(nsteps,),
        in_specs=[pl.BlockSpec((1, 1, tb * hk), lambda i: (i, 0, 0), memory_space=pltpu.SMEM),
                  pl.BlockSpec((1, 1, tb * hk), lambda i: (jnp.minimum(i + 1, nsteps - 1), 0, 0),
                               memory_space=pltpu.SMEM),
                  pl.BlockSpec((tb, d), lambda i: (i, 0)),
                  pl.BlockSpec((tb, hk), lambda i: (i, 0)),
                  pl.BlockSpec((tb, d), lambda i: (i, 0)),
                  pl.BlockSpec((1, 1, d), lambda i: (i // per_batch, 0, 0)),
                  pl.BlockSpec(memory_space=pl.ANY)],
        out_specs=pl.BlockSpec((tb, d), lambda i: (i, 0)),
        out_shape=jax.ShapeDtypeStruct((t, d), jnp.float32),
        scratch_shapes=[stage, stage, pltpu.SemaphoreType.DMA((2,))],
        compiler_params=_cparams(("arbitrary",)),
        name="peer_experts",
    )(ids, ids, h2, gate_tok, x1, g2, table)


def _pack_expert_table(u, v):
    e, d = u.shape
    ub = lax.bitcast_convert_type(u.astype(jnp.bfloat16), jnp.uint16).astype(jnp.uint32)
    vb = lax.bitcast_convert_type(v.astype(jnp.bfloat16), jnp.uint16).astype(jnp.uint32)
    return ((ub << 16) | vb).reshape(e, d // LANES, 1, LANES)


def _proj_weights(w_in, nsa_q_gain, nsa_k_gain, dil_q_gain, dil_k_gain, d_model, lay):
    dh = HEAD_DIM
    off_kv = NSA_HEADS * dh
    off_gate = off_kv + 6 * NSA_KV_GROUPS * dh
    off_dil = off_gate + 3 * NSA_HEADS
    off_merge = off_dil + 3 * DIL_HEADS * dh
    n_cols = lay["n"] * LANES
    pad = n_cols - (lay["gate"] * LANES + 3 * NSA_HEADS)
    w = jnp.concatenate([w_in[:, off_merge:], w_in[:, :off_kv], w_in[:, off_kv:off_gate],
                         w_in[:, off_dil:off_merge], w_in[:, off_gate:off_dil],
                         jnp.zeros((d_model, pad), w_in.dtype)], axis=1).astype(jnp.bfloat16)
    ones = jnp.ones((dh,), jnp.float32)
    gains, flags = [], []

    def add(gain, flag, count=1):
        for _ in range(count):
            gains.append(gain)
            flags.append(jnp.full((dh,), flag, jnp.float32))

    add(ones, 0.0, lay["q"])
    add(nsa_q_gain, 1.0, NSA_HEADS)
    for i in range(6):
        if i == 2:
            add(nsa_k_gain[1], 1.0, NSA_KV_GROUPS)
        elif i == 4:
            add(nsa_k_gain[2], 1.0, NSA_KV_GROUPS)
        else:
            add(ones, 0.0, NSA_KV_GROUPS)
    add(dil_q_gain, 1.0, DIL_HEADS)
    add(dil_k_gain, 1.0, DIL_HEADS)
    add(ones, 0.0, DIL_HEADS)
    add(ones, 0.0, lay["n"] - lay["gate"])
    return w, jnp.concatenate(flags).reshape(1, n_cols), jnp.concatenate(gains).reshape(1, n_cols)


def _token_mixer_and_norm2(x, mod, norm1_g, w_in, nsa_q_gain, nsa_k_gain, cmp_pos, cmp_w1, cmp_w2,
                           dil_q_gain, dil_k_gain, w_br_nsa, w_br_dil, w_out, norm2_g, rel_bias):
    b, seq, d = x.shape
    t = b * seq
    lay = _layout(d)
    sh1, sc1, g1, sh2, sc2, g2 = [m.reshape(b, 1, d) for m in jnp.split(mod, 6, axis=-1)]
    x2d = x.reshape(t, d)

    w, flags, gains = _proj_weights(w_in, nsa_q_gain, nsa_k_gain, dil_q_gain, dil_k_gain, d, lay)
    proj2d = _projection(x2d, norm1_g, sc1, sh1, w, flags, gains, seq)
    proj3 = proj2d.reshape(b, seq, lay["n"] * LANES)

    rel_flat = rel_bias.reshape(-1)
    nchunk = seq // NSA_CMP_STRIDE
    ns = seq // NSA_SEL_LEN
    bias_win = _bias_table(rel_flat, _band_buckets(-(-(NSA_WINDOW - 1) // QT), NSA_WINDOW - 1, 1), 0, NSA_HEADS)
    bias_sel = _bias_table(rel_flat, _causal_buckets(_n_causal_tiles(seq)), 0, NSA_HEADS)
    bias_cmp = _bias_table(rel_flat, _cmp_buckets(seq, nchunk), 0, NSA_HEADS)
    bias_dil = [_bias_table(rel_flat, _band_buckets(-(-(wnd // dil) // QT), wnd // dil, dil),
                            NSA_HEADS + gi * DIL_HEADS_PER_GROUP, DIL_HEADS_PER_GROUP)
                for gi, (wnd, dil) in enumerate(DIL_CONFIGS)]

    g = NSA_KV_GROUPS
    c0 = lay["kv"] * LANES
    tc = proj3[:, :, c0:c0 + 2 * g * HEAD_DIM].reshape(b, nchunk, NSA_CMP_STRIDE, 2 * g, HEAD_DIM)
    tc = tc.transpose(0, 3, 1, 2, 4).reshape(b, 2 * g, nchunk, NSA_CMP_STRIDE * HEAD_DIM)
    kcvc = _compress(tc, cmp_pos, cmp_w1, cmp_w2, nsa_k_gain[0])

    cstart = np.arange(nchunk) * NSA_CMP_STRIDE
    sstart = np.arange(ns) * NSA_SEL_LEN
    n_cmp = (seq - NSA_CMP_LEN) // NSA_CMP_STRIDE + 1
    overlap = ((cstart[:, None] < sstart[None, :] + NSA_SEL_LEN) & (cstart[:, None] + NSA_CMP_LEN > sstart[None, :])
               & (np.arange(nchunk)[:, None] < n_cmp)).astype(np.float32)
    expand = (np.arange(seq)[None, :] // NSA_SEL_LEN == np.arange(ns)[:, None]).astype(np.float32)

    ocmp, sel = _cmp_select(proj3, kcvc, bias_cmp, jnp.asarray(overlap), lay, seq)
    osel = _sel_attention(proj3, sel, jnp.asarray(expand, jnp.bfloat16), bias_sel, lay, seq)
    owin = _win_attention(proj3, bias_win, lay, seq)
    odil, lsedil = zip(*[_dil_attention(proj3, bias_dil[gi], lay, seq, gi) for gi in range(len(DIL_CONFIGS))])

    x1, h2 = _post(ocmp.reshape(t, -1), osel.reshape(t, -1), owin.reshape(t, -1), proj2d,
                   [o.reshape(t, -1) for o in odil], [l.reshape(t, -1) for l in lsedil],
                   x2d, g1, sc2, sh2, norm2_g, w_br_nsa.astype(jnp.bfloat16), w_br_dil.astype(jnp.bfloat16),
                   w_out.astype(jnp.bfloat16), lay, seq)
    return x1, h2, g2


def kernel(x, c, w_ada, b_ada, norm1_g, w_in, nsa_q_gain, nsa_k_gain, cmp_pos, cmp_w1, cmp_w2,
           dil_q_gain, dil_k_gain, w_br_nsa, w_br_dil, w_out, norm2_g, peer_w_q, peer_q_gain,
           peer_sub_keys, peer_u, peer_v, rel_bias):
    b, seq, d = x.shape
    depth = w_ada.shape[0]
    for layer in range(depth):
        mod = _modulation(c, w_ada[layer], b_ada[layer])
        x1, h2, g2 = _token_mixer_and_norm2(
            x, mod, norm1_g[layer], w_in[layer], nsa_q_gain[layer], nsa_k_gain[layer], cmp_pos[layer],
            cmp_w1[layer], cmp_w2[layer], dil_q_gain[layer], dil_k_gain[layer], w_br_nsa[layer],
            w_br_dil[layer], w_out[layer], norm2_g[layer], rel_bias)
        idx, gate = _route(h2, peer_w_q[layer].astype(jnp.bfloat16), peer_q_gain[layer],
                           peer_sub_keys[layer].reshape(2 * PEER_HEADS, PEER_N_KEYS, PEER_HALF).astype(jnp.bfloat16))
        table = _pack_expert_table(peer_u[layer], peer_v[layer])
        out = _experts(idx.T, gate.T, h2, x1, g2, table, seq)
        x = out.reshape(b, seq, d)
    return x
```

```python
import functools
import math

import numpy as np
import jax
import jax.numpy as jnp
from jax import lax
from jax.experimental import pallas as pl
from jax.experimental.pallas import tpu as pltpu

HEAD_DIM = 128
LANES = 128
NSA_HEADS = 8
NSA_KV_GROUPS = 2
NSA_REP = NSA_HEADS // NSA_KV_GROUPS
NSA_CMP_LEN = 32
NSA_CMP_STRIDE = 16
NSA_SEL_LEN = 64
NSA_TOP_N = 16
NSA_WINDOW = 512
DIL_CONFIGS = ((128, 1), (512, 4), (2048, 16))
DIL_HEADS_PER_GROUP = 4
DIL_HEADS = DIL_HEADS_PER_GROUP * len(DIL_CONFIGS)
REL_BUCKETS = 32
REL_MAX_DIST = 2048
PEER_HEADS = 8
PEER_N_KEYS = 128
PEER_TOPK = 16
PEER_HALF = 128
EPS = 1e-6
NEG = -1e30
SEL_FORCE = 1e4
MASKED_BUCKET = REL_BUCKETS

QT = 128
VMEM_LIMIT = 48 * 1024 * 1024

CB_MERGE = 0
def _layout(d_model):
    nmerge = 2 * d_model // LANES
    cb_q = nmerge
    cb_kv = cb_q + NSA_HEADS
    cb_gate = cb_kv + 6 * NSA_KV_GROUPS
    n_used = cb_gate + 1
    n_blocks = -(-n_used // 4) * 4
    return dict(q=cb_q, kv=cb_kv, gate=cb_gate, n=n_blocks)


def _cparams(sem):
    return pltpu.CompilerParams(dimension_semantics=sem, vmem_limit_bytes=VMEM_LIMIT)


def _t5_bucket_np(dist):
    n = np.maximum(dist, 0)
    max_exact = REL_BUCKETS // 2
    nf = np.maximum(n, max_exact).astype(np.float32)
    large = max_exact + (np.log(nf / np.float32(max_exact)) / np.float32(math.log(REL_MAX_DIST / max_exact))
                         * np.float32(REL_BUCKETS - max_exact)).astype(np.int32)
    large = np.minimum(large, REL_BUCKETS - 1)
    return np.where(n < max_exact, n, large).astype(np.int32)


def _gelu(x):
    return jax.nn.gelu(x)


def _sigmoid(x):
    return 1.0 / (1.0 + jnp.exp(-x))


def _mod_kernel(c_ref, w_ref, b_ref, o_ref):
    c = c_ref[...]
    cond = c * _sigmoid(c)
    o_ref[...] = jnp.dot(cond.astype(jnp.bfloat16), w_ref[...].astype(jnp.bfloat16),
                         preferred_element_type=jnp.float32) + b_ref[...]


def _modulation(c, w_ada, b_ada):
    b, d = c.shape
    n = w_ada.shape[1]
    rows = 8
    cp = jnp.zeros((rows, d), jnp.float32).at[:b].set(c)
    tn = 512
    out = pl.pallas_call(
        _mod_kernel,
        grid=(n // tn,),
        in_specs=[pl.BlockSpec((rows, d), lambda j: (0, 0)),
                  pl.BlockSpec((d, tn), lambda j: (0, j)),
                  pl.BlockSpec((1, tn), lambda j: (0, j))],
        out_specs=pl.BlockSpec((rows, tn), lambda j: (0, j)),
        out_shape=jax.ShapeDtypeStruct((rows, n), jnp.float32),
        compiler_params=_cparams(("arbitrary",)),
        name="adaln_mod",
    )(cp, w_ada, b_ada.reshape(1, n))
    return out[:b]


def _proj_kernel(x_ref, g_ref, sc_ref, sh_ref, w_ref, flag_ref, gain_ref, o_ref, h_ref):
    @pl.when(pl.program_id(1) == 0)
    def _():
        x = x_ref[...]
        y = x * lax.rsqrt(jnp.mean(x * x, axis=-1, keepdims=True) + EPS) * g_ref[...]
        h_ref[...] = (y * (1.0 + sc_ref[0]) + sh_ref[0]).astype(jnp.bfloat16)

    acc = jnp.dot(h_ref[...], w_ref[...], preferred_element_type=jnp.float32)
    tn = acc.shape[1]
    for k in range(tn // LANES):
        a = acc[:, k * LANES:(k + 1) * LANES]
        f = flag_ref[:, k * LANES:(k + 1) * LANES]
        r = lax.rsqrt(jnp.mean(a * a, axis=-1, keepdims=True) + EPS)
        scale = f * r + (1.0 - f)
        o_ref[:, k * LANES:(k + 1) * LANES] = (a * scale * gain_ref[:, k * LANES:(k + 1) * LANES]).astype(o_ref.dtype)


def _projection(x2d, norm_g, sc, sh, w_bf16, flags, gains, seq, out_dtype):
    t, d = x2d.shape
    n = w_bf16.shape[1]
    tm = min(1024, seq)
    tn = 512
    per_batch = seq // tm
    return pl.pallas_call(
        _proj_kernel,
        grid=(t // tm, n // tn),
        in_specs=[pl.BlockSpec((tm, d), lambda i, j: (i, 0)),
                  pl.BlockSpec((1, d), lambda i, j: (0, 0)),
                  pl.BlockSpec((1, 1, d), lambda i, j: (i // per_batch, 0, 0)),
                  pl.BlockSpec((1, 1, d), lambda i, j: (i // per_batch, 0, 0)),
                  pl.BlockSpec((d, tn), lambda i, j: (0, j)),
                  pl.BlockSpec((1, tn), lambda i, j: (0, j)),
                  pl.BlockSpec((1, tn), lambda i, j: (0, j))],
        out_specs=pl.BlockSpec((tm, tn), lambda i, j: (i, j)),
        out_shape=jax.ShapeDtypeStruct((t, n), out_dtype),
        scratch_shapes=[pltpu.VMEM((tm, d), jnp.bfloat16)],
        compiler_params=_cparams(("arbitrary", "arbitrary")),
        name="norm_in_proj",
    )(x2d, norm_g.reshape(1, d), sc, sh, w_bf16, flags, gains)


def _bias_kernel(tbl_ref, idx_ref, o_ref, *, head0):
    h = pl.program_id(0) + head0
    b = idx_ref[0]
    out = jnp.full(b.shape, NEG, jnp.float32)
    for k in range(REL_BUCKETS):
        out = jnp.where(b == k, tbl_ref[h * REL_BUCKETS + k], out)
    o_ref[0, 0] = out


def _bias_table(rel_bias_flat, bucket_idx, head0, n_heads):
    nt, r, c = bucket_idx.shape
    return pl.pallas_call(
        functools.partial(_bias_kernel, head0=head0),
        grid=(n_heads, nt),
        in_specs=[pl.BlockSpec(memory_space=pltpu.SMEM),
                  pl.BlockSpec((1, r, c), lambda h, t: (t, 0, 0))],
        out_specs=pl.BlockSpec((1, 1, r, c), lambda h, t: (h, t, 0, 0)),
        out_shape=jax.ShapeDtypeStruct((n_heads, nt, r, c), jnp.float32),
        compiler_params=_cparams(("arbitrary", "arbitrary")),
        name="rel_bias_table",
    )(rel_bias_flat, jnp.asarray(bucket_idx))


def _band_buckets(n_prev, span, dist_scale):
    i = np.arange(QT)[:, None]
    c = np.arange((n_prev + 1) * QT)[None, :]
    dist = n_prev * QT + i - c
    ok = (dist >= 0) & (dist <= span)
    return np.where(ok, _t5_bucket_np(dist * dist_scale), MASKED_BUCKET).astype(np.int32)[None]


def _causal_buckets(n_tiles):
    i = np.arange(QT)[None, :]
    j = np.arange(QT)[:, None]
    out = []
    for o in range(n_tiles):
        dist = QT * o + i - j
        out.append(np.where(dist >= 0, _t5_bucket_np(dist), MASKED_BUCKET))
    return np.stack(out).astype(np.int32)


def _n_causal_tiles(seq):
    first_last = int(np.argmax(_t5_bucket_np(np.arange(4 * REL_MAX_DIST)) == REL_BUCKETS - 1))
    o_const = -(-(first_last + QT - 1) // QT)
    return min(seq // QT, o_const + 1)


def _cmp_buckets(seq, n_cmp_pad):
    nq = seq // QT
    n_cmp = (seq - NSA_CMP_LEN) // NSA_CMP_STRIDE + 1
    i = np.arange(QT)[None, :, None]
    n = np.arange(nq)[:, None, None]
    c = np.arange(n_cmp_pad)[None, None, :]
    dist = n * QT + i - (c * NSA_CMP_STRIDE + NSA_CMP_LEN - 1)
    ok = (dist >= 0) & (c < n_cmp)
    return np.where(ok, _t5_bucket_np(dist), MASKED_BUCKET).astype(np.int32)


def _compress_kernel(t_ref, pos_ref, w1_ref, w2_ref, gain_ref, o_ref):
    half = t_ref.shape[3]
    a = t_ref[0, 0].astype(jnp.float32)
    lo = (a + pos_ref[0, :, :half]).astype(jnp.bfloat16)
    hi = (a + pos_ref[0, :, half:]).astype(jnp.bfloat16)
    p = jnp.dot(lo, w1_ref[0, :half, :], preferred_element_type=jnp.float32)
    q = jnp.dot(hi, w1_ref[0, half:, :], preferred_element_type=jnp.float32)
    n = q.shape[0]
    q_next = jnp.concatenate([q[1:], q[:1]], axis=0)
    hid = _gelu(p + q_next)
    out = jnp.dot(hid.astype(jnp.bfloat16), w2_ref[0], preferred_element_type=jnp.float32)
    is_key = pl.program_id(1) < NSA_KV_GROUPS
    r = lax.rsqrt(jnp.mean(out * out, axis=-1, keepdims=True) + EPS)
    normed = out * r * gain_ref[...]
    o_ref[0, 0] = jnp.where(is_key, normed, out).astype(o_ref.dtype)


def _compress(t_chunks, cmp_pos, cmp_w1, cmp_w2, k_gain0):
    b, nh, nchunk, width = t_chunks.shape
    dh = HEAD_DIM
    g = NSA_KV_GROUPS
    return pl.pallas_call(
        _compress_kernel,
        grid=(b, nh),
        in_specs=[pl.BlockSpec((1, 1, nchunk, width), lambda i, h: (i, h, 0, 0)),
                  pl.BlockSpec((1, 1, 2 * width), lambda i, h: (h // g, 0, 0)),
                  pl.BlockSpec((1, 2 * width, dh), lambda i, h: (h // g, 0, 0)),
                  pl.BlockSpec((1, dh, dh), lambda i, h: (h // g, 0, 0)),
                  pl.BlockSpec((1, dh), lambda i, h: (0, 0))],
        out_specs=pl.BlockSpec((1, 1, nchunk, dh), lambda i, h: (i, h, 0, 0)),
        out_shape=jax.ShapeDtypeStruct((b, nh, nchunk, dh), jnp.bfloat16),
        compiler_params=_cparams(("arbitrary", "arbitrary")),
        name="nsa_compress",
    )(t_chunks, cmp_pos.reshape(2, 1, 2 * width), cmp_w1.astype(jnp.bfloat16),
      cmp_w2.astype(jnp.bfloat16), k_gain0.reshape(1, dh))


def _stack_heads(q):
    return jnp.concatenate([q[:, r * HEAD_DIM:(r + 1) * HEAD_DIM] for r in range(NSA_REP)], axis=0)


def _cmp_select_kernel(q_ref, kc_ref, vc_ref, bias_ref, ov_ref, o_ref, sel_ref, *, n_sel):
    tq = q_ref.shape[1]
    n = pl.program_id(2)
    scale = HEAD_DIM ** -0.5
    qs = _stack_heads(q_ref[0])
    s = lax.dot_general(qs, kc_ref[0, 0], (((1,), (1,)), ((), ())),
                        preferred_element_type=jnp.float32) * scale
    bias = jnp.concatenate([bias_ref[r, 0] for r in range(NSA_REP)], axis=0)
    valid = bias > 0.5 * NEG
    s = jnp.where(valid, s + bias, NEG)
    m = jnp.max(s, axis=-1, keepdims=True)
    e = jnp.exp(s - m)
    p = e / jnp.sum(e, axis=-1, keepdims=True) * valid.astype(jnp.float32)
    o = jnp.dot(p.astype(jnp.bfloat16), vc_ref[0, 0], preferred_element_type=jnp.float32)
    for r in range(NSA_REP):
        o_ref[0, :, r * HEAD_DIM:(r + 1) * HEAD_DIM] = o[r * tq:(r + 1) * tq].astype(o_ref.dtype)

    psum = p[0:tq]
    for r in range(1, NSA_REP):
        psum = psum + p[r * tq:(r + 1) * tq]
    imp = jnp.dot(psum, ov_ref[...], preferred_element_type=jnp.float32,
                  precision=lax.Precision.HIGHEST)
    n_blocks = sel_ref.shape[3]
    jb = lax.broadcasted_iota(jnp.int32, (tq, n_blocks), 1).astype(jnp.float32)
    row = lax.broadcasted_iota(jnp.int32, (tq, n_blocks), 0)
    cur = ((n * tq + row) // NSA_SEL_LEN).astype(jnp.float32)
    forced = (jb == 0.0) | (jb == cur) | (jb == cur - 1.0)
    score = jnp.where(forced, SEL_FORCE, jnp.where(jb > cur, -SEL_FORCE, imp))
    chosen = jnp.zeros((tq, n_blocks), jnp.float32)
    for _ in range(n_sel):
        mx = jnp.max(score, axis=-1, keepdims=True)
        first = jnp.min(jnp.where(score == mx, jb, float(n_blocks)), axis=-1, keepdims=True)
        hit = jb == first
        chosen = jnp.where(hit, 1.0, chosen)
        score = jnp.where(hit, -jnp.inf, score)
    sel_ref[0, 0] = chosen


def _cmp_select(proj3, kcvc, cbias, overlap, lay, seq):
    b = proj3.shape[0]
    g = NSA_KV_GROUPS
    ncp = kcvc.shape[2]
    ns = seq // NSA_SEL_LEN
    n_sel = min(NSA_TOP_N, ns)
    qw = NSA_REP * HEAD_DIM
    return pl.pallas_call(
        functools.partial(_cmp_select_kernel, n_sel=n_sel),
        grid=(b, g, seq // QT),
        in_specs=[pl.BlockSpec((1, QT, qw), lambda i, j, n: (i, n, lay["q"] // NSA_REP + j)),
                  pl.BlockSpec((1, 1, ncp, HEAD_DIM), lambda i, j, n: (i, j, 0, 0)),
                  pl.BlockSpec((1, 1, ncp, HEAD_DIM), lambda i, j, n: (i, g + j, 0, 0)),
                  pl.BlockSpec((NSA_REP, 1, QT, ncp), lambda i, j, n: (j, n, 0, 0)),
                  pl.BlockSpec((ncp, ns), lambda i, j, n: (0, 0))],
        out_specs=[pl.BlockSpec((1, QT, qw), lambda i, j, n: (i, n, j)),
                   pl.BlockSpec((1, 1, QT, ns), lambda i, j, n: (i, j, n, 0))],
        out_shape=[jax.ShapeDtypeStruct((b, seq, NSA_HEADS * HEAD_DIM), jnp.bfloat16),
                   jax.ShapeDtypeStruct((b, g, seq, ns), jnp.float32)],
        compiler_params=_cparams(("arbitrary", "arbitrary", "arbitrary")),
        name="nsa_cmp_select",
    )(proj3, kcvc, kcvc, cbias, overlap)


def _sel_attn_kernel(q_ref, k_ref, vt_ref, sel_ref, exp_ref, bias_ref, o_ref, madd_ref):
    tq = q_ref.shape[1]
    n = pl.program_id(2)
    scale = HEAD_DIM ** -0.5
    n_bias = bias_ref.shape[1]
    q = q_ref[0].astype(jnp.float32)
    qt = jnp.concatenate([q[:, r * HEAD_DIM:(r + 1) * HEAD_DIM].T for r in range(NSA_REP)],
                         axis=1).astype(jnp.bfloat16)
    km = lax.dot_general(exp_ref[...], sel_ref[0, 0].astype(jnp.bfloat16), (((1,), (1,)), ((), ())),
                         preferred_element_type=jnp.float32)
    madd_ref[...] = (km - 1.0) * (-NEG)

    def logits(kt):
        start = pl.multiple_of(kt * QT, QT)
        s = jnp.dot(k_ref[0, pl.ds(start, QT), :], qt, preferred_element_type=jnp.float32) * scale
        o = n - kt
        ob = jnp.clip(o, 0, n_bias - 1)
        bias = jnp.concatenate([bias_ref[r, ob] for r in range(NSA_REP)], axis=1)
        ma = jnp.where(o >= 0, madd_ref[pl.ds(start, QT), :], NEG)
        return s + bias + jnp.concatenate([ma] * NSA_REP, axis=1), start

    def body(kp, carry):
        m, l, acc = carry
        (s0, st0), (s1, st1) = logits(2 * kp), logits(2 * kp + 1)
        m_new = jnp.maximum(m, jnp.maximum(jnp.max(s0, axis=0, keepdims=True),
                                           jnp.max(s1, axis=0, keepdims=True)))
        alpha = jnp.exp(m - m_new)
        p0 = jnp.exp(s0 - m_new)
        p1 = jnp.exp(s1 - m_new)
        l = alpha * l + jnp.sum(p0, axis=0, keepdims=True) + jnp.sum(p1, axis=0, keepdims=True)
        acc = (alpha * acc
               + jnp.dot(vt_ref[0, 0, :, pl.ds(st0, QT)], p0.astype(jnp.bfloat16), preferred_element_type=jnp.float32)
               + jnp.dot(vt_ref[0, 0, :, pl.ds(st1, QT)], p1.astype(jnp.bfloat16), preferred_element_type=jnp.float32))
        return m_new, l, acc

    cols = NSA_REP * tq
    init = (jnp.full((1, cols), NEG, jnp.float32), jnp.zeros((1, cols), jnp.float32),
            jnp.zeros((HEAD_DIM, cols), jnp.float32))
    m, l, acc = lax.fori_loop(0, n // 2 + 1, body, init)
    out = acc / l
    for r in range(NSA_REP):
        o_ref[0, :, r * HEAD_DIM:(r + 1) * HEAD_DIM] = out[:, r * tq:(r + 1) * tq].T.astype(o_ref.dtype)


def _sel_attention(proj3, sel, expand_t, bias_sel_t, lay, seq):
    b = proj3.shape[0]
    g = NSA_KV_GROUPS
    ns = seq // NSA_SEL_LEN
    qw = NSA_REP * HEAD_DIM
    nb = bias_sel_t.shape[1]
    cb_k = lay["kv"] + 2 * g
    cb_v = lay["kv"] + 3 * g
    vt = proj3[:, :, cb_v * LANES:(cb_v + g) * LANES].reshape(b, seq, g, HEAD_DIM).transpose(0, 2, 3, 1)
    return pl.pallas_call(
        _sel_attn_kernel,
        grid=(b, g, seq // QT),
        in_specs=[pl.BlockSpec((1, QT, qw), lambda i, j, n: (i, n, lay["q"] // NSA_REP + j)),
                  pl.BlockSpec((1, seq, HEAD_DIM), lambda i, j, n: (i, 0, cb_k + j)),
                  pl.BlockSpec((1, 1, HEAD_DIM, seq), lambda i, j, n: (i, j, 0, 0)),
                  pl.BlockSpec((1, 1, QT, ns), lambda i, j, n: (i, j, n, 0)),
                  pl.BlockSpec((seq, ns), lambda i, j, n: (0, 0)),
                  pl.BlockSpec((NSA_REP, nb, QT, QT), lambda i, j, n: (j, 0, 0, 0))],
        out_specs=pl.BlockSpec((1, QT, qw), lambda i, j, n: (i, n, j)),
        out_shape=jax.ShapeDtypeStruct((b, seq, NSA_HEADS * HEAD_DIM), jnp.bfloat16),
        scratch_shapes=[pltpu.VMEM((seq, QT), jnp.float32)],
        compiler_params=_cparams(("arbitrary", "arbitrary", "arbitrary")),
        name="nsa_selected",
    )(proj3, proj3, vt, sel, expand_t, bias_sel_t)


def _band_softmax(qs, k_tile, v_tile, bias, n, n_prev, scale):
    logits = []
    for j in range(n_prev + 1):
        kb = n - n_prev + j
        k = k_tile(jnp.maximum(kb, 0))
        s = lax.dot_general(qs, k, (((1,), (1,)), ((), ())), preferred_element_type=jnp.float32) * scale
        bj = jnp.where(kb >= 0, bias[:, j * QT:(j + 1) * QT], NEG)
        logits.append(jnp.where(bj > 0.5 * NEG, s + bj, NEG))
    m = logits[0].max(axis=-1, keepdims=True)
    for s in logits[1:]:
        m = jnp.maximum(m, s.max(axis=-1, keepdims=True))
    l = jnp.zeros_like(m)
    acc = jnp.zeros((qs.shape[0], HEAD_DIM), jnp.float32)
    for j, s in enumerate(logits):
        p = jnp.exp(s - m)
        l = l + jnp.sum(p, axis=-1, keepdims=True)
        acc = acc + jnp.dot(p.astype(jnp.bfloat16), v_tile(jnp.maximum(n - n_prev + j, 0)),
                            preferred_element_type=jnp.float32)
    return acc, m, l


def _win_attn_kernel(q_ref, k_ref, v_ref, bias_ref, o_ref, *, n_prev):
    tq = q_ref.shape[1]
    n = pl.program_id(2)
    qs = _stack_heads(q_ref[0])
    bias = jnp.concatenate([bias_ref[r, 0] for r in range(NSA_REP)], axis=0)
    tile = lambda ref: (lambda kb: ref[0, pl.ds(pl.multiple_of(kb * QT, QT), QT), :])
    acc, _, l = _band_softmax(qs, tile(k_ref), tile(v_ref), bias, n, n_prev, HEAD_DIM ** -0.5)
    o = acc / l
    for r in range(NSA_REP):
        o_ref[0, :, r * HEAD_DIM:(r + 1) * HEAD_DIM] = o[r * tq:(r + 1) * tq].astype(o_ref.dtype)


def _win_attention(proj3, bias_win, lay, seq):
    b = proj3.shape[0]
    g = NSA_KV_GROUPS
    qw = NSA_REP * HEAD_DIM
    n_prev = -(-(NSA_WINDOW - 1) // QT)
    cb_k = lay["kv"] + 4 * g
    cb_v = lay["kv"] + 5 * g
    return pl.pallas_call(
        functools.partial(_win_attn_kernel, n_prev=n_prev),
        grid=(b, g, seq // QT),
        in_specs=[pl.BlockSpec((1, QT, qw), lambda i, j, n: (i, n, lay["q"] // NSA_REP + j)),
                  pl.BlockSpec((1, seq, HEAD_DIM), lambda i, j, n: (i, 0, cb_k + j)),
                  pl.BlockSpec((1, seq, HEAD_DIM), lambda i, j, n: (i, 0, cb_v + j)),
                  pl.BlockSpec((NSA_REP, 1, QT, (n_prev + 1) * QT), lambda i, j, n: (j, 0, 0, 0))],
        out_specs=pl.BlockSpec((1, QT, qw), lambda i, j, n: (i, n, j)),
        out_shape=jax.ShapeDtypeStruct((b, seq, NSA_HEADS * HEAD_DIM), jnp.bfloat16),
        compiler_params=_cparams(("arbitrary", "arbitrary", "arbitrary")),
        name="nsa_window",
    )(proj3, proj3, proj3, bias_win)


def _dil_attn_kernel(q_ref, k_ref, v_ref, bias_ref, o_ref, lse_ref, *, n_prev, dil):
    tiles = q_ref.shape[1] // (QT * dil)
    t = pl.program_id(2)
    r = t // tiles
    n = t % tiles
    rows = lambda kb: pl.ds(kb * (QT * dil) + r, QT, stride=dil) if dil > 1 else pl.ds(
        pl.multiple_of(kb * QT, QT), QT)
    tile = lambda ref: (lambda kb: ref[0, rows(kb), :].astype(jnp.bfloat16))
    acc, m, l = _band_softmax(tile(q_ref)(n), tile(k_ref), tile(v_ref), bias_ref[0, 0], n, n_prev,
                              HEAD_DIM ** -0.5)
    o_ref[0, rows(n), :] = acc / l
    lse_ref[0, rows(n), :] = jnp.broadcast_to(m + jnp.log(l), (QT, HEAD_DIM))


def _dil_attention(projd3, bias_dil, seq, gi):
    b = projd3.shape[0]
    window, dil = DIL_CONFIGS[gi]
    p_heads = DIL_HEADS_PER_GROUP
    n_prev = -(-(window // dil) // QT)
    cb_q = gi * p_heads
    cb_k = cb_q + DIL_HEADS
    cb_v = cb_k + DIL_HEADS
    out_w = p_heads * HEAD_DIM
    whole = lambda cb: pl.BlockSpec((1, seq, HEAD_DIM), lambda i, p, t: (i, 0, cb + p))
    return pl.pallas_call(
        functools.partial(_dil_attn_kernel, n_prev=n_prev, dil=dil),
        grid=(b, p_heads, seq // QT),
        in_specs=[whole(cb_q), whole(cb_k), whole(cb_v),
                  pl.BlockSpec((1, 1, QT, (n_prev + 1) * QT), lambda i, p, t: (p, 0, 0, 0))],
        out_specs=[whole(0), whole(0)],
        out_shape=[jax.ShapeDtypeStruct((b, seq, out_w), jnp.float32),
                   jax.ShapeDtypeStruct((b, seq, out_w), jnp.float32)],
        compiler_params=_cparams(("arbitrary",) * 3),
        name=f"dilated_attn_{gi}",
    )(projd3, projd3, projd3, bias_dil)


def _post_kernel(ocmp_ref, osel_ref, owin_ref, gate_ref, od0_ref, od1_ref, od2_ref, l0_ref, l1_ref, l2_ref,
                 mg1_ref, mg2_ref, x_ref, g1_ref, sc2_ref, sh2_ref, n2_ref, wn_ref, wd_ref, wo_ref,
                 x1_ref, h2_ref):
    gates = _sigmoid(gate_ref[...].astype(jnp.float32))
    parts = []
    for h in range(NSA_HEADS):
        sl = slice(h * HEAD_DIM, (h + 1) * HEAD_DIM)
        y = (gates[:, 3 * h:3 * h + 1] * ocmp_ref[:, sl].astype(jnp.float32)
             + gates[:, 3 * h + 1:3 * h + 2] * osel_ref[:, sl].astype(jnp.float32)
             + gates[:, 3 * h + 2:3 * h + 3] * owin_ref[:, sl].astype(jnp.float32))
        parts.append(y.astype(jnp.bfloat16))
    y_nsa = jnp.concatenate(parts, axis=1)

    l0, l1, l2 = l0_ref[...], l1_ref[...], l2_ref[...]
    mx = jnp.maximum(jnp.maximum(l0, l1), l2)
    e0, e1, e2 = jnp.exp(l0 - mx), jnp.exp(l1 - mx), jnp.exp(l2 - mx)
    den = e0 + e1 + e2
    y_dil = ((e0 / den) * od0_ref[...].astype(jnp.float32) + (e1 / den) * od1_ref[...].astype(jnp.float32)
             + (e2 / den) * od2_ref[...].astype(jnp.float32)).astype(jnp.bfloat16)

    a = jnp.dot(y_nsa, wn_ref[...], preferred_element_type=jnp.float32)
    bb = jnp.dot(y_dil, wd_ref[...], preferred_element_type=jnp.float32)
    merged = (_sigmoid(mg1_ref[...].astype(jnp.float32)) * a
              + _sigmoid(mg2_ref[...].astype(jnp.float32)) * bb).astype(jnp.bfloat16)
    mix = jnp.dot(merged, wo_ref[...], preferred_element_type=jnp.float32)
    x1 = x_ref[...] + g1_ref[0] * mix
    x1_ref[...] = x1
    y = x1 * lax.rsqrt(jnp.mean(x1 * x1, axis=-1, keepdims=True) + EPS) * n2_ref[...]
    h2_ref[...] = y * (1.0 + sc2_ref[0]) + sh2_ref[0]


def _post(ocmp, osel, owin, proj2d, odil, lsedil, x2d, g1, sc2, sh2, norm2_g, wn, wd, wo, lay, seq):
    t, d = x2d.shape
    tm = 256
    per_batch = seq // tm
    nw = NSA_HEADS * HEAD_DIM
    dw = DIL_HEADS_PER_GROUP * HEAD_DIM
    row = lambda w: pl.BlockSpec((tm, w), lambda i: (i, 0))
    mod = pl.BlockSpec((1, 1, d), lambda i: (i // per_batch, 0, 0))
    full = lambda a: pl.BlockSpec(a.shape, lambda i: (0, 0))
    return pl.pallas_call(
        _post_kernel,
        grid=(t // tm,),
        in_specs=[row(nw), row(nw), row(nw),
                  pl.BlockSpec((tm, LANES), lambda i: (i, lay["gate"])),
                  row(dw), row(dw), row(dw), row(dw), row(dw), row(dw),
                  pl.BlockSpec((tm, d), lambda i: (i, 0)),
                  pl.BlockSpec((tm, d), lambda i: (i, 1)),
                  row(d), mod, mod, mod,
                  pl.BlockSpec((1, d), lambda i: (0, 0)),
                  full(wn), full(wd), full(wo)],
        out_specs=[row(d), row(d)],
        out_shape=[jax.ShapeDtypeStruct((t, d), jnp.float32), jax.ShapeDtypeStruct((t, d), jnp.float32)],
        compiler_params=_cparams(("arbitrary",)),
        name="merge_out_proj",
    )(ocmp, osel, owin, proj2d, odil[0], odil[1], odil[2], lsedil[0], lsedil[1], lsedil[2],
      proj2d, proj2d, x2d, g1, sc2, sh2, norm2_g.reshape(1, d), wn, wd, wo)


def _topk_rows(sc, k, payload=None):
    n = sc.shape[0]
    row = lax.broadcasted_iota(jnp.int32, sc.shape, 0).astype(jnp.float32)
    vals, picks = [], []
    for _ in range(k):
        mx = jnp.max(sc, axis=0, keepdims=True)
        first = jnp.min(jnp.where(sc == mx, row, float(n)), axis=0, keepdims=True)
        hit = row == first
        vals.append(mx)
        if payload is None:
            picks.append(first)
        else:
            picks.append(jnp.sum(jnp.where(hit, payload, 0.0), axis=0, keepdims=True))
        sc = jnp.where(hit, -jnp.inf, sc)
    return jnp.concatenate(vals, axis=0), jnp.concatenate(picks, axis=0)


def _route_kernel(h_ref, wq_ref, gain_ref, keys_ref, idx_ref, gate_ref):
    q = jnp.dot(h_ref[...].astype(jnp.bfloat16), wq_ref[...], preferred_element_type=jnp.float32)
    k = PEER_TOPK
    for head in range(PEER_HEADS):
        tops = []
        for part in range(2):
            hp = head * 2 + part
            a = q[:, hp * PEER_HALF:(hp + 1) * PEER_HALF]
            qn = (a * lax.rsqrt(jnp.mean(a * a, axis=-1, keepdims=True) + EPS)
                  * gain_ref[hp:hp + 1, :]).astype(jnp.bfloat16)
            sc = lax.dot_general(keys_ref[hp], qn, (((1,), (1,)), ((), ())),
                                 preferred_element_type=jnp.float32)
            tops.append(_topk_rows(sc, k))
        (s1, i1), (s2, i2) = tops
        half = k // 2
        sub = lax.broadcasted_iota(jnp.int32, (half, s1.shape[1]), 0)
        vals = [s1[0:1] + s2]
        ids = [i1[0:1] * float(PEER_N_KEYS) + i2]
        for a in range(1, half):
            vals.append(jnp.where(sub < k // (a + 1), s1[a:a + 1] + s2[:half], -jnp.inf))
            ids.append(i1[a:a + 1] * float(PEER_N_KEYS) + i2[:half])
        vals.append(s1[half:] + s2[0:1])
        ids.append(i1[half:] * float(PEER_N_KEYS) + i2[0:1])
        top_s, expert = _topk_rows(jnp.concatenate(vals, axis=0), k, payload=jnp.concatenate(ids, axis=0))
        e = jnp.exp(top_s - top_s[0:1])
        gate = e / jnp.sum(e, axis=0, keepdims=True)
        idx_ref[head * k:(head + 1) * k, :] = expert.astype(jnp.int32)
        gate_ref[head * k:(head + 1) * k, :] = gate


def _route(h2, wq_bf16, q_gain, sub_keys_bf16):
    t, d = h2.shape
    tm = LANES
    hk = PEER_HEADS * PEER_TOPK
    nq = wq_bf16.shape[1]
    return pl.pallas_call(
        _route_kernel,
        grid=(t // tm,),
        in_specs=[pl.BlockSpec((tm, d), lambda i: (i, 0)),
                  pl.BlockSpec((d, nq), lambda i: (0, 0)),
                  pl.BlockSpec((2 * PEER_HEADS, PEER_HALF), lambda i: (0, 0)),
                  pl.BlockSpec((2 * PEER_HEADS, PEER_N_KEYS, PEER_HALF), lambda i: (0, 0, 0))],
        out_specs=[pl.BlockSpec((hk, tm), lambda i: (0, i)),
                   pl.BlockSpec((hk, tm), lambda i: (0, i))],
        out_shape=[jax.ShapeDtypeStruct((hk, t), jnp.int32), jax.ShapeDtypeStruct((hk, t), jnp.float32)],
        compiler_params=_cparams(("arbitrary",)),
        name="peer_route",
    )(h2, wq_bf16, q_gain.reshape(2 * PEER_HEADS, PEER_HALF), sub_keys_bf16)


PEER_TB = 32
SUBLANES = 8


def _expert_kernel(idx_ref, nidx_ref, h_ref, gate_ref, x1_ref, g2_ref, tab_ref, o_ref, stage_a, stage_b, sem):
    tb, d = h_ref.shape
    k = PEER_TOPK
    hk = PEER_HEADS * k
    nr, nc = tb // SUBLANES, d // LANES
    stages = (stage_a, stage_b)
    step = pl.program_id(0)
    hi_mask = jnp.uint32(0xFFFF0000)

    def start_rows(ids, head, j, r):
        for s in range(SUBLANES):
            e = ids[0, 0, (r * SUBLANES + s) * hk + head * k + j]
            pltpu.make_async_copy(tab_ref.at[e], stages[head % 2].at[j, r, :, pl.ds(s, 1), :],
                                  sem.at[head % 2]).start(priority=s % 2)

    def start_head(ids, head):
        for j in range(k):
            for r in range(nr):
                start_rows(ids, head, j, r)

    def wait_head(head):
        buf = stages[head % 2]
        pltpu.make_async_copy(buf, buf, sem.at[head % 2]).wait()

    @pl.when(step == 0)
    def _():
        start_head(idx_ref, 0)

    lane = lax.broadcasted_iota(jnp.int32, (SUBLANES, LANES), 1)
    for head in range(PEER_HEADS):
        stage = stages[head % 2]
        wait_head(head)
        if head + 1 == PEER_HEADS:
            @pl.when(step + 1 < pl.num_programs(0))
            def _():
                start_head(nidx_ref, 0)
        acts = []
        for r in range(nr):
            rows = slice(r * SUBLANES, (r + 1) * SUBLANES)
            pre = jnp.zeros((SUBLANES, LANES), jnp.float32)
            for j in range(k):
                if head + 1 < PEER_HEADS:
                    start_rows(idx_ref, head + 1, j, r)
                acc = None
                for c in range(nc):
                    u = lax.bitcast_convert_type(stage[j, r, c] & hi_mask, jnp.float32)
                    prod = h_ref[rows, c * LANES:(c + 1) * LANES] * u
                    acc = prod if acc is None else acc + prod
                pre = jnp.where(lane == head * k + j, jnp.sum(acc, axis=-1, keepdims=True), pre)
            acts.append(gate_ref[rows, :] * _gelu(pre))
        for r in range(nr):
            rows = slice(r * SUBLANES, (r + 1) * SUBLANES)
            accs = [None] * nc
            for j in range(k):
                a = jnp.broadcast_to(acts[r][:, head * k + j:head * k + j + 1], (SUBLANES, LANES))
                for c in range(nc):
                    term = a * lax.bitcast_convert_type(stage[j, r, c] << 16, jnp.float32)
                    accs[c] = term if accs[c] is None else accs[c] + term
            for c in range(nc):
                cs = slice(c * LANES, (c + 1) * LANES)
                if head == 0:
                    o_ref[rows, cs] = accs[c]
                else:
                    o_ref[rows, cs] = o_ref[rows, cs] + accs[c]
    o_ref[...] = x1_ref[...] + g2_ref[0] * o_ref[...]


def _experts(idx_tok, gate_tok, h2, x1, g2, table, seq):
    t, d = h2.shape
    tb = PEER_TB
    hk = PEER_HEADS * PEER_TOPK
    per_batch = seq // tb
    nsteps = t // tb
    ids = idx_tok.reshape(nsteps, 1, tb * hk)
    stage = pltpu.VMEM((PEER_TOPK, tb // SUBLANES, d // LANES, SUBLANES, LANES), jnp.uint32)
    return pl.pallas_call(
        _expert_kernel,
        grid=(nsteps,),
        in_specs=[pl.BlockSpec((1, 1, tb * hk), lambda i: (i, 0, 0), memory_space=pltpu.SMEM),
                  pl.BlockSpec((1, 1, tb * hk), lambda i: (jnp.minimum(i + 1, nsteps - 1), 0, 0),
                               memory_space=pltpu.SMEM),
                  pl.BlockSpec((tb, d), lambda i: (i, 0)),
                  pl.BlockSpec((tb, hk), lambda i: (i, 0)),
                  pl.BlockSpec((tb, d), lambda i: (i, 0)),
                  pl.BlockSpec((1, 1, d), lambda i: (i // per_batch, 0, 0)),
                  pl.BlockSpec(memory_space=pl.ANY)],
        out_specs=pl.BlockSpec((tb, d), lambda i: (i, 0)),
        out_shape=jax.ShapeDtypeStruct((t, d), jnp.float32),
        scratch_shapes=[stage, stage, pltpu.SemaphoreType.DMA((2,))],
        compiler_params=_cparams(("arbitrary",)),
        name="peer_experts",
    )(ids, ids, h2, gate_tok, x1, g2, table)


def _pack_expert_table(u, v):
    e, d = u.shape
    ub = lax.bitcast_convert_type(u.astype(jnp.bfloat16), jnp.uint16).astype(jnp.uint32)
    vb = lax.bitcast_convert_type(v.astype(jnp.bfloat16), jnp.uint16).astype(jnp.uint32)
    return ((ub << 16) | vb).reshape(e, d // LANES, 1, LANES)


def _proj_weights(w_in, nsa_q_gain, nsa_k_gain, dil_q_gain, dil_k_gain, d_model, lay):
    dh = HEAD_DIM
    off_kv = NSA_HEADS * dh
    off_gate = off_kv + 6 * NSA_KV_GROUPS * dh
    off_dil = off_gate + 3 * NSA_HEADS
    off_merge = off_dil + 3 * DIL_HEADS * dh
    n_cols = lay["n"] * LANES
    pad = n_cols - (lay["gate"] * LANES + 3 * NSA_HEADS)
    w = jnp.concatenate([w_in[:, off_merge:], w_in[:, :off_kv], w_in[:, off_kv:off_gate],
                         w_in[:, off_gate:off_dil], jnp.zeros((d_model, pad), w_in.dtype)],
                        axis=1).astype(jnp.bfloat16)
    w_dil = w_in[:, off_dil:off_merge].astype(jnp.bfloat16)
    ones = jnp.ones((dh,), jnp.float32)

    def rows(spec):
        gains = jnp.concatenate([g for g, _, n in spec for _ in range(n)])
        flags = jnp.concatenate([jnp.full((dh,), f, jnp.float32) for _, f, n in spec for _ in range(n)])
        return flags.reshape(1, -1), gains.reshape(1, -1)

    g = NSA_KV_GROUPS
    main = [(ones, 0.0, lay["q"]), (nsa_q_gain, 1.0, NSA_HEADS), (ones, 0.0, 2 * g),
            (nsa_k_gain[1], 1.0, g), (ones, 0.0, g), (nsa_k_gain[2], 1.0, g), (ones, 0.0, g),
            (ones, 0.0, lay["n"] - lay["gate"])]
    dil = [(dil_q_gain, 1.0, DIL_HEADS), (dil_k_gain, 1.0, DIL_HEADS), (ones, 0.0, DIL_HEADS)]
    return (w,) + rows(main), (w_dil,) + rows(dil)


def _token_mixer_and_norm2(x, mod, norm1_g, w_in, nsa_q_gain, nsa_k_gain, cmp_pos, cmp_w1, cmp_w2,
                           dil_q_gain, dil_k_gain, w_br_nsa, w_br_dil, w_out, norm2_g, rel_bias):
    b, seq, d = x.shape
    t = b * seq
    lay = _layout(d)
    sh1, sc1, g1, sh2, sc2, g2 = [m.reshape(b, 1, d) for m in jnp.split(mod, 6, axis=-1)]
    x2d = x.reshape(t, d)

    main_w, dil_w = _proj_weights(w_in, nsa_q_gain, nsa_k_gain, dil_q_gain, dil_k_gain, d, lay)
    proj2d = _projection(x2d, norm1_g, sc1, sh1, *main_w, seq, jnp.bfloat16)
    proj3 = proj2d.reshape(b, seq, lay["n"] * LANES)
    projd3 = _projection(x2d, norm1_g, sc1, sh1, *dil_w, seq, jnp.float32).reshape(b, seq, -1)

    rel_flat = rel_bias.reshape(-1)
    nchunk = seq // NSA_CMP_STRIDE
    ns = seq // NSA_SEL_LEN
    bias_win = _bias_table(rel_flat, _band_buckets(-(-(NSA_WINDOW - 1) // QT), NSA_WINDOW - 1, 1), 0, NSA_HEADS)
    bias_sel = _bias_table(rel_flat, _causal_buckets(_n_causal_tiles(seq)), 0, NSA_HEADS)
    bias_cmp = _bias_table(rel_flat, _cmp_buckets(seq, nchunk), 0, NSA_HEADS)
    bias_dil = [_bias_table(rel_flat, _band_buckets(-(-(wnd // dil) // QT), wnd // dil, dil),
                            NSA_HEADS + gi * DIL_HEADS_PER_GROUP, DIL_HEADS_PER_GROUP)
                for gi, (wnd, dil) in enumerate(DIL_CONFIGS)]

    g = NSA_KV_GROUPS
    c0 = lay["kv"] * LANES
    tc = proj3[:, :, c0:c0 + 2 * g * HEAD_DIM].reshape(b, nchunk, NSA_CMP_STRIDE, 2 * g, HEAD_DIM)
    tc = tc.transpose(0, 3, 1, 2, 4).reshape(b, 2 * g, nchunk, NSA_CMP_STRIDE * HEAD_DIM)
    kcvc = _compress(tc, cmp_pos, cmp_w1, cmp_w2, nsa_k_gain[0])

    cstart = np.arange(nchunk) * NSA_CMP_STRIDE
    sstart = np.arange(ns) * NSA_SEL_LEN
    n_cmp = (seq - NSA_CMP_LEN) // NSA_CMP_STRIDE + 1
    overlap = ((cstart[:, None] < sstart[None, :] + NSA_SEL_LEN) & (cstart[:, None] + NSA_CMP_LEN > sstart[None, :])
               & (np.arange(nchunk)[:, None] < n_cmp)).astype(np.float32)
    expand_t = (np.arange(seq)[:, None] // NSA_SEL_LEN == np.arange(ns)[None, :]).astype(np.float32)

    ocmp, sel = _cmp_select(proj3, kcvc, bias_cmp, jnp.asarray(overlap), lay, seq)
    osel = _sel_attention(proj3, sel, jnp.asarray(expand_t, jnp.bfloat16), bias_sel, lay, seq)
    owin = _win_attention(proj3, bias_win, lay, seq)
    odil, lsedil = zip(*[_dil_attention(projd3, bias_dil[gi], seq, gi) for gi in range(len(DIL_CONFIGS))])

    x1, h2 = _post(ocmp.reshape(t, -1), osel.reshape(t, -1), owin.reshape(t, -1), proj2d,
                   [o.reshape(t, -1) for o in odil], [l.reshape(t, -1) for l in lsedil],
                   x2d, g1, sc2, sh2, norm2_g, w_br_nsa.astype(jnp.bfloat16), w_br_dil.astype(jnp.bfloat16),
                   w_out.astype(jnp.bfloat16), lay, seq)
    return x1, h2, g2


def kernel(x, c, w_ada, b_ada, norm1_g, w_in, nsa_q_gain, nsa_k_gain, cmp_pos, cmp_w1, cmp_w2,
           dil_q_gain, dil_k_gain, w_br_nsa, w_br_dil, w_out, norm2_g, peer_w_q, peer_q_gain,
           peer_sub_keys, peer_u, peer_v, rel_bias):
    b, seq, d = x.shape
    depth = w_ada.shape[0]
    for layer in range(depth):
        mod = _modulation(c, w_ada[layer], b_ada[layer])
        x1, h2, g2 = _token_mixer_and_norm2(
            x, mod, norm1_g[layer], w_in[layer], nsa_q_gain[layer], nsa_k_gain[layer], cmp_pos[layer],
            cmp_w1[layer], cmp_w2[layer], dil_q_gain[layer], dil_k_gain[layer], w_br_nsa[layer],
            w_br_dil[layer], w_out[layer], norm2_g[layer], rel_bias)
        idx, gate = _route(h2, peer_w_q[layer].astype(jnp.bfloat16), peer_q_gain[layer],
                           peer_sub_keys[layer].reshape(2 * PEER_HEADS, PEER_N_KEYS, PEER_HALF).astype(jnp.bfloat16))
        table = _pack_expert_table(peer_u[layer], peer_v[layer])
        out = _experts(idx.T, gate.T, h2, x1, g2, table, seq)
        x = out.reshape(b, seq, d)
    return x
```

```python
import functools
import math

import numpy as np
import jax
import jax.numpy as jnp
from jax import lax
from jax.experimental import pallas as pl
from jax.experimental.pallas import tpu as pltpu

HEAD_DIM = 128
LANES = 128
NSA_HEADS = 8
NSA_KV_GROUPS = 2
NSA_REP = NSA_HEADS // NSA_KV_GROUPS
NSA_CMP_LEN = 32
NSA_CMP_STRIDE = 16
NSA_SEL_LEN = 64
NSA_TOP_N = 16
NSA_WINDOW = 512
DIL_CONFIGS = ((128, 1), (512, 4), (2048, 16))
DIL_HEADS_PER_GROUP = 4
DIL_HEADS = DIL_HEADS_PER_GROUP * len(DIL_CONFIGS)
REL_BUCKETS = 32
REL_MAX_DIST = 2048
PEER_HEADS = 8
PEER_N_KEYS = 128
PEER_TOPK = 16
PEER_HALF = 128
EPS = 1e-6
NEG = -1e30
SEL_FORCE = 1e4
MASKED_BUCKET = REL_BUCKETS

QT = 128
VMEM_LIMIT = 48 * 1024 * 1024

CB_MERGE = 0
def _layout(d_model):
    nmerge = 2 * d_model // LANES
    cb_q = nmerge
    cb_kv = cb_q + NSA_HEADS
    cb_gate = cb_kv + 6 * NSA_KV_GROUPS
    n_used = cb_gate + 1
    n_blocks = -(-n_used // 4) * 4
    return dict(q=cb_q, kv=cb_kv, gate=cb_gate, n=n_blocks)


def _cparams(sem):
    return pltpu.CompilerParams(dimension_semantics=sem, vmem_limit_bytes=VMEM_LIMIT)


def _t5_bucket_np(dist):
    n = np.maximum(dist, 0)
    max_exact = REL_BUCKETS // 2
    nf = np.maximum(n, max_exact).astype(np.float32)
    large = max_exact + (np.log(nf / np.float32(max_exact)) / np.float32(math.log(REL_MAX_DIST / max_exact))
                         * np.float32(REL_BUCKETS - max_exact)).astype(np.int32)
    large = np.minimum(large, REL_BUCKETS - 1)
    return np.where(n < max_exact, n, large).astype(np.int32)


def _gelu(x):
    return jax.nn.gelu(x)


def _sigmoid(x):
    return 1.0 / (1.0 + jnp.exp(-x))


def _mod_kernel(c_ref, w_ref, b_ref, o_ref):
    c = c_ref[...]
    cond = c * _sigmoid(c)
    o_ref[...] = jnp.dot(cond.astype(jnp.bfloat16), w_ref[...].astype(jnp.bfloat16),
                         preferred_element_type=jnp.float32) + b_ref[...]


def _modulation(c, w_ada, b_ada):
    b, d = c.shape
    n = w_ada.shape[1]
    rows = 8
    cp = jnp.zeros((rows, d), jnp.float32).at[:b].set(c)
    tn = 512
    out = pl.pallas_call(
        _mod_kernel,
        grid=(n // tn,),
        in_specs=[pl.BlockSpec((rows, d), lambda j: (0, 0)),
                  pl.BlockSpec((d, tn), lambda j: (0, j)),
                  pl.BlockSpec((1, tn), lambda j: (0, j))],
        out_specs=pl.BlockSpec((rows, tn), lambda j: (0, j)),
        out_shape=jax.ShapeDtypeStruct((rows, n), jnp.float32),
        compiler_params=_cparams(("arbitrary",)),
        name="adaln_mod",
    )(cp, w_ada, b_ada.reshape(1, n))
    return out[:b]


def _proj_kernel(x_ref, g_ref, sc_ref, sh_ref, w_ref, flag_ref, gain_ref, o_ref, h_ref):
    @pl.when(pl.program_id(1) == 0)
    def _():
        x = x_ref[...]
        y = x * lax.rsqrt(jnp.mean(x * x, axis=-1, keepdims=True) + EPS) * g_ref[...]
        h_ref[...] = (y * (1.0 + sc_ref[0]) + sh_ref[0]).astype(jnp.bfloat16)

    acc = jnp.dot(h_ref[...], w_ref[...], preferred_element_type=jnp.float32)
    tn = acc.shape[1]
    for k in range(tn // LANES):
        a = acc[:, k * LANES:(k + 1) * LANES]
        f = flag_ref[:, k * LANES:(k + 1) * LANES]
        r = lax.rsqrt(jnp.mean(a * a, axis=-1, keepdims=True) + EPS)
        scale = f * r + (1.0 - f)
        o_ref[:, k * LANES:(k + 1) * LANES] = (a * scale * gain_ref[:, k * LANES:(k + 1) * LANES]).astype(o_ref.dtype)


def _projection(x2d, norm_g, sc, sh, w_bf16, flags, gains, seq, out_dtype):
    t, d = x2d.shape
    n = w_bf16.shape[1]
    tm = min(1024, seq)
    tn = 512
    per_batch = seq // tm
    return pl.pallas_call(
        _proj_kernel,
        grid=(t // tm, n // tn),
        in_specs=[pl.BlockSpec((tm, d), lambda i, j: (i, 0)),
                  pl.BlockSpec((1, d), lambda i, j: (0, 0)),
                  pl.BlockSpec((1, 1, d), lambda i, j: (i // per_batch, 0, 0)),
                  pl.BlockSpec((1, 1, d), lambda i, j: (i // per_batch, 0, 0)),
                  pl.BlockSpec((d, tn), lambda i, j: (0, j)),
                  pl.BlockSpec((1, tn), lambda i, j: (0, j)),
                  pl.BlockSpec((1, tn), lambda i, j: (0, j))],
        out_specs=pl.BlockSpec((tm, tn), lambda i, j: (i, j)),
        out_shape=jax.ShapeDtypeStruct((t, n), out_dtype),
        scratch_shapes=[pltpu.VMEM((tm, d), jnp.bfloat16)],
        compiler_params=_cparams(("arbitrary", "arbitrary")),
        name="norm_in_proj",
    )(x2d, norm_g.reshape(1, d), sc, sh, w_bf16, flags, gains)


def _bias_kernel(tbl_ref, idx_ref, o_ref, *, head0):
    h = pl.program_id(0) + head0
    b = idx_ref[0]
    out = jnp.full(b.shape, NEG, jnp.float32)
    for k in range(REL_BUCKETS):
        out = jnp.where(b == k, tbl_ref[h * REL_BUCKETS + k], out)
    o_ref[0, 0] = out


def _bias_table(rel_bias_flat, bucket_idx, head0, n_heads):
    nt, r, c = bucket_idx.shape
    return pl.pallas_call(
        functools.partial(_bias_kernel, head0=head0),
        grid=(n_heads, nt),
        in_specs=[pl.BlockSpec(memory_space=pltpu.SMEM),
                  pl.BlockSpec((1, r, c), lambda h, t: (t, 0, 0))],
        out_specs=pl.BlockSpec((1, 1, r, c), lambda h, t: (h, t, 0, 0)),
        out_shape=jax.ShapeDtypeStruct((n_heads, nt, r, c), jnp.float32),
        compiler_params=_cparams(("arbitrary", "arbitrary")),
        name="rel_bias_table",
    )(rel_bias_flat, jnp.asarray(bucket_idx))


def _band_buckets(n_prev, span, dist_scale):
    i = np.arange(QT)[:, None]
    c = np.arange((n_prev + 1) * QT)[None, :]
    dist = n_prev * QT + i - c
    ok = (dist >= 0) & (dist <= span)
    return np.where(ok, _t5_bucket_np(dist * dist_scale), MASKED_BUCKET).astype(np.int32)[None]


def _causal_buckets(n_tiles):
    i = np.arange(QT)[None, :]
    j = np.arange(QT)[:, None]
    out = []
    for o in range(n_tiles):
        dist = QT * o + i - j
        out.append(np.where(dist >= 0, _t5_bucket_np(dist), MASKED_BUCKET))
    return np.stack(out).astype(np.int32)


def _n_causal_tiles(seq):
    first_last = int(np.argmax(_t5_bucket_np(np.arange(4 * REL_MAX_DIST)) == REL_BUCKETS - 1))
    o_const = -(-(first_last + QT - 1) // QT)
    return min(seq // QT, o_const + 1)


def _cmp_buckets(seq, n_cmp_pad):
    nq = seq // QT
    n_cmp = (seq - NSA_CMP_LEN) // NSA_CMP_STRIDE + 1
    i = np.arange(QT)[None, :, None]
    n = np.arange(nq)[:, None, None]
    c = np.arange(n_cmp_pad)[None, None, :]
    dist = n * QT + i - (c * NSA_CMP_STRIDE + NSA_CMP_LEN - 1)
    ok = (dist >= 0) & (c < n_cmp)
    return np.where(ok, _t5_bucket_np(dist), MASKED_BUCKET).astype(np.int32)


def _compress_kernel(t_ref, pos_ref, w1_ref, w2_ref, gain_ref, o_ref):
    half = t_ref.shape[3]
    a = t_ref[0, 0].astype(jnp.float32)
    lo = (a + pos_ref[0, :, :half]).astype(jnp.bfloat16)
    hi = (a + pos_ref[0, :, half:]).astype(jnp.bfloat16)
    p = jnp.dot(lo, w1_ref[0, :half, :], preferred_element_type=jnp.float32)
    q = jnp.dot(hi, w1_ref[0, half:, :], preferred_element_type=jnp.float32)
    n = q.shape[0]
    q_next = jnp.concatenate([q[1:], q[:1]], axis=0)
    hid = _gelu(p + q_next)
    out = jnp.dot(hid.astype(jnp.bfloat16), w2_ref[0], preferred_element_type=jnp.float32)
    is_key = pl.program_id(1) < NSA_KV_GROUPS
    r = lax.rsqrt(jnp.mean(out * out, axis=-1, keepdims=True) + EPS)
    normed = out * r * gain_ref[...]
    o_ref[0, 0] = jnp.where(is_key, normed, out).astype(o_ref.dtype)


def _compress(t_chunks, cmp_pos, cmp_w1, cmp_w2, k_gain0):
    b, nh, nchunk, width = t_chunks.shape
    dh = HEAD_DIM
    g = NSA_KV_GROUPS
    return pl.pallas_call(
        _compress_kernel,
        grid=(b, nh),
        in_specs=[pl.BlockSpec((1, 1, nchunk, width), lambda i, h: (i, h, 0, 0)),
                  pl.BlockSpec((1, 1, 2 * width), lambda i, h: (h // g, 0, 0)),
                  pl.BlockSpec((1, 2 * width, dh), lambda i, h: (h // g, 0, 0)),
                  pl.BlockSpec((1, dh, dh), lambda i, h: (h // g, 0, 0)),
                  pl.BlockSpec((1, dh), lambda i, h: (0, 0))],
        out_specs=pl.BlockSpec((1, 1, nchunk, dh), lambda i, h: (i, h, 0, 0)),
        out_shape=jax.ShapeDtypeStruct((b, nh, nchunk, dh), jnp.bfloat16),
        compiler_params=_cparams(("arbitrary", "arbitrary")),
        name="nsa_compress",
    )(t_chunks, cmp_pos.reshape(2, 1, 2 * width), cmp_w1.astype(jnp.bfloat16),
      cmp_w2.astype(jnp.bfloat16), k_gain0.reshape(1, dh))


def _stack_heads(q):
    return jnp.concatenate([q[:, r * HEAD_DIM:(r + 1) * HEAD_DIM] for r in range(NSA_REP)], axis=0)


def _cmp_select_kernel(q_ref, kc_ref, vc_ref, bias_ref, ov_ref, o_ref, sel_ref, *, n_sel):
    tq = q_ref.shape[1]
    n = pl.program_id(2)
    scale = HEAD_DIM ** -0.5
    qs = _stack_heads(q_ref[0])
    s = lax.dot_general(qs, kc_ref[0, 0], (((1,), (1,)), ((), ())),
                        preferred_element_type=jnp.float32) * scale
    bias = jnp.concatenate([bias_ref[r, 0] for r in range(NSA_REP)], axis=0)
    valid = bias > 0.5 * NEG
    s = jnp.where(valid, s + bias, NEG)
    m = jnp.max(s, axis=-1, keepdims=True)
    e = jnp.exp(s - m)
    p = e / jnp.sum(e, axis=-1, keepdims=True) * valid.astype(jnp.float32)
    o = jnp.dot(p.astype(jnp.bfloat16), vc_ref[0, 0], preferred_element_type=jnp.float32)
    for r in range(NSA_REP):
        o_ref[0, :, r * HEAD_DIM:(r + 1) * HEAD_DIM] = o[r * tq:(r + 1) * tq].astype(o_ref.dtype)

    psum = p[0:tq]
    for r in range(1, NSA_REP):
        psum = psum + p[r * tq:(r + 1) * tq]
    imp = lax.dot_general(ov_ref[...], psum, (((1,), (1,)), ((), ())), preferred_element_type=jnp.float32,
                          precision=lax.Precision.HIGHEST)
    n_blocks = sel_ref.shape[2]
    jb = lax.broadcasted_iota(jnp.int32, (n_blocks, tq), 0).astype(jnp.float32)
    qpos = lax.broadcasted_iota(jnp.int32, (n_blocks, tq), 1)
    cur = ((n * tq + qpos) // NSA_SEL_LEN).astype(jnp.float32)
    forced = (jb == 0.0) | (jb == cur) | (jb == cur - 1.0)
    score = jnp.where(forced, SEL_FORCE, jnp.where(jb > cur, -SEL_FORCE, imp))
    chosen = jnp.zeros((n_blocks, tq), jnp.float32)
    for _ in range(n_sel):
        mx = jnp.max(score, axis=0, keepdims=True)
        first = jnp.min(jnp.where(score == mx, jb, float(n_blocks)), axis=0, keepdims=True)
        hit = jb == first
        chosen = jnp.where(hit, 1.0, chosen)
        score = jnp.where(hit, -jnp.inf, score)
    sel_ref[0, 0] = chosen


def _cmp_select(proj3, kcvc, cbias, overlap, lay, seq):
    b = proj3.shape[0]
    g = NSA_KV_GROUPS
    ncp = kcvc.shape[2]
    ns = seq // NSA_SEL_LEN
    n_sel = min(NSA_TOP_N, ns)
    qw = NSA_REP * HEAD_DIM
    return pl.pallas_call(
        functools.partial(_cmp_select_kernel, n_sel=n_sel),
        grid=(b, g, seq // QT),
        in_specs=[pl.BlockSpec((1, QT, qw), lambda i, j, n: (i, n, lay["q"] // NSA_REP + j)),
                  pl.BlockSpec((1, 1, ncp, HEAD_DIM), lambda i, j, n: (i, j, 0, 0)),
                  pl.BlockSpec((1, 1, ncp, HEAD_DIM), lambda i, j, n: (i, g + j, 0, 0)),
                  pl.BlockSpec((NSA_REP, 1, QT, ncp), lambda i, j, n: (j, n, 0, 0)),
                  pl.BlockSpec((ns, ncp), lambda i, j, n: (0, 0))],
        out_specs=[pl.BlockSpec((1, QT, qw), lambda i, j, n: (i, n, j)),
                   pl.BlockSpec((1, 1, ns, QT), lambda i, j, n: (i, j, 0, n))],
        out_shape=[jax.ShapeDtypeStruct((b, seq, NSA_HEADS * HEAD_DIM), jnp.bfloat16),
                   jax.ShapeDtypeStruct((b, g, ns, seq), jnp.float32)],
        compiler_params=_cparams(("arbitrary", "arbitrary", "arbitrary")),
        name="nsa_cmp_select",
    )(proj3, kcvc, kcvc, cbias, overlap)


def _sel_attn_kernel(q_ref, k_ref, vt_ref, sel_ref, exp_ref, bias_ref, o_ref, madd_ref):
    tq = q_ref.shape[1]
    n = pl.program_id(2)
    scale = HEAD_DIM ** -0.5
    n_bias = bias_ref.shape[1]
    q = q_ref[0].astype(jnp.float32)
    qt = jnp.concatenate([q[:, r * HEAD_DIM:(r + 1) * HEAD_DIM].T for r in range(NSA_REP)],
                         axis=1).astype(jnp.bfloat16)
    km = jnp.dot(exp_ref[...], sel_ref[0, 0].astype(jnp.bfloat16), preferred_element_type=jnp.float32)
    madd_ref[...] = (km - 1.0) * (-NEG)

    last_tile = k_ref.shape[1] // QT - 1

    def tile_start(kt):
        return pl.multiple_of(jnp.minimum(kt, last_tile) * QT, QT)

    def logits(kt):
        start = tile_start(kt)
        s = jnp.dot(k_ref[0, pl.ds(start, QT), :], qt, preferred_element_type=jnp.float32) * scale
        o = n - kt
        ob = jnp.clip(o, 0, n_bias - 1)
        bias = jnp.concatenate([bias_ref[r, ob] for r in range(NSA_REP)], axis=1)
        ma = jnp.where(o >= 0, madd_ref[pl.ds(start, QT), :], NEG)
        return s + bias + jnp.concatenate([ma] * NSA_REP, axis=1)

    def body(kp, carry):
        m, l, acc, s0, s1 = carry
        n0, n1 = logits(2 * kp + 2), logits(2 * kp + 3)
        m_new = jnp.maximum(m, jnp.maximum(jnp.max(s0, axis=0, keepdims=True),
                                           jnp.max(s1, axis=0, keepdims=True)))
        alpha = jnp.exp(m - m_new)
        p0 = jnp.exp(s0 - m_new)
        p1 = jnp.exp(s1 - m_new)
        l = alpha * l + jnp.sum(p0, axis=0, keepdims=True) + jnp.sum(p1, axis=0, keepdims=True)
        acc = (alpha * acc
               + jnp.dot(vt_ref[0, 0, :, pl.ds(tile_start(2 * kp), QT)], p0.astype(jnp.bfloat16),
                         preferred_element_type=jnp.float32)
               + jnp.dot(vt_ref[0, 0, :, pl.ds(tile_start(2 * kp + 1), QT)], p1.astype(jnp.bfloat16),
                         preferred_element_type=jnp.float32))
        return m_new, l, acc, n0, n1

    cols = NSA_REP * tq
    init = (jnp.full((1, cols), NEG, jnp.float32), jnp.zeros((1, cols), jnp.float32),
            jnp.zeros((HEAD_DIM, cols), jnp.float32), logits(0), logits(1))
    m, l, acc, _, _ = lax.fori_loop(0, n // 2 + 1, body, init)
    out = acc / l
    for r in range(NSA_REP):
        o_ref[0, :, r * HEAD_DIM:(r + 1) * HEAD_DIM] = out[:, r * tq:(r + 1) * tq].T.astype(o_ref.dtype)


def _sel_attention(proj3, sel, expand_t, bias_sel_t, lay, seq):
    b = proj3.shape[0]
    g = NSA_KV_GROUPS
    ns = seq // NSA_SEL_LEN
    qw = NSA_REP * HEAD_DIM
    nb = bias_sel_t.shape[1]
    cb_k = lay["kv"] + 2 * g
    cb_v = lay["kv"] + 3 * g
    vt = proj3[:, :, cb_v * LANES:(cb_v + g) * LANES].reshape(b, seq, g, HEAD_DIM).transpose(0, 2, 3, 1)
    return pl.pallas_call(
        _sel_attn_kernel,
        grid=(b, g, seq // QT),
        in_specs=[pl.BlockSpec((1, QT, qw), lambda i, j, n: (i, n, lay["q"] // NSA_REP + j)),
                  pl.BlockSpec((1, seq, HEAD_DIM), lambda i, j, n: (i, 0, cb_k + j)),
                  pl.BlockSpec((1, 1, HEAD_DIM, seq), lambda i, j, n: (i, j, 0, 0)),
                  pl.BlockSpec((1, 1, ns, QT), lambda i, j, n: (i, j, 0, n)),
                  pl.BlockSpec((seq, ns), lambda i, j, n: (0, 0)),
                  pl.BlockSpec((NSA_REP, nb, QT, QT), lambda i, j, n: (j, 0, 0, 0))],
        out_specs=pl.BlockSpec((1, QT, qw), lambda i, j, n: (i, n, j)),
        out_shape=jax.ShapeDtypeStruct((b, seq, NSA_HEADS * HEAD_DIM), jnp.bfloat16),
        scratch_shapes=[pltpu.VMEM((seq, QT), jnp.float32)],
        compiler_params=_cparams(("arbitrary", "arbitrary", "arbitrary")),
        name="nsa_selected",
    )(proj3, proj3, vt, sel, expand_t, bias_sel_t)


def _band_softmax(qs, k_tile, v_tile, bias, n, n_prev, scale):
    logits = []
    for j in range(n_prev + 1):
        kb = n - n_prev + j
        k = k_tile(jnp.maximum(kb, 0))
        s = lax.dot_general(qs, k, (((1,), (1,)), ((), ())), preferred_element_type=jnp.float32) * scale
        bj = jnp.where(kb >= 0, bias[:, j * QT:(j + 1) * QT], NEG)
        logits.append(jnp.where(bj > 0.5 * NEG, s + bj, NEG))
    m = logits[0].max(axis=-1, keepdims=True)
    for s in logits[1:]:
        m = jnp.maximum(m, s.max(axis=-1, keepdims=True))
    l = jnp.zeros_like(m)
    acc = jnp.zeros((qs.shape[0], HEAD_DIM), jnp.float32)
    for j, s in enumerate(logits):
        p = jnp.exp(s - m)
        l = l + jnp.sum(p, axis=-1, keepdims=True)
        acc = acc + jnp.dot(p.astype(jnp.bfloat16), v_tile(jnp.maximum(n - n_prev + j, 0)),
                            preferred_element_type=jnp.float32)
    return acc, m, l


def _win_attn_kernel(q_ref, k_ref, v_ref, bias_ref, o_ref, *, n_prev):
    tq = q_ref.shape[1]
    n = pl.program_id(2)
    qs = _stack_heads(q_ref[0])
    bias = jnp.concatenate([bias_ref[r, 0] for r in range(NSA_REP)], axis=0)
    tile = lambda ref: (lambda kb: ref[0, pl.ds(pl.multiple_of(kb * QT, QT), QT), :])
    acc, _, l = _band_softmax(qs, tile(k_ref), tile(v_ref), bias, n, n_prev, HEAD_DIM ** -0.5)
    o = acc / l
    for r in range(NSA_REP):
        o_ref[0, :, r * HEAD_DIM:(r + 1) * HEAD_DIM] = o[r * tq:(r + 1) * tq].astype(o_ref.dtype)


def _win_attention(proj3, bias_win, lay, seq):
    b = proj3.shape[0]
    g = NSA_KV_GROUPS
    qw = NSA_REP * HEAD_DIM
    n_prev = -(-(NSA_WINDOW - 1) // QT)
    cb_k = lay["kv"] + 4 * g
    cb_v = lay["kv"] + 5 * g
    return pl.pallas_call(
        functools.partial(_win_attn_kernel, n_prev=n_prev),
        grid=(b, g, seq // QT),
        in_specs=[pl.BlockSpec((1, QT, qw), lambda i, j, n: (i, n, lay["q"] // NSA_REP + j)),
                  pl.BlockSpec((1, seq, HEAD_DIM), lambda i, j, n: (i, 0, cb_k + j)),
                  pl.BlockSpec((1, seq, HEAD_DIM), lambda i, j, n: (i, 0, cb_v + j)),
                  pl.BlockSpec((NSA_REP, 1, QT, (n_prev + 1) * QT), lambda i, j, n: (j, 0, 0, 0))],
        out_specs=pl.BlockSpec((1, QT, qw), lambda i, j, n: (i, n, j)),
        out_shape=jax.ShapeDtypeStruct((b, seq, NSA_HEADS * HEAD_DIM), jnp.bfloat16),
        compiler_params=_cparams(("arbitrary", "arbitrary", "arbitrary")),
        name="nsa_window",
    )(proj3, proj3, proj3, bias_win)


DIL_TILES_PER_STEP = 4


def _dil_attn_kernel(q_ref, k_ref, v_ref, bias_ref, o_ref, lse_ref, *, n_prev, dil):
    tiles = q_ref.shape[1] // (QT * dil)
    for sub in range(DIL_TILES_PER_STEP):
        t = pl.program_id(2) * DIL_TILES_PER_STEP + sub
        r = t // tiles
        n = t % tiles
        rows = lambda kb, r=r: pl.ds(kb * (QT * dil) + r, QT, stride=dil) if dil > 1 else pl.ds(
            pl.multiple_of(kb * QT, QT), QT)
        tile = lambda ref, rows=rows: (lambda kb: ref[0, rows(kb), :].astype(jnp.bfloat16))
        acc, m, l = _band_softmax(tile(q_ref)(n), tile(k_ref), tile(v_ref), bias_ref[0, 0], n, n_prev,
                                  HEAD_DIM ** -0.5)
        o_ref[0, rows(n), :] = acc / l
        lse_ref[0, rows(n), :] = jnp.broadcast_to(m + jnp.log(l), (QT, HEAD_DIM))


def _dil_attention(projd3, bias_dil, seq, gi):
    b = projd3.shape[0]
    window, dil = DIL_CONFIGS[gi]
    p_heads = DIL_HEADS_PER_GROUP
    n_prev = -(-(window // dil) // QT)
    cb_q = gi * p_heads
    cb_k = cb_q + DIL_HEADS
    cb_v = cb_k + DIL_HEADS
    out_w = p_heads * HEAD_DIM
    whole = lambda cb: pl.BlockSpec((1, seq, HEAD_DIM), lambda i, p, t: (i, 0, cb + p))
    return pl.pallas_call(
        functools.partial(_dil_attn_kernel, n_prev=n_prev, dil=dil),
        grid=(b, p_heads, seq // (QT * DIL_TILES_PER_STEP)),
        in_specs=[whole(cb_q), whole(cb_k), whole(cb_v),
                  pl.BlockSpec((1, 1, QT, (n_prev + 1) * QT), lambda i, p, t: (p, 0, 0, 0))],
        out_specs=[whole(0), whole(0)],
        out_shape=[jax.ShapeDtypeStruct((b, seq, out_w), jnp.float32),
                   jax.ShapeDtypeStruct((b, seq, out_w), jnp.float32)],
        compiler_params=_cparams(("arbitrary",) * 3),
        name=f"dilated_attn_{gi}",
    )(projd3, projd3, projd3, bias_dil)


def _post_kernel(ocmp_ref, osel_ref, owin_ref, gate_ref, od0_ref, od1_ref, od2_ref, l0_ref, l1_ref, l2_ref,
                 mg1_ref, mg2_ref, x_ref, g1_ref, sc2_ref, sh2_ref, n2_ref, wn_ref, wd_ref, wo_ref,
                 x1_ref, h2_ref):
    gates = _sigmoid(gate_ref[...].astype(jnp.float32))
    parts = []
    for h in range(NSA_HEADS):
        sl = slice(h * HEAD_DIM, (h + 1) * HEAD_DIM)
        y = (gates[:, 3 * h:3 * h + 1] * ocmp_ref[:, sl].astype(jnp.float32)
             + gates[:, 3 * h + 1:3 * h + 2] * osel_ref[:, sl].astype(jnp.float32)
             + gates[:, 3 * h + 2:3 * h + 3] * owin_ref[:, sl].astype(jnp.float32))
        parts.append(y.astype(jnp.bfloat16))
    y_nsa = jnp.concatenate(parts, axis=1)

    l0, l1, l2 = l0_ref[...], l1_ref[...], l2_ref[...]
    mx = jnp.maximum(jnp.maximum(l0, l1), l2)
    e0, e1, e2 = jnp.exp(l0 - mx), jnp.exp(l1 - mx), jnp.exp(l2 - mx)
    den = e0 + e1 + e2
    y_dil = ((e0 / den) * od0_ref[...].astype(jnp.float32) + (e1 / den) * od1_ref[...].astype(jnp.float32)
             + (e2 / den) * od2_ref[...].astype(jnp.float32)).astype(jnp.bfloat16)

    a = jnp.dot(y_nsa, wn_ref[...], preferred_element_type=jnp.float32)
    bb = jnp.dot(y_dil, wd_ref[...], preferred_element_type=jnp.float32)
    merged = (_sigmoid(mg1_ref[...].astype(jnp.float32)) * a
              + _sigmoid(mg2_ref[...].astype(jnp.float32)) * bb).astype(jnp.bfloat16)
    mix = jnp.dot(merged, wo_ref[...], preferred_element_type=jnp.float32)
    x1 = x_ref[...] + g1_ref[0] * mix
    x1_ref[...] = x1
    y = x1 * lax.rsqrt(jnp.mean(x1 * x1, axis=-1, keepdims=True) + EPS) * n2_ref[...]
    h2_ref[...] = y * (1.0 + sc2_ref[0]) + sh2_ref[0]


def _post(ocmp, osel, owin, proj2d, odil, lsedil, x2d, g1, sc2, sh2, norm2_g, wn, wd, wo, lay, seq):
    t, d = x2d.shape
    tm = 256
    per_batch = seq // tm
    nw = NSA_HEADS * HEAD_DIM
    dw = DIL_HEADS_PER_GROUP * HEAD_DIM
    row = lambda w: pl.BlockSpec((tm, w), lambda i: (i, 0))
    mod = pl.BlockSpec((1, 1, d), lambda i: (i // per_batch, 0, 0))
    full = lambda a: pl.BlockSpec(a.shape, lambda i: (0, 0))
    return pl.pallas_call(
        _post_kernel,
        grid=(t // tm,),
        in_specs=[row(nw), row(nw), row(nw),
                  pl.BlockSpec((tm, LANES), lambda i: (i, lay["gate"])),
                  row(dw), row(dw), row(dw), row(dw), row(dw), row(dw),
                  pl.BlockSpec((tm, d), lambda i: (i, 0)),
                  pl.BlockSpec((tm, d), lambda i: (i, 1)),
                  row(d), mod, mod, mod,
                  pl.BlockSpec((1, d), lambda i: (0, 0)),
                  full(wn), full(wd), full(wo)],
        out_specs=[row(d), row(d)],
        out_shape=[jax.ShapeDtypeStruct((t, d), jnp.float32), jax.ShapeDtypeStruct((t, d), jnp.float32)],
        compiler_params=_cparams(("arbitrary",)),
        name="merge_out_proj",
    )(ocmp, osel, owin, proj2d, odil[0], odil[1], odil[2], lsedil[0], lsedil[1], lsedil[2],
      proj2d, proj2d, x2d, g1, sc2, sh2, norm2_g.reshape(1, d), wn, wd, wo)


def _topk_rows(sc, k, payload=None):
    n = sc.shape[0]
    row = lax.broadcasted_iota(jnp.int32, sc.shape, 0).astype(jnp.float32)
    vals, picks = [], []
    for _ in range(k):
        mx = jnp.max(sc, axis=0, keepdims=True)
        first = jnp.min(jnp.where(sc == mx, row, float(n)), axis=0, keepdims=True)
        hit = row == first
        vals.append(mx)
        if payload is None:
            picks.append(first)
        else:
            picks.append(jnp.sum(jnp.where(hit, payload, 0.0), axis=0, keepdims=True))
        sc = jnp.where(hit, -jnp.inf, sc)
    return jnp.concatenate(vals, axis=0), jnp.concatenate(picks, axis=0)


def _route_kernel(h_ref, wq_ref, gain_ref, keys_ref, idx_ref, gate_ref):
    q = jnp.dot(h_ref[...].astype(jnp.bfloat16), wq_ref[...], preferred_element_type=jnp.float32)
    k = PEER_TOPK
    for head in range(PEER_HEADS):
        tops = []
        for part in range(2):
            hp = head * 2 + part
            a = q[:, hp * PEER_HALF:(hp + 1) * PEER_HALF]
            qn = (a * lax.rsqrt(jnp.mean(a * a, axis=-1, keepdims=True) + EPS)
                  * gain_ref[hp:hp + 1, :]).astype(jnp.bfloat16)
            sc = lax.dot_general(keys_ref[hp], qn, (((1,), (1,)), ((), ())),
                                 preferred_element_type=jnp.float32)
            tops.append(_topk_rows(sc, k))
        (s1, i1), (s2, i2) = tops
        half = k // 2
        sub = lax.broadcasted_iota(jnp.int32, (half, s1.shape[1]), 0)
        vals = [s1[0:1] + s2]
        ids = [i1[0:1] * float(PEER_N_KEYS) + i2]
        for a in range(1, half):
            vals.append(jnp.where(sub < k // (a + 1), s1[a:a + 1] + s2[:half], -jnp.inf))
            ids.append(i1[a:a + 1] * float(PEER_N_KEYS) + i2[:half])
        vals.append(s1[half:] + s2[0:1])
        ids.append(i1[half:] * float(PEER_N_KEYS) + i2[0:1])
        top_s, expert = _topk_rows(jnp.concatenate(vals, axis=0), k, payload=jnp.concatenate(ids, axis=0))
        e = jnp.exp(top_s - top_s[0:1])
        gate = e / jnp.sum(e, axis=0, keepdims=True)
        idx_ref[head * k:(head + 1) * k, :] = expert.astype(jnp.int32)
        gate_ref[head * k:(head + 1) * k, :] = gate


def _route(h2, wq_bf16, q_gain, sub_keys_bf16):
    t, d = h2.shape
    tm = LANES
    hk = PEER_HEADS * PEER_TOPK
    nq = wq_bf16.shape[1]
    return pl.pallas_call(
        _route_kernel,
        grid=(t // tm,),
        in_specs=[pl.BlockSpec((tm, d), lambda i: (i, 0)),
                  pl.BlockSpec((d, nq), lambda i: (0, 0)),
                  pl.BlockSpec((2 * PEER_HEADS, PEER_HALF), lambda i: (0, 0)),
                  pl.BlockSpec((2 * PEER_HEADS, PEER_N_KEYS, PEER_HALF), lambda i: (0, 0, 0))],
        out_specs=[pl.BlockSpec((hk, tm), lambda i: (0, i)),
                   pl.BlockSpec((hk, tm), lambda i: (0, i))],
        out_shape=[jax.ShapeDtypeStruct((hk, t), jnp.int32), jax.ShapeDtypeStruct((hk, t), jnp.float32)],
        compiler_params=_cparams(("arbitrary",)),
        name="peer_route",
    )(h2, wq_bf16, q_gain.reshape(2 * PEER_HEADS, PEER_HALF), sub_keys_bf16)


PEER_TB = 32
SUBLANES = 8


def _expert_kernel(idx_ref, nidx_ref, h_ref, gate_ref, x1_ref, g2_ref, tab_ref, o_ref, stage_a, stage_b, sem):
    tb, d = h_ref.shape
    k = PEER_TOPK
    hk = PEER_HEADS * k
    nr, nc = tb // SUBLANES, d // LANES
    stages = (stage_a, stage_b)
    step = pl.program_id(0)
    hi_mask = jnp.uint32(0xFFFF0000)

    def start_rows(ids, head, j, r):
        for s in range(SUBLANES):
            e = ids[0, 0, (r * SUBLANES + s) * hk + head * k + j]
            pltpu.make_async_copy(tab_ref.at[pl.ds(e, 1), :],
                                  stages[head % 2].at[pl.ds((j * nr + r) * SUBLANES + s, 1), :],
                                  sem.at[head % 2]).start(priority=s % 2)

    def words(stage, j, r, c):
        row0 = (j * nr + r) * SUBLANES
        return stage[row0:row0 + SUBLANES, c * LANES:(c + 1) * LANES]

    def start_head(ids, head):
        for j in range(k):
            for r in range(nr):
                start_rows(ids, head, j, r)

    def wait_head(head):
        buf = stages[head % 2]
        pltpu.make_async_copy(buf, buf, sem.at[head % 2]).wait()

    @pl.when(step == 0)
    def _():
        start_head(idx_ref, 0)

    lane = lax.broadcasted_iota(jnp.int32, (SUBLANES, LANES), 1)
    for head in range(PEER_HEADS):
        stage = stages[head % 2]
        wait_head(head)
        if head + 1 == PEER_HEADS:
            @pl.when(step + 1 < pl.num_programs(0))
            def _():
                start_head(nidx_ref, 0)
        acts = []
        for r in range(nr):
            rows = slice(r * SUBLANES, (r + 1) * SUBLANES)
            pre = jnp.zeros((SUBLANES, LANES), jnp.float32)
            for j in range(k):
                if head + 1 < PEER_HEADS:
                    start_rows(idx_ref, head + 1, j, r)
                acc = None
                for c in range(nc):
                    u = lax.bitcast_convert_type(words(stage, j, r, c) & hi_mask, jnp.float32)
                    prod = h_ref[rows, c * LANES:(c + 1) * LANES] * u
                    acc = prod if acc is None else acc + prod
                pre = jnp.where(lane == head * k + j, jnp.sum(acc, axis=-1, keepdims=True), pre)
            acts.append(gate_ref[rows, :] * _gelu(pre))
        for r in range(nr):
            rows = slice(r * SUBLANES, (r + 1) * SUBLANES)
            accs = [None] * nc
            for j in range(k):
                a = jnp.broadcast_to(acts[r][:, head * k + j:head * k + j + 1], (SUBLANES, LANES))
                for c in range(nc):
                    term = a * lax.bitcast_convert_type(words(stage, j, r, c) << 16, jnp.float32)
                    accs[c] = term if accs[c] is None else accs[c] + term
            for c in range(nc):
                cs = slice(c * LANES, (c + 1) * LANES)
                if head == 0:
                    o_ref[rows, cs] = accs[c]
                else:
                    o_ref[rows, cs] = o_ref[rows, cs] + accs[c]
    o_ref[...] = x1_ref[...] + g2_ref[0] * o_ref[...]


def _experts(idx_tok, gate_tok, h2, x1, g2, table, seq):
    t, d = h2.shape
    tb = PEER_TB
    hk = PEER_HEADS * PEER_TOPK
    per_batch = seq // tb
    nsteps = t // tb
    ids = idx_tok.reshape(nsteps, 1, tb * hk)
    stage = pltpu.VMEM((PEER_TOPK * tb, d), jnp.uint32)
    return pl.pallas_call(
        _expert_kernel,
        grid=(nsteps,),
        in_specs=[pl.BlockSpec((1, 1, tb * hk), lambda i: (i, 0, 0), memory_space=pltpu.SMEM),
                  pl.BlockSpec((1, 1, tb * hk), lambda i: (jnp.minimum(i + 1, nsteps - 1), 0, 0),
                               memory_space=pltpu.SMEM),
                  pl.BlockSpec((tb, d), lambda i: (i, 0)),
                  pl.BlockSpec((tb, hk), lambda i: (i, 0)),
                  pl.BlockSpec((tb, d), lambda i: (i, 0)),
                  pl.BlockSpec((1, 1, d), lambda i: (i // per_batch, 0, 0)),
                  pl.BlockSpec(memory_space=pl.ANY)],
        out_specs=pl.BlockSpec((tb, d), lambda i: (i, 0)),
        out_shape=jax.ShapeDtypeStruct((t, d), jnp.float32),
        scratch_shapes=[stage, stage, pltpu.SemaphoreType.DMA((2,))],
        compiler_params=_cparams(("arbitrary",)),
        name="peer_experts",
    )(ids, ids, h2, gate_tok, x1, g2, table)


def _pack_expert_table(u, v):
    ub = lax.bitcast_convert_type(u.astype(jnp.bfloat16), jnp.uint16).astype(jnp.uint32)
    vb = lax.bitcast_convert_type(v.astype(jnp.bfloat16), jnp.uint16).astype(jnp.uint32)
    return (ub << 16) | vb


def _proj_weights(w_in, nsa_q_gain, nsa_k_gain, dil_q_gain, dil_k_gain, d_model, lay):
    dh = HEAD_DIM
    off_kv = NSA_HEADS * dh
    off_gate = off_kv + 6 * NSA_KV_GROUPS * dh
    off_dil = off_gate + 3 * NSA_HEADS
    off_merge = off_dil + 3 * DIL_HEADS * dh
    n_cols = lay["n"] * LANES
    pad = n_cols - (lay["gate"] * LANES + 3 * NSA_HEADS)
    w = jnp.concatenate([w_in[:, off_merge:], w_in[:, :off_kv], w_in[:, off_kv:off_gate],
                         w_in[:, off_gate:off_dil], jnp.zeros((d_model, pad), w_in.dtype)],
                        axis=1).astype(jnp.bfloat16)
    w_dil = w_in[:, off_dil:off_merge].astype(jnp.bfloat16)
    ones = jnp.ones((dh,), jnp.float32)

    def rows(spec):
        gains = jnp.concatenate([g for g, _, n in spec for _ in range(n)])
        flags = jnp.concatenate([jnp.full((dh,), f, jnp.float32) for _, f, n in spec for _ in range(n)])
        return flags.reshape(1, -1), gains.reshape(1, -1)

    g = NSA_KV_GROUPS
    main = [(ones, 0.0, lay["q"]), (nsa_q_gain, 1.0, NSA_HEADS), (ones, 0.0, 2 * g),
            (nsa_k_gain[1], 1.0, g), (ones, 0.0, g), (nsa_k_gain[2], 1.0, g), (ones, 0.0, g),
            (ones, 0.0, lay["n"] - lay["gate"])]
    dil = [(dil_q_gain, 1.0, DIL_HEADS), (dil_k_gain, 1.0, DIL_HEADS), (ones, 0.0, DIL_HEADS)]
    return (w,) + rows(main), (w_dil,) + rows(dil)


def _token_mixer_and_norm2(x, mod, norm1_g, w_in, nsa_q_gain, nsa_k_gain, cmp_pos, cmp_w1, cmp_w2,
                           dil_q_gain, dil_k_gain, w_br_nsa, w_br_dil, w_out, norm2_g, rel_bias):
    b, seq, d = x.shape
    t = b * seq
    lay = _layout(d)
    sh1, sc1, g1, sh2, sc2, g2 = [m.reshape(b, 1, d) for m in jnp.split(mod, 6, axis=-1)]
    x2d = x.reshape(t, d)

    main_w, dil_w = _proj_weights(w_in, nsa_q_gain, nsa_k_gain, dil_q_gain, dil_k_gain, d, lay)
    proj2d = _projection(x2d, norm1_g, sc1, sh1, *main_w, seq, jnp.bfloat16)
    proj3 = proj2d.reshape(b, seq, lay["n"] * LANES)
    projd3 = _projection(x2d, norm1_g, sc1, sh1, *dil_w, seq, jnp.float32).reshape(b, seq, -1)

    rel_flat = rel_bias.reshape(-1)
    nchunk = seq // NSA_CMP_STRIDE
    ns = seq // NSA_SEL_LEN
    bias_win = _bias_table(rel_flat, _band_buckets(-(-(NSA_WINDOW - 1) // QT), NSA_WINDOW - 1, 1), 0, NSA_HEADS)
    bias_sel = _bias_table(rel_flat, _causal_buckets(_n_causal_tiles(seq)), 0, NSA_HEADS)
    bias_cmp = _bias_table(rel_flat, _cmp_buckets(seq, nchunk), 0, NSA_HEADS)
    bias_dil = [_bias_table(rel_flat, _band_buckets(-(-(wnd // dil) // QT), wnd // dil, dil),
                            NSA_HEADS + gi * DIL_HEADS_PER_GROUP, DIL_HEADS_PER_GROUP)
                for gi, (wnd, dil) in enumerate(DIL_CONFIGS)]

    g = NSA_KV_GROUPS
    c0 = lay["kv"] * LANES
    tc = proj3[:, :, c0:c0 + 2 * g * HEAD_DIM].reshape(b, nchunk, NSA_CMP_STRIDE, 2 * g, HEAD_DIM)
    tc = tc.transpose(0, 3, 1, 2, 4).reshape(b, 2 * g, nchunk, NSA_CMP_STRIDE * HEAD_DIM)
    kcvc = _compress(tc, cmp_pos, cmp_w1, cmp_w2, nsa_k_gain[0])

    cstart = np.arange(nchunk) * NSA_CMP_STRIDE
    sstart = np.arange(ns) * NSA_SEL_LEN
    n_cmp = (seq - NSA_CMP_LEN) // NSA_CMP_STRIDE + 1
    overlap = ((cstart[:, None] < sstart[None, :] + NSA_SEL_LEN) & (cstart[:, None] + NSA_CMP_LEN > sstart[None, :])
               & (np.arange(nchunk)[:, None] < n_cmp)).astype(np.float32)
    expand_t = (np.arange(seq)[:, None] // NSA_SEL_LEN == np.arange(ns)[None, :]).astype(np.float32)

    ocmp, sel = _cmp_select(proj3, kcvc, bias_cmp, jnp.asarray(overlap.T), lay, seq)
    osel = _sel_attention(proj3, sel, jnp.asarray(expand_t, jnp.bfloat16), bias_sel, lay, seq)
    owin = _win_attention(proj3, bias_win, lay, seq)
    odil, lsedil = zip(*[_dil_attention(projd3, bias_dil[gi], seq, gi) for gi in range(len(DIL_CONFIGS))])

    x1, h2 = _post(ocmp.reshape(t, -1), osel.reshape(t, -1), owin.reshape(t, -1), proj2d,
                   [o.reshape(t, -1) for o in odil], [l.reshape(t, -1) for l in lsedil],
                   x2d, g1, sc2, sh2, norm2_g, w_br_nsa.astype(jnp.bfloat16), w_br_dil.astype(jnp.bfloat16),
                   w_out.astype(jnp.bfloat16), lay, seq)
    return x1, h2, g2


def kernel(x, c, w_ada, b_ada, norm1_g, w_in, nsa_q_gain, nsa_k_gain, cmp_pos, cmp_w1, cmp_w2,
           dil_q_gain, dil_k_gain, w_br_nsa, w_br_dil, w_out, norm2_g, peer_w_q, peer_q_gain,
           peer_sub_keys, peer_u, peer_v, rel_bias):
    b, seq, d = x.shape
    depth = w_ada.shape[0]
    for layer in range(depth):
        mod = _modulation(c, w_ada[layer], b_ada[layer])
        x1, h2, g2 = _token_mixer_and_norm2(
            x, mod, norm1_g[layer], w_in[layer], nsa_q_gain[layer], nsa_k_gain[layer], cmp_pos[layer],
            cmp_w1[layer], cmp_w2[layer], dil_q_gain[layer], dil_k_gain[layer], w_br_nsa[layer],
            w_br_dil[layer], w_out[layer], norm2_g[layer], rel_bias)
        idx, gate = _route(h2, peer_w_q[layer].astype(jnp.bfloat16), peer_q_gain[layer],
                           peer_sub_keys[layer].reshape(2 * PEER_HEADS, PEER_N_KEYS, PEER_HALF).astype(jnp.bfloat16))
        table = _pack_expert_table(peer_u[layer], peer_v[layer])
        out = _experts(idx.T, gate.T, h2, x1, g2, table, seq)
        x = out.reshape(b, seq, d)
    return x
```

```python
import functools
import math

import numpy as np
import jax
import jax.numpy as jnp
from jax import lax
from jax.experimental import pallas as pl
from jax.experimental.pallas import tpu as pltpu

HEAD_DIM = 128
LANES = 128
NSA_HEADS = 8
NSA_KV_GROUPS = 2
NSA_REP = NSA_HEADS // NSA_KV_GROUPS
NSA_CMP_LEN = 32
NSA_CMP_STRIDE = 16
NSA_SEL_LEN = 64
NSA_TOP_N = 16
NSA_WINDOW = 512
DIL_CONFIGS = ((128, 1), (512, 4), (2048, 16))
DIL_HEADS_PER_GROUP = 4
DIL_HEADS = DIL_HEADS_PER_GROUP * len(DIL_CONFIGS)
REL_BUCKETS = 32
REL_MAX_DIST = 2048
PEER_HEADS = 8
PEER_N_KEYS = 128
PEER_TOPK = 16
PEER_HALF = 128
EPS = 1e-6
NEG = -1e30
SEL_FORCE = 1e4
MASKED_BUCKET = REL_BUCKETS

QT = 128
VMEM_LIMIT = 48 * 1024 * 1024

CB_MERGE = 0
def _layout(d_model):
    nmerge = 2 * d_model // LANES
    cb_q = nmerge
    cb_kv = cb_q + NSA_HEADS
    cb_gate = cb_kv + 6 * NSA_KV_GROUPS
    n_used = cb_gate + 1
    n_blocks = -(-n_used // 4) * 4
    return dict(q=cb_q, kv=cb_kv, gate=cb_gate, n=n_blocks)


def _cparams(sem):
    return pltpu.CompilerParams(dimension_semantics=sem, vmem_limit_bytes=VMEM_LIMIT)


def _t5_bucket_np(dist):
    n = np.maximum(dist, 0)
    max_exact = REL_BUCKETS // 2
    nf = np.maximum(n, max_exact).astype(np.float32)
    large = max_exact + (np.log(nf / np.float32(max_exact)) / np.float32(math.log(REL_MAX_DIST / max_exact))
                         * np.float32(REL_BUCKETS - max_exact)).astype(np.int32)
    large = np.minimum(large, REL_BUCKETS - 1)
    return np.where(n < max_exact, n, large).astype(np.int32)


def _gelu(x):
    return jax.nn.gelu(x)


def _sigmoid(x):
    return 1.0 / (1.0 + jnp.exp(-x))


def _mod_kernel(c_ref, w_ref, b_ref, o_ref):
    c = c_ref[...]
    cond = c * _sigmoid(c)
    o_ref[...] = jnp.dot(cond.astype(jnp.bfloat16), w_ref[...].astype(jnp.bfloat16),
                         preferred_element_type=jnp.float32) + b_ref[...]


def _modulation(c, w_ada, b_ada):
    b, d = c.shape
    n = w_ada.shape[1]
    rows = 8
    cp = jnp.zeros((rows, d), jnp.float32).at[:b].set(c)
    tn = 512
    out = pl.pallas_call(
        _mod_kernel,
        grid=(n // tn,),
        in_specs=[pl.BlockSpec((rows, d), lambda j: (0, 0)),
                  pl.BlockSpec((d, tn), lambda j: (0, j)),
                  pl.BlockSpec((1, tn), lambda j: (0, j))],
        out_specs=pl.BlockSpec((rows, tn), lambda j: (0, j)),
        out_shape=jax.ShapeDtypeStruct((rows, n), jnp.float32),
        compiler_params=_cparams(("arbitrary",)),
        name="adaln_mod",
    )(cp, w_ada, b_ada.reshape(1, n))
    return out[:b]


def _proj_kernel(x_ref, g_ref, sc_ref, sh_ref, w_ref, flag_ref, gain_ref, o_ref, h_ref, *, plain):
    j = pl.program_id(1)

    @pl.when(j == 0)
    def _():
        x = x_ref[...]
        y = x * lax.rsqrt(jnp.mean(x * x, axis=-1, keepdims=True) + EPS) * g_ref[...]
        h_ref[...] = (y * (1.0 + sc_ref[0]) + sh_ref[0]).astype(jnp.bfloat16)

    acc = jnp.dot(h_ref[...], w_ref[...], preferred_element_type=jnp.float32)
    is_plain = (j >= plain[0]) & (j < plain[1])

    @pl.when(is_plain)
    def _():
        o_ref[...] = acc.astype(o_ref.dtype)

    @pl.when(jnp.logical_not(is_plain))
    def _():
        for k in range(acc.shape[1] // LANES):
            a = acc[:, k * LANES:(k + 1) * LANES]
            f = flag_ref[:, k * LANES:(k + 1) * LANES]
            r = lax.rsqrt(jnp.mean(a * a, axis=-1, keepdims=True) + EPS)
            scale = f * r + (1.0 - f)
            o_ref[:, k * LANES:(k + 1) * LANES] = (a * scale * gain_ref[:, k * LANES:(k + 1) * LANES]).astype(o_ref.dtype)


def _projection(x2d, norm_g, sc, sh, w_bf16, flags, gains, seq, out_dtype, plain):
    t, d = x2d.shape
    n = w_bf16.shape[1]
    tm = min(1024, seq)
    tn = 512
    per_batch = seq // tm
    return pl.pallas_call(
        functools.partial(_proj_kernel, plain=plain),
        grid=(t // tm, n // tn),
        in_specs=[pl.BlockSpec((tm, d), lambda i, j: (i, 0)),
                  pl.BlockSpec((1, d), lambda i, j: (0, 0)),
                  pl.BlockSpec((1, 1, d), lambda i, j: (i // per_batch, 0, 0)),
                  pl.BlockSpec((1, 1, d), lambda i, j: (i // per_batch, 0, 0)),
                  pl.BlockSpec((d, tn), lambda i, j: (0, j)),
                  pl.BlockSpec((1, tn), lambda i, j: (0, j)),
                  pl.BlockSpec((1, tn), lambda i, j: (0, j))],
        out_specs=pl.BlockSpec((tm, tn), lambda i, j: (i, j)),
        out_shape=jax.ShapeDtypeStruct((t, n), out_dtype),
        scratch_shapes=[pltpu.VMEM((tm, d), jnp.bfloat16)],
        compiler_params=_cparams(("arbitrary", "arbitrary")),
        name="norm_in_proj",
    )(x2d, norm_g.reshape(1, d), sc, sh, w_bf16, flags, gains)


def _bias_kernel(tbl_ref, idx_ref, o_ref, *, head0):
    h = pl.program_id(0) + head0
    b = idx_ref[0]
    out = jnp.full(b.shape, NEG, jnp.float32)
    for k in range(REL_BUCKETS):
        out = jnp.where(b == k, tbl_ref[h * REL_BUCKETS + k], out)
    o_ref[0, 0] = out


def _bias_table(rel_bias_flat, bucket_idx, head0, n_heads):
    nt, r, c = bucket_idx.shape
    return pl.pallas_call(
        functools.partial(_bias_kernel, head0=head0),
        grid=(n_heads, nt),
        in_specs=[pl.BlockSpec(memory_space=pltpu.SMEM),
                  pl.BlockSpec((1, r, c), lambda h, t: (t, 0, 0))],
        out_specs=pl.BlockSpec((1, 1, r, c), lambda h, t: (h, t, 0, 0)),
        out_shape=jax.ShapeDtypeStruct((n_heads, nt, r, c), jnp.float32),
        compiler_params=_cparams(("arbitrary", "arbitrary")),
        name="rel_bias_table",
    )(rel_bias_flat, jnp.asarray(bucket_idx))


def _band_buckets(n_prev, span, dist_scale):
    i = np.arange(QT)[:, None]
    c = np.arange((n_prev + 1) * QT)[None, :]
    dist = n_prev * QT + i - c
    ok = (dist >= 0) & (dist <= span)
    return np.where(ok, _t5_bucket_np(dist * dist_scale), MASKED_BUCKET).astype(np.int32)[None]


def _causal_buckets(n_tiles):
    i = np.arange(QT)[None, :]
    j = np.arange(QT)[:, None]
    out = []
    for o in range(n_tiles):
        dist = QT * o + i - j
        out.append(np.where(dist >= 0, _t5_bucket_np(dist), MASKED_BUCKET))
    return np.stack(out).astype(np.int32)


def _n_causal_tiles(seq):
    first_last = int(np.argmax(_t5_bucket_np(np.arange(4 * REL_MAX_DIST)) == REL_BUCKETS - 1))
    o_const = -(-(first_last + QT - 1) // QT)
    return min(seq // QT, o_const + 1)


def _cmp_buckets(seq, n_cmp_pad):
    nq = seq // QT
    n_cmp = (seq - NSA_CMP_LEN) // NSA_CMP_STRIDE + 1
    i = np.arange(QT)[None, :, None]
    n = np.arange(nq)[:, None, None]
    c = np.arange(n_cmp_pad)[None, None, :]
    dist = n * QT + i - (c * NSA_CMP_STRIDE + NSA_CMP_LEN - 1)
    ok = (dist >= 0) & (c < n_cmp)
    return np.where(ok, _t5_bucket_np(dist), MASKED_BUCKET).astype(np.int32)


def _compress_kernel(t_ref, pos_ref, w1_ref, w2_ref, gain_ref, o_ref):
    half = t_ref.shape[3]
    a = t_ref[0, 0].astype(jnp.float32)
    lo = (a + pos_ref[0, :, :half]).astype(jnp.bfloat16)
    hi = (a + pos_ref[0, :, half:]).astype(jnp.bfloat16)
    p = jnp.dot(lo, w1_ref[0, :half, :], preferred_element_type=jnp.float32)
    q = jnp.dot(hi, w1_ref[0, half:, :], preferred_element_type=jnp.float32)
    n = q.shape[0]
    q_next = jnp.concatenate([q[1:], q[:1]], axis=0)
    hid = _gelu(p + q_next)
    out = jnp.dot(hid.astype(jnp.bfloat16), w2_ref[0], preferred_element_type=jnp.float32)
    is_key = pl.program_id(1) < NSA_KV_GROUPS
    r = lax.rsqrt(jnp.mean(out * out, axis=-1, keepdims=True) + EPS)
    normed = out * r * gain_ref[...]
    o_ref[0, 0] = jnp.where(is_key, normed, out).astype(o_ref.dtype)


def _compress(t_chunks, cmp_pos, cmp_w1, cmp_w2, k_gain0):
    b, nh, nchunk, width = t_chunks.shape
    dh = HEAD_DIM
    g = NSA_KV_GROUPS
    return pl.pallas_call(
        _compress_kernel,
        grid=(b, nh),
        in_specs=[pl.BlockSpec((1, 1, nchunk, width), lambda i, h: (i, h, 0, 0)),
                  pl.BlockSpec((1, 1, 2 * width), lambda i, h: (h // g, 0, 0)),
                  pl.BlockSpec((1, 2 * width, dh), lambda i, h: (h // g, 0, 0)),
                  pl.BlockSpec((1, dh, dh), lambda i, h: (h // g, 0, 0)),
                  pl.BlockSpec((1, dh), lambda i, h: (0, 0))],
        out_specs=pl.BlockSpec((1, 1, nchunk, dh), lambda i, h: (i, h, 0, 0)),
        out_shape=jax.ShapeDtypeStruct((b, nh, nchunk, dh), jnp.bfloat16),
        compiler_params=_cparams(("arbitrary", "arbitrary")),
        name="nsa_compress",
    )(t_chunks, cmp_pos.reshape(2, 1, 2 * width), cmp_w1.astype(jnp.bfloat16),
      cmp_w2.astype(jnp.bfloat16), k_gain0.reshape(1, dh))


def _stack_heads(q):
    return jnp.concatenate([q[:, r * HEAD_DIM:(r + 1) * HEAD_DIM] for r in range(NSA_REP)], axis=0)


def _cmp_select_kernel(q_ref, kc_ref, vc_ref, bias_ref, ov_ref, o_ref, sel_ref, *, n_sel):
    tq = q_ref.shape[1]
    n = pl.program_id(2)
    scale = HEAD_DIM ** -0.5
    qs = _stack_heads(q_ref[0])
    s = lax.dot_general(qs, kc_ref[0, 0], (((1,), (1,)), ((), ())),
                        preferred_element_type=jnp.float32) * scale
    bias = jnp.concatenate([bias_ref[r, 0] for r in range(NSA_REP)], axis=0)
    valid = bias > 0.5 * NEG
    s = jnp.where(valid, s + bias, NEG)
    m = jnp.max(s, axis=-1, keepdims=True)
    e = jnp.exp(s - m)
    p = e / jnp.sum(e, axis=-1, keepdims=True) * valid.astype(jnp.float32)
    o = jnp.dot(p.astype(jnp.bfloat16), vc_ref[0, 0], preferred_element_type=jnp.float32)
    for r in range(NSA_REP):
        o_ref[0, :, r * HEAD_DIM:(r + 1) * HEAD_DIM] = o[r * tq:(r + 1) * tq].astype(o_ref.dtype)

    psum = p[0:tq]
    for r in range(1, NSA_REP):
        psum = psum + p[r * tq:(r + 1) * tq]
    imp = lax.dot_general(ov_ref[...], psum, (((1,), (1,)), ((), ())), preferred_element_type=jnp.float32,
                          precision=lax.Precision.HIGHEST)
    n_blocks = sel_ref.shape[2]
    jb = lax.broadcasted_iota(jnp.int32, (n_blocks, tq), 0).astype(jnp.float32)
    qpos = lax.broadcasted_iota(jnp.int32, (n_blocks, tq), 1)
    cur = ((n * tq + qpos) // NSA_SEL_LEN).astype(jnp.float32)
    forced = (jb == 0.0) | (jb == cur) | (jb == cur - 1.0)
    score = jnp.where(forced, SEL_FORCE, jnp.where(jb > cur, -SEL_FORCE, imp))
    chosen = jnp.zeros((n_blocks, tq), jnp.float32)
    for _ in range(n_sel):
        mx = jnp.max(score, axis=0, keepdims=True)
        first = jnp.min(jnp.where(score == mx, jb, float(n_blocks)), axis=0, keepdims=True)
        hit = jb == first
        chosen = jnp.where(hit, 1.0, chosen)
        score = jnp.where(hit, -jnp.inf, score)
    sel_ref[0, 0] = chosen


def _cmp_select(proj3, kcvc, cbias, overlap, lay, seq):
    b = proj3.shape[0]
    g = NSA_KV_GROUPS
    ncp = kcvc.shape[2]
    ns = seq // NSA_SEL_LEN
    n_sel = min(NSA_TOP_N, ns)
    qw = NSA_REP * HEAD_DIM
    return pl.pallas_call(
        functools.partial(_cmp_select_kernel, n_sel=n_sel),
        grid=(b, g, seq // QT),
        in_specs=[pl.BlockSpec((1, QT, qw), lambda i, j, n: (i, n, lay["q"] // NSA_REP + j)),
                  pl.BlockSpec((1, 1, ncp, HEAD_DIM), lambda i, j, n: (i, j, 0, 0)),
                  pl.BlockSpec((1, 1, ncp, HEAD_DIM), lambda i, j, n: (i, g + j, 0, 0)),
                  pl.BlockSpec((NSA_REP, 1, QT, ncp), lambda i, j, n: (j, n, 0, 0)),
                  pl.BlockSpec((ns, ncp), lambda i, j, n: (0, 0))],
        out_specs=[pl.BlockSpec((1, QT, qw), lambda i, j, n: (i, n, j)),
                   pl.BlockSpec((1, 1, ns, QT), lambda i, j, n: (i, j, 0, n))],
        out_shape=[jax.ShapeDtypeStruct((b, seq, NSA_HEADS * HEAD_DIM), jnp.bfloat16),
                   jax.ShapeDtypeStruct((b, g, ns, seq), jnp.float32)],
        compiler_params=_cparams(("arbitrary", "arbitrary", "arbitrary")),
        name="nsa_cmp_select",
    )(proj3, kcvc, kcvc, cbias, overlap)


def _sel_attn_kernel(q_ref, k_ref, vt_ref, sel_ref, exp_ref, bias_ref, o_ref, madd_ref):
    tq = q_ref.shape[1]
    n = pl.program_id(2)
    scale = HEAD_DIM ** -0.5
    n_bias = bias_ref.shape[1]
    q = q_ref[0].astype(jnp.float32)
    qt = jnp.concatenate([q[:, r * HEAD_DIM:(r + 1) * HEAD_DIM].T for r in range(NSA_REP)],
                         axis=1).astype(jnp.bfloat16)
    km = jnp.dot(exp_ref[...], sel_ref[0, 0].astype(jnp.bfloat16), preferred_element_type=jnp.float32)
    madd_ref[...] = (km - 1.0) * (-NEG)

    last_tile = k_ref.shape[1] // QT - 1

    def tile_start(kt):
        return pl.multiple_of(jnp.minimum(kt, last_tile) * QT, QT)

    def logits(kt):
        start = tile_start(kt)
        s = jnp.dot(k_ref[0, pl.ds(start, QT), :], qt, preferred_element_type=jnp.float32) * scale
        o = n - kt
        ob = jnp.clip(o, 0, n_bias - 1)
        bias = jnp.concatenate([bias_ref[r, ob] for r in range(NSA_REP)], axis=1)
        ma = jnp.where(o >= 0, madd_ref[pl.ds(start, QT), :], NEG)
        return s + bias + jnp.concatenate([ma] * NSA_REP, axis=1)

    def body(kp, carry):
        m, l, acc, s0, s1 = carry
        n0, n1 = logits(2 * kp + 2), logits(2 * kp + 3)
        m_new = jnp.maximum(m, jnp.maximum(jnp.max(s0, axis=0, keepdims=True),
                                           jnp.max(s1, axis=0, keepdims=True)))
        alpha = jnp.exp(m - m_new)
        p0 = jnp.exp(s0 - m_new)
        p1 = jnp.exp(s1 - m_new)
        l = alpha * l + jnp.sum(p0, axis=0, keepdims=True) + jnp.sum(p1, axis=0, keepdims=True)
        acc = (alpha * acc
               + jnp.dot(vt_ref[0, 0, :, pl.ds(tile_start(2 * kp), QT)], p0.astype(jnp.bfloat16),
                         preferred_element_type=jnp.float32)
               + jnp.dot(vt_ref[0, 0, :, pl.ds(tile_start(2 * kp + 1), QT)], p1.astype(jnp.bfloat16),
                         preferred_element_type=jnp.float32))
        return m_new, l, acc, n0, n1

    cols = NSA_REP * tq
    init = (jnp.full((1, cols), NEG, jnp.float32), jnp.zeros((1, cols), jnp.float32),
            jnp.zeros((HEAD_DIM, cols), jnp.float32), logits(0), logits(1))
    m, l, acc, _, _ = lax.fori_loop(0, n // 2 + 1, body, init)
    out = acc / l
    for r in range(NSA_REP):
        o_ref[0, :, r * HEAD_DIM:(r + 1) * HEAD_DIM] = out[:, r * tq:(r + 1) * tq].T.astype(o_ref.dtype)


def _sel_attention(proj3, sel, expand_t, bias_sel_t, lay, seq):
    b = proj3.shape[0]
    g = NSA_KV_GROUPS
    ns = seq // NSA_SEL_LEN
    qw = NSA_REP * HEAD_DIM
    nb = bias_sel_t.shape[1]
    cb_k = lay["kv"] + 2 * g
    cb_v = lay["kv"] + 3 * g
    vt = proj3[:, :, cb_v * LANES:(cb_v + g) * LANES].reshape(b, seq, g, HEAD_DIM).transpose(0, 2, 3, 1)
    return pl.pallas_call(
        _sel_attn_kernel,
        grid=(b, g, seq // QT),
        in_specs=[pl.BlockSpec((1, QT, qw), lambda i, j, n: (i, n, lay["q"] // NSA_REP + j)),
                  pl.BlockSpec((1, seq, HEAD_DIM), lambda i, j, n: (i, 0, cb_k + j)),
                  pl.BlockSpec((1, 1, HEAD_DIM, seq), lambda i, j, n: (i, j, 0, 0)),
                  pl.BlockSpec((1, 1, ns, QT), lambda i, j, n: (i, j, 0, n)),
                  pl.BlockSpec((seq, ns), lambda i, j, n: (0, 0)),
                  pl.BlockSpec((NSA_REP, nb, QT, QT), lambda i, j, n: (j, 0, 0, 0))],
        out_specs=pl.BlockSpec((1, QT, qw), lambda i, j, n: (i, n, j)),
        out_shape=jax.ShapeDtypeStruct((b, seq, NSA_HEADS * HEAD_DIM), jnp.bfloat16),
        scratch_shapes=[pltpu.VMEM((seq, QT), jnp.float32)],
        compiler_params=_cparams(("arbitrary", "arbitrary", "arbitrary")),
        name="nsa_selected",
    )(proj3, proj3, vt, sel, expand_t, bias_sel_t)


def _band_softmax(qs, k_tile, v_tile, bias, n, n_prev, scale):
    logits = []
    for j in range(n_prev + 1):
        kb = n - n_prev + j
        k = k_tile(jnp.maximum(kb, 0))
        s = lax.dot_general(qs, k, (((1,), (1,)), ((), ())), preferred_element_type=jnp.float32) * scale
        bj = jnp.where(kb >= 0, bias[:, j * QT:(j + 1) * QT], NEG)
        logits.append(jnp.where(bj > 0.5 * NEG, s + bj, NEG))
    m = logits[0].max(axis=-1, keepdims=True)
    for s in logits[1:]:
        m = jnp.maximum(m, s.max(axis=-1, keepdims=True))
    l = jnp.zeros_like(m)
    acc = jnp.zeros((qs.shape[0], HEAD_DIM), jnp.float32)
    for j, s in enumerate(logits):
        p = jnp.exp(s - m)
        l = l + jnp.sum(p, axis=-1, keepdims=True)
        acc = acc + jnp.dot(p.astype(jnp.bfloat16), v_tile(jnp.maximum(n - n_prev + j, 0)),
                            preferred_element_type=jnp.float32)
    return acc, m, l


def _win_attn_kernel(q_ref, k_ref, v_ref, bias_ref, o_ref, *, n_prev):
    tq = q_ref.shape[1]
    n = pl.program_id(2)
    qs = _stack_heads(q_ref[0])
    bias = jnp.concatenate([bias_ref[r, 0] for r in range(NSA_REP)], axis=0)
    tile = lambda ref: (lambda kb: ref[0, pl.ds(pl.multiple_of(kb * QT, QT), QT), :])
    acc, _, l = _band_softmax(qs, tile(k_ref), tile(v_ref), bias, n, n_prev, HEAD_DIM ** -0.5)
    o = acc / l
    for r in range(NSA_REP):
        o_ref[0, :, r * HEAD_DIM:(r + 1) * HEAD_DIM] = o[r * tq:(r + 1) * tq].astype(o_ref.dtype)


def _win_attention(proj3, bias_win, lay, seq):
    b = proj3.shape[0]
    g = NSA_KV_GROUPS
    qw = NSA_REP * HEAD_DIM
    n_prev = -(-(NSA_WINDOW - 1) // QT)
    cb_k = lay["kv"] + 4 * g
    cb_v = lay["kv"] + 5 * g
    return pl.pallas_call(
        functools.partial(_win_attn_kernel, n_prev=n_prev),
        grid=(b, g, seq // QT),
        in_specs=[pl.BlockSpec((1, QT, qw), lambda i, j, n: (i, n, lay["q"] // NSA_REP + j)),
                  pl.BlockSpec((1, seq, HEAD_DIM), lambda i, j, n: (i, 0, cb_k + j)),
                  pl.BlockSpec((1, seq, HEAD_DIM), lambda i, j, n: (i, 0, cb_v + j)),
                  pl.BlockSpec((NSA_REP, 1, QT, (n_prev + 1) * QT), lambda i, j, n: (j, 0, 0, 0))],
        out_specs=pl.BlockSpec((1, QT, qw), lambda i, j, n: (i, n, j)),
        out_shape=jax.ShapeDtypeStruct((b, seq, NSA_HEADS * HEAD_DIM), jnp.bfloat16),
        compiler_params=_cparams(("arbitrary", "arbitrary", "arbitrary")),
        name="nsa_window",
    )(proj3, proj3, proj3, bias_win)


DIL_TILES_PER_STEP = 4


def _dil_attn_kernel(q_ref, k_ref, v_ref, bias_ref, o_ref, lse_ref, *, n_prev, dil):
    tiles = q_ref.shape[1] // (QT * dil)
    for sub in range(DIL_TILES_PER_STEP):
        t = pl.program_id(2) * DIL_TILES_PER_STEP + sub
        r = t // tiles
        n = t % tiles
        rows = lambda kb, r=r: pl.ds(kb * (QT * dil) + r, QT, stride=dil) if dil > 1 else pl.ds(
            pl.multiple_of(kb * QT, QT), QT)
        tile = lambda ref, rows=rows: (lambda kb: ref[0, rows(kb), :].astype(jnp.bfloat16))
        acc, m, l = _band_softmax(tile(q_ref)(n), tile(k_ref), tile(v_ref), bias_ref[0, 0], n, n_prev,
                                  HEAD_DIM ** -0.5)
        o_ref[0, rows(n), :] = acc / l
        lse_ref[0, rows(n), :] = jnp.broadcast_to(m + jnp.log(l), (QT, HEAD_DIM))


def _dil_attention(projd3, bias_dil, seq, gi):
    b = projd3.shape[0]
    window, dil = DIL_CONFIGS[gi]
    p_heads = DIL_HEADS_PER_GROUP
    n_prev = -(-(window // dil) // QT)
    cb_q = gi * p_heads
    cb_k = cb_q + DIL_HEADS
    cb_v = cb_k + DIL_HEADS
    out_w = p_heads * HEAD_DIM
    whole = lambda cb: pl.BlockSpec((1, seq, HEAD_DIM), lambda i, p, t: (i, 0, cb + p))
    return pl.pallas_call(
        functools.partial(_dil_attn_kernel, n_prev=n_prev, dil=dil),
        grid=(b, p_heads, seq // (QT * DIL_TILES_PER_STEP)),
        in_specs=[whole(cb_q), whole(cb_k), whole(cb_v),
                  pl.BlockSpec((1, 1, QT, (n_prev + 1) * QT), lambda i, p, t: (p, 0, 0, 0))],
        out_specs=[whole(0), whole(0)],
        out_shape=[jax.ShapeDtypeStruct((b, seq, out_w), jnp.float32),
                   jax.ShapeDtypeStruct((b, seq, out_w), jnp.float32)],
        compiler_params=_cparams(("arbitrary",) * 3),
        name=f"dilated_attn_{gi}",
    )(projd3, projd3, projd3, bias_dil)


def _post_kernel(ocmp_ref, osel_ref, owin_ref, gate_ref, od0_ref, od1_ref, od2_ref, l0_ref, l1_ref, l2_ref,
                 mg1_ref, mg2_ref, x_ref, g1_ref, sc2_ref, sh2_ref, n2_ref, wn_ref, wd_ref, wo_ref,
                 x1_ref, h2_ref):
    gates = _sigmoid(gate_ref[...].astype(jnp.float32))
    parts = []
    for h in range(NSA_HEADS):
        sl = slice(h * HEAD_DIM, (h + 1) * HEAD_DIM)
        y = (gates[:, 3 * h:3 * h + 1] * ocmp_ref[:, sl].astype(jnp.float32)
             + gates[:, 3 * h + 1:3 * h + 2] * osel_ref[:, sl].astype(jnp.float32)
             + gates[:, 3 * h + 2:3 * h + 3] * owin_ref[:, sl].astype(jnp.float32))
        parts.append(y.astype(jnp.bfloat16))
    y_nsa = jnp.concatenate(parts, axis=1)

    l0, l1, l2 = l0_ref[...], l1_ref[...], l2_ref[...]
    mx = jnp.maximum(jnp.maximum(l0, l1), l2)
    e0, e1, e2 = jnp.exp(l0 - mx), jnp.exp(l1 - mx), jnp.exp(l2 - mx)
    den = e0 + e1 + e2
    y_dil = ((e0 / den) * od0_ref[...].astype(jnp.float32) + (e1 / den) * od1_ref[...].astype(jnp.float32)
             + (e2 / den) * od2_ref[...].astype(jnp.float32)).astype(jnp.bfloat16)

    a = jnp.dot(y_nsa, wn_ref[...], preferred_element_type=jnp.float32)
    bb = jnp.dot(y_dil, wd_ref[...], preferred_element_type=jnp.float32)
    merged = (_sigmoid(mg1_ref[...].astype(jnp.float32)) * a
              + _sigmoid(mg2_ref[...].astype(jnp.float32)) * bb).astype(jnp.bfloat16)
    mix = jnp.dot(merged, wo_ref[...], preferred_element_type=jnp.float32)
    x1 = x_ref[...] + g1_ref[0] * mix
    x1_ref[...] = x1
    y = x1 * lax.rsqrt(jnp.mean(x1 * x1, axis=-1, keepdims=True) + EPS) * n2_ref[...]
    h2_ref[...] = y * (1.0 + sc2_ref[0]) + sh2_ref[0]


def _post(ocmp, osel, owin, proj2d, odil, lsedil, x2d, g1, sc2, sh2, norm2_g, wn, wd, wo, lay, seq):
    t, d = x2d.shape
    tm = 256
    per_batch = seq // tm
    nw = NSA_HEADS * HEAD_DIM
    dw = DIL_HEADS_PER_GROUP * HEAD_DIM
    row = lambda w: pl.BlockSpec((tm, w), lambda i: (i, 0))
    mod = pl.BlockSpec((1, 1, d), lambda i: (i // per_batch, 0, 0))
    full = lambda a: pl.BlockSpec(a.shape, lambda i: (0, 0))
    return pl.pallas_call(
        _post_kernel,
        grid=(t // tm,),
        in_specs=[row(nw), row(nw), row(nw),
                  pl.BlockSpec((tm, LANES), lambda i: (i, lay["gate"])),
                  row(dw), row(dw), row(dw), row(dw), row(dw), row(dw),
                  pl.BlockSpec((tm, d), lambda i: (i, 0)),
                  pl.BlockSpec((tm, d), lambda i: (i, 1)),
                  row(d), mod, mod, mod,
                  pl.BlockSpec((1, d), lambda i: (0, 0)),
                  full(wn), full(wd), full(wo)],
        out_specs=[row(d), row(d)],
        out_shape=[jax.ShapeDtypeStruct((t, d), jnp.float32), jax.ShapeDtypeStruct((t, d), jnp.float32)],
        compiler_params=_cparams(("arbitrary",)),
        name="merge_out_proj",
    )(ocmp, osel, owin, proj2d, odil[0], odil[1], odil[2], lsedil[0], lsedil[1], lsedil[2],
      proj2d, proj2d, x2d, g1, sc2, sh2, norm2_g.reshape(1, d), wn, wd, wo)


def _topk_rows(sc, k, payload=None):
    n = sc.shape[0]
    row = lax.broadcasted_iota(jnp.int32, sc.shape, 0).astype(jnp.float32)
    vals, picks = [], []
    for _ in range(k):
        mx = jnp.max(sc, axis=0, keepdims=True)
        first = jnp.min(jnp.where(sc == mx, row, float(n)), axis=0, keepdims=True)
        hit = row == first
        vals.append(mx)
        if payload is None:
            picks.append(first)
        else:
            picks.append(jnp.sum(jnp.where(hit, payload, 0.0), axis=0, keepdims=True))
        sc = jnp.where(hit, -jnp.inf, sc)
    return jnp.concatenate(vals, axis=0), jnp.concatenate(picks, axis=0)


def _route_kernel(h_ref, wq_ref, gain_ref, keys_ref, idx_ref, gate_ref):
    q = jnp.dot(h_ref[...].astype(jnp.bfloat16), wq_ref[...], preferred_element_type=jnp.float32)
    k = PEER_TOPK
    for head in range(PEER_HEADS):
        tops = []
        for part in range(2):
            hp = head * 2 + part
            a = q[:, hp * PEER_HALF:(hp + 1) * PEER_HALF]
            qn = (a * lax.rsqrt(jnp.mean(a * a, axis=-1, keepdims=True) + EPS)
                  * gain_ref[hp:hp + 1, :]).astype(jnp.bfloat16)
            sc = lax.dot_general(keys_ref[hp], qn, (((1,), (1,)), ((), ())),
                                 preferred_element_type=jnp.float32)
            tops.append(_topk_rows(sc, k))
        (s1, i1), (s2, i2) = tops
        half = k // 2
        sub = lax.broadcasted_iota(jnp.int32, (half, s1.shape[1]), 0)
        vals = [s1[0:1] + s2]
        ids = [i1[0:1] * float(PEER_N_KEYS) + i2]
        for a in range(1, half):
            vals.append(jnp.where(sub < k // (a + 1), s1[a:a + 1] + s2[:half], -jnp.inf))
            ids.append(i1[a:a + 1] * float(PEER_N_KEYS) + i2[:half])
        vals.append(s1[half:] + s2[0:1])
        ids.append(i1[half:] * float(PEER_N_KEYS) + i2[0:1])
        top_s, expert = _topk_rows(jnp.concatenate(vals, axis=0), k, payload=jnp.concatenate(ids, axis=0))
        e = jnp.exp(top_s - top_s[0:1])
        gate = e / jnp.sum(e, axis=0, keepdims=True)
        idx_ref[head * k:(head + 1) * k, :] = expert.astype(jnp.int32)
        gate_ref[head * k:(head + 1) * k, :] = gate


def _route(h2, wq_bf16, q_gain, sub_keys_bf16):
    t, d = h2.shape
    tm = LANES
    hk = PEER_HEADS * PEER_TOPK
    nq = wq_bf16.shape[1]
    return pl.pallas_call(
        _route_kernel,
        grid=(t // tm,),
        in_specs=[pl.BlockSpec((tm, d), lambda i: (i, 0)),
                  pl.BlockSpec((d, nq), lambda i: (0, 0)),
                  pl.BlockSpec((2 * PEER_HEADS, PEER_HALF), lambda i: (0, 0)),
                  pl.BlockSpec((2 * PEER_HEADS, PEER_N_KEYS, PEER_HALF), lambda i: (0, 0, 0))],
        out_specs=[pl.BlockSpec((hk, tm), lambda i: (0, i)),
                   pl.BlockSpec((hk, tm), lambda i: (0, i))],
        out_shape=[jax.ShapeDtypeStruct((hk, t), jnp.int32), jax.ShapeDtypeStruct((hk, t), jnp.float32)],
        compiler_params=_cparams(("arbitrary",)),
        name="peer_route",
    )(h2, wq_bf16, q_gain.reshape(2 * PEER_HEADS, PEER_HALF), sub_keys_bf16)


PEER_TB = 32
SUBLANES = 8


PEER_STAGES = 4
PEER_LOOKAHEAD = 2


def _expert_kernel(idx_ref, nidx_ref, h_ref, gate_ref, x1_ref, g2_ref, tab_ref, o_ref, *scratch):
    stages, sem = scratch[:PEER_STAGES], scratch[PEER_STAGES]
    tb, d = h_ref.shape
    k = PEER_TOPK
    hk = PEER_HEADS * k
    nr, nc = tb // SUBLANES, d // LANES
    step = pl.program_id(0)
    hi_mask = jnp.uint32(0xFFFF0000)

    def start_rows(ids, head, j, r):
        for s in range(SUBLANES):
            e = ids[0, 0, (r * SUBLANES + s) * hk + head * k + j]
            pltpu.make_async_copy(tab_ref.at[pl.ds(e, 1), :],
                                  stages[head % PEER_STAGES].at[pl.ds((j * nr + r) * SUBLANES + s, 1), :],
                                  sem.at[head % PEER_STAGES]).start(priority=s % 2)

    def words(stage, j, r, c):
        row0 = (j * nr + r) * SUBLANES
        return stage[row0:row0 + SUBLANES, c * LANES:(c + 1) * LANES]

    def start_head(ids, head):
        for j in range(k):
            for r in range(nr):
                start_rows(ids, head, j, r)

    def wait_head(head):
        buf = stages[head % PEER_STAGES]
        pltpu.make_async_copy(buf, buf, sem.at[head % PEER_STAGES]).wait()

    @pl.when(step == 0)
    def _():
        for head in range(PEER_LOOKAHEAD):
            start_head(idx_ref, head)

    lane = lax.broadcasted_iota(jnp.int32, (SUBLANES, LANES), 1)
    for head in range(PEER_HEADS):
        stage = stages[head % PEER_STAGES]
        wait_head(head)
        ahead = head + PEER_LOOKAHEAD
        ahead_ids, ahead_head = (idx_ref, ahead) if ahead < PEER_HEADS else (nidx_ref, ahead - PEER_HEADS)
        acts = []
        for r in range(nr):
            rows = slice(r * SUBLANES, (r + 1) * SUBLANES)
            pre = jnp.zeros((SUBLANES, LANES), jnp.float32)
            for j in range(k):
                start_rows(ahead_ids, ahead_head, j, r)
                acc = None
                for c in range(nc):
                    u = lax.bitcast_convert_type(words(stage, j, r, c) & hi_mask, jnp.float32)
                    prod = h_ref[rows, c * LANES:(c + 1) * LANES] * u
                    acc = prod if acc is None else acc + prod
                pre = jnp.where(lane == head * k + j, jnp.sum(acc, axis=-1, keepdims=True), pre)
            acts.append(gate_ref[rows, :] * _gelu(pre))
        for r in range(nr):
            rows = slice(r * SUBLANES, (r + 1) * SUBLANES)
            accs = [None] * nc
            for j in range(k):
                a = jnp.broadcast_to(acts[r][:, head * k + j:head * k + j + 1], (SUBLANES, LANES))
                for c in range(nc):
                    term = a * lax.bitcast_convert_type(words(stage, j, r, c) << 16, jnp.float32)
                    accs[c] = term if accs[c] is None else accs[c] + term
            for c in range(nc):
                cs = slice(c * LANES, (c + 1) * LANES)
                if head == 0:
                    o_ref[rows, cs] = accs[c]
                else:
                    o_ref[rows, cs] = o_ref[rows, cs] + accs[c]

    @pl.when(step + 1 == pl.num_programs(0))
    def _():
        for head in range(PEER_LOOKAHEAD):
            wait_head(head)
    o_ref[...] = x1_ref[...] + g2_ref[0] * o_ref[...]


def _experts(idx_tok, gate_tok, h2, x1, g2, table, seq):
    t, d = h2.shape
    tb = PEER_TB
    hk = PEER_HEADS * PEER_TOPK
    per_batch = seq // tb
    nsteps = t // tb
    ids = idx_tok.reshape(nsteps, 1, tb * hk)
    stage = pltpu.VMEM((PEER_TOPK * tb, d), jnp.uint32)
    return pl.pallas_call(
        _expert_kernel,
        grid=(nsteps,),
        in_specs=[pl.BlockSpec((1, 1, tb * hk), lambda i: (i, 0, 0), memory_space=pltpu.SMEM),
                  pl.BlockSpec((1, 1, tb * hk), lambda i: (jnp.minimum(i + 1, nsteps - 1), 0, 0),
                               memory_space=pltpu.SMEM),
                  pl.BlockSpec((tb, d), lambda i: (i, 0)),
                  pl.BlockSpec((tb, hk), lambda i: (i, 0)),
                  pl.BlockSpec((tb, d), lambda i: (i, 0)),
                  pl.BlockSpec((1, 1, d), lambda i: (i // per_batch, 0, 0)),
                  pl.BlockSpec(memory_space=pl.ANY)],
        out_specs=pl.BlockSpec((tb, d), lambda i: (i, 0)),
        out_shape=jax.ShapeDtypeStruct((t, d), jnp.float32),
        scratch_shapes=[stage] * PEER_STAGES + [pltpu.SemaphoreType.DMA((PEER_STAGES,))],
        compiler_params=_cparams(("arbitrary",)),
        name="peer_experts",
    )(ids, ids, h2, gate_tok, x1, g2, table)


def _pack_expert_table(u, v):
    ub = lax.bitcast_convert_type(u.astype(jnp.bfloat16), jnp.uint16).astype(jnp.uint32)
    vb = lax.bitcast_convert_type(v.astype(jnp.bfloat16), jnp.uint16).astype(jnp.uint32)
    return (ub << 16) | vb


def _proj_weights(w_in, nsa_q_gain, nsa_k_gain, dil_q_gain, dil_k_gain, d_model, lay):
    dh = HEAD_DIM
    off_kv = NSA_HEADS * dh
    off_gate = off_kv + 6 * NSA_KV_GROUPS * dh
    off_dil = off_gate + 3 * NSA_HEADS
    off_merge = off_dil + 3 * DIL_HEADS * dh
    n_cols = lay["n"] * LANES
    pad = n_cols - (lay["gate"] * LANES + 3 * NSA_HEADS)
    w = jnp.concatenate([w_in[:, off_merge:], w_in[:, :off_kv], w_in[:, off_kv:off_gate],
                         w_in[:, off_gate:off_dil], jnp.zeros((d_model, pad), w_in.dtype)],
                        axis=1).astype(jnp.bfloat16)
    w_dil = w_in[:, off_dil:off_merge].astype(jnp.bfloat16)
    ones = jnp.ones((dh,), jnp.float32)

    def rows(spec):
        gains = jnp.concatenate([g for g, _, n in spec for _ in range(n)])
        flags = jnp.concatenate([jnp.full((dh,), f, jnp.float32) for _, f, n in spec for _ in range(n)])
        return flags.reshape(1, -1), gains.reshape(1, -1)

    g = NSA_KV_GROUPS
    main = [(ones, 0.0, lay["q"]), (nsa_q_gain, 1.0, NSA_HEADS), (ones, 0.0, 2 * g),
            (nsa_k_gain[1], 1.0, g), (ones, 0.0, g), (nsa_k_gain[2], 1.0, g), (ones, 0.0, g),
            (ones, 0.0, lay["n"] - lay["gate"])]
    dil = [(dil_q_gain, 1.0, DIL_HEADS), (dil_k_gain, 1.0, DIL_HEADS), (ones, 0.0, DIL_HEADS)]
    return (w,) + rows(main), (w_dil,) + rows(dil)


def _token_mixer_and_norm2(x, mod, norm1_g, w_in, nsa_q_gain, nsa_k_gain, cmp_pos, cmp_w1, cmp_w2,
                           dil_q_gain, dil_k_gain, w_br_nsa, w_br_dil, w_out, norm2_g, rel_bias):
    b, seq, d = x.shape
    t = b * seq
    lay = _layout(d)
    sh1, sc1, g1, sh2, sc2, g2 = [m.reshape(b, 1, d) for m in jnp.split(mod, 6, axis=-1)]
    x2d = x.reshape(t, d)

    main_w, dil_w = _proj_weights(w_in, nsa_q_gain, nsa_k_gain, dil_q_gain, dil_k_gain, d, lay)
    blocks_per_tile = 4
    proj2d = _projection(x2d, norm1_g, sc1, sh1, *main_w, seq, jnp.bfloat16, (0, lay["q"] // blocks_per_tile))
    proj3 = proj2d.reshape(b, seq, lay["n"] * LANES)
    projd3 = _projection(x2d, norm1_g, sc1, sh1, *dil_w, seq, jnp.float32,
                         (2 * DIL_HEADS // blocks_per_tile, 3 * DIL_HEADS // blocks_per_tile)).reshape(b, seq, -1)

    rel_flat = rel_bias.reshape(-1)
    nchunk = seq // NSA_CMP_STRIDE
    ns = seq // NSA_SEL_LEN
    bias_win = _bias_table(rel_flat, _band_buckets(-(-(NSA_WINDOW - 1) // QT), NSA_WINDOW - 1, 1), 0, NSA_HEADS)
    bias_sel = _bias_table(rel_flat, _causal_buckets(_n_causal_tiles(seq)), 0, NSA_HEADS)
    bias_cmp = _bias_table(rel_flat, _cmp_buckets(seq, nchunk), 0, NSA_HEADS)
    bias_dil = [_bias_table(rel_flat, _band_buckets(-(-(wnd // dil) // QT), wnd // dil, dil),
                            NSA_HEADS + gi * DIL_HEADS_PER_GROUP, DIL_HEADS_PER_GROUP)
                for gi, (wnd, dil) in enumerate(DIL_CONFIGS)]

    g = NSA_KV_GROUPS
    c0 = lay["kv"] * LANES
    tc = proj3[:, :, c0:c0 + 2 * g * HEAD_DIM].reshape(b, nchunk, NSA_CMP_STRIDE, 2 * g, HEAD_DIM)
    tc = tc.transpose(0, 3, 1, 2, 4).reshape(b, 2 * g, nchunk, NSA_CMP_STRIDE * HEAD_DIM)
    kcvc = _compress(tc, cmp_pos, cmp_w1, cmp_w2, nsa_k_gain[0])

    cstart = np.arange(nchunk) * NSA_CMP_STRIDE
    sstart = np.arange(ns) * NSA_SEL_LEN
    n_cmp = (seq - NSA_CMP_LEN) // NSA_CMP_STRIDE + 1
    overlap = ((cstart[:, None] < sstart[None, :] + NSA_SEL_LEN) & (cstart[:, None] + NSA_CMP_LEN > sstart[None, :])
               & (np.arange(nchunk)[:, None] < n_cmp)).astype(np.float32)
    expand_t = (np.arange(seq)[:, None] // NSA_SEL_LEN == np.arange(ns)[None, :]).astype(np.float32)

    ocmp, sel = _cmp_select(proj3, kcvc, bias_cmp, jnp.asarray(overlap.T), lay, seq)
    osel = _sel_attention(proj3, sel, jnp.asarray(expand_t, jnp.bfloat16), bias_sel, lay, seq)
    owin = _win_attention(proj3, bias_win, lay, seq)
    odil, lsedil = zip(*[_dil_attention(projd3, bias_dil[gi], seq, gi) for gi in range(len(DIL_CONFIGS))])

    x1, h2 = _post(ocmp.reshape(t, -1), osel.reshape(t, -1), owin.reshape(t, -1), proj2d,
                   [o.reshape(t, -1) for o in odil], [l.reshape(t, -1) for l in lsedil],
                   x2d, g1, sc2, sh2, norm2_g, w_br_nsa.astype(jnp.bfloat16), w_br_dil.astype(jnp.bfloat16),
                   w_out.astype(jnp.bfloat16), lay, seq)
    return x1, h2, g2


def kernel(x, c, w_ada, b_ada, norm1_g, w_in, nsa_q_gain, nsa_k_gain, cmp_pos, cmp_w1, cmp_w2,
           dil_q_gain, dil_k_gain, w_br_nsa, w_br_dil, w_out, norm2_g, peer_w_q, peer_q_gain,
           peer_sub_keys, peer_u, peer_v, rel_bias):
    b, seq, d = x.shape
    depth = w_ada.shape[0]
    for layer in range(depth):
        mod = _modulation(c, w_ada[layer], b_ada[layer])
        x1, h2, g2 = _token_mixer_and_norm2(
            x, mod, norm1_g[layer], w_in[layer], nsa_q_gain[layer], nsa_k_gain[layer], cmp_pos[layer],
            cmp_w1[layer], cmp_w2[layer], dil_q_gain[layer], dil_k_gain[layer], w_br_nsa[layer],
            w_br_dil[layer], w_out[layer], norm2_g[layer], rel_bias)
        idx, gate = _route(h2, peer_w_q[layer].astype(jnp.bfloat16), peer_q_gain[layer],
                           peer_sub_keys[layer].reshape(2 * PEER_HEADS, PEER_N_KEYS, PEER_HALF).astype(jnp.bfloat16))
        table = _pack_expert_table(peer_u[layer], peer_v[layer])
        out = _experts(idx.T, gate.T, h2, x1, g2, table, seq)
        x = out.reshape(b, seq, d)
    return x
```

```python
import functools
import math

import numpy as np
import jax
import jax.numpy as jnp
from jax import lax
from jax.experimental import pallas as pl
from jax.experimental.pallas import tpu as pltpu

HEAD_DIM = 128
LANES = 128
NSA_HEADS = 8
NSA_KV_GROUPS = 2
NSA_REP = NSA_HEADS // NSA_KV_GROUPS
NSA_CMP_LEN = 32
NSA_CMP_STRIDE = 16
NSA_SEL_LEN = 64
NSA_TOP_N = 16
NSA_WINDOW = 512
DIL_CONFIGS = ((128, 1), (512, 4), (2048, 16))
DIL_HEADS_PER_GROUP = 4
DIL_HEADS = DIL_HEADS_PER_GROUP * len(DIL_CONFIGS)
REL_BUCKETS = 32
REL_MAX_DIST = 2048
PEER_HEADS = 8
PEER_N_KEYS = 128
PEER_TOPK = 16
PEER_HALF = 128
EPS = 1e-6
NEG = -1e30
SEL_FORCE = 1e4
MASKED_BUCKET = REL_BUCKETS

QT = 128
VMEM_LIMIT = 48 * 1024 * 1024

CB_MERGE = 0
def _layout(d_model):
    nmerge = 2 * d_model // LANES
    cb_q = nmerge
    cb_kv = cb_q + NSA_HEADS
    cb_gate = cb_kv + 6 * NSA_KV_GROUPS
    n_used = cb_gate + 1
    n_blocks = -(-n_used // 4) * 4
    return dict(q=cb_q, kv=cb_kv, gate=cb_gate, n=n_blocks)


def _cparams(sem):
    return pltpu.CompilerParams(dimension_semantics=sem, vmem_limit_bytes=VMEM_LIMIT)


def _t5_bucket_np(dist):
    n = np.maximum(dist, 0)
    max_exact = REL_BUCKETS // 2
    nf = np.maximum(n, max_exact).astype(np.float32)
    large = max_exact + (np.log(nf / np.float32(max_exact)) / np.float32(math.log(REL_MAX_DIST / max_exact))
                         * np.float32(REL_BUCKETS - max_exact)).astype(np.int32)
    large = np.minimum(large, REL_BUCKETS - 1)
    return np.where(n < max_exact, n, large).astype(np.int32)


def _gelu(x):
    return jax.nn.gelu(x)


def _sigmoid(x):
    return 1.0 / (1.0 + jnp.exp(-x))


def _mod_kernel(c_ref, w_ref, b_ref, o_ref):
    c = c_ref[...]
    cond = c * _sigmoid(c)
    o_ref[...] = jnp.dot(cond.astype(jnp.bfloat16), w_ref[...].astype(jnp.bfloat16),
                         preferred_element_type=jnp.float32) + b_ref[...]


def _modulation(c, w_ada, b_ada):
    b, d = c.shape
    n = w_ada.shape[1]
    rows = 8
    cp = jnp.zeros((rows, d), jnp.float32).at[:b].set(c)
    tn = 512
    out = pl.pallas_call(
        _mod_kernel,
        grid=(n // tn,),
        in_specs=[pl.BlockSpec((rows, d), lambda j: (0, 0)),
                  pl.BlockSpec((d, tn), lambda j: (0, j)),
                  pl.BlockSpec((1, tn), lambda j: (0, j))],
        out_specs=pl.BlockSpec((rows, tn), lambda j: (0, j)),
        out_shape=jax.ShapeDtypeStruct((rows, n), jnp.float32),
        compiler_params=_cparams(("arbitrary",)),
        name="adaln_mod",
    )(cp, w_ada, b_ada.reshape(1, n))
    return out[:b]


def _proj_kernel(x_ref, g_ref, sc_ref, sh_ref, w_ref, flag_ref, gain_ref, o_ref, h_ref, *, plain):
    j = pl.program_id(1)

    @pl.when(j == 0)
    def _():
        x = x_ref[...]
        y = x * lax.rsqrt(jnp.mean(x * x, axis=-1, keepdims=True) + EPS) * g_ref[...]
        h_ref[...] = (y * (1.0 + sc_ref[0]) + sh_ref[0]).astype(jnp.bfloat16)

    acc = jnp.dot(h_ref[...], w_ref[...], preferred_element_type=jnp.float32)
    is_plain = (j >= plain[0]) & (j < plain[1])

    @pl.when(is_plain)
    def _():
        o_ref[...] = acc.astype(o_ref.dtype)

    @pl.when(jnp.logical_not(is_plain))
    def _():
        for k in range(acc.shape[1] // LANES):
            a = acc[:, k * LANES:(k + 1) * LANES]
            f = flag_ref[:, k * LANES:(k + 1) * LANES]
            r = lax.rsqrt(jnp.mean(a * a, axis=-1, keepdims=True) + EPS)
            scale = f * r + (1.0 - f)
            o_ref[:, k * LANES:(k + 1) * LANES] = (a * scale * gain_ref[:, k * LANES:(k + 1) * LANES]).astype(o_ref.dtype)


def _projection(x2d, norm_g, sc, sh, w_bf16, flags, gains, seq, out_dtype, plain):
    t, d = x2d.shape
    n = w_bf16.shape[1]
    tm = min(1024, seq)
    tn = 512
    per_batch = seq // tm
    return pl.pallas_call(
        functools.partial(_proj_kernel, plain=plain),
        grid=(t // tm, n // tn),
        in_specs=[pl.BlockSpec((tm, d), lambda i, j: (i, 0)),
                  pl.BlockSpec((1, d), lambda i, j: (0, 0)),
                  pl.BlockSpec((1, 1, d), lambda i, j: (i // per_batch, 0, 0)),
                  pl.BlockSpec((1, 1, d), lambda i, j: (i // per_batch, 0, 0)),
                  pl.BlockSpec((d, tn), lambda i, j: (0, j)),
                  pl.BlockSpec((1, tn), lambda i, j: (0, j)),
                  pl.BlockSpec((1, tn), lambda i, j: (0, j))],
        out_specs=pl.BlockSpec((tm, tn), lambda i, j: (i, j)),
        out_shape=jax.ShapeDtypeStruct((t, n), out_dtype),
        scratch_shapes=[pltpu.VMEM((tm, d), jnp.bfloat16)],
        compiler_params=_cparams(("arbitrary", "arbitrary")),
        name="norm_in_proj",
    )(x2d, norm_g.reshape(1, d), sc, sh, w_bf16, flags, gains)


def _bias_kernel(tbl_ref, idx_ref, o_ref, *, head0):
    h = pl.program_id(0) + head0
    b = idx_ref[0]
    out = jnp.full(b.shape, NEG, jnp.float32)
    for k in range(REL_BUCKETS):
        out = jnp.where(b == k, tbl_ref[h * REL_BUCKETS + k], out)
    o_ref[0, 0] = out


def _bias_table(rel_bias_flat, bucket_idx, head0, n_heads):
    nt, r, c = bucket_idx.shape
    return pl.pallas_call(
        functools.partial(_bias_kernel, head0=head0),
        grid=(n_heads, nt),
        in_specs=[pl.BlockSpec(memory_space=pltpu.SMEM),
                  pl.BlockSpec((1, r, c), lambda h, t: (t, 0, 0))],
        out_specs=pl.BlockSpec((1, 1, r, c), lambda h, t: (h, t, 0, 0)),
        out_shape=jax.ShapeDtypeStruct((n_heads, nt, r, c), jnp.float32),
        compiler_params=_cparams(("arbitrary", "arbitrary")),
        name="rel_bias_table",
    )(rel_bias_flat, jnp.asarray(bucket_idx))


def _band_buckets(n_prev, span, dist_scale):
    i = np.arange(QT)[:, None]
    c = np.arange((n_prev + 1) * QT)[None, :]
    dist = n_prev * QT + i - c
    ok = (dist >= 0) & (dist <= span)
    return np.where(ok, _t5_bucket_np(dist * dist_scale), MASKED_BUCKET).astype(np.int32)[None]


def _causal_buckets(n_tiles):
    i = np.arange(QT)[None, :]
    j = np.arange(QT)[:, None]
    out = []
    for o in range(n_tiles):
        dist = QT * o + i - j
        out.append(np.where(dist >= 0, _t5_bucket_np(dist), MASKED_BUCKET))
    return np.stack(out).astype(np.int32)


def _n_causal_tiles(seq):
    first_last = int(np.argmax(_t5_bucket_np(np.arange(4 * REL_MAX_DIST)) == REL_BUCKETS - 1))
    o_const = -(-(first_last + QT - 1) // QT)
    return min(seq // QT, o_const + 1)


def _cmp_buckets(seq, n_cmp_pad):
    nq = seq // QT
    n_cmp = (seq - NSA_CMP_LEN) // NSA_CMP_STRIDE + 1
    i = np.arange(QT)[None, :, None]
    n = np.arange(nq)[:, None, None]
    c = np.arange(n_cmp_pad)[None, None, :]
    dist = n * QT + i - (c * NSA_CMP_STRIDE + NSA_CMP_LEN - 1)
    ok = (dist >= 0) & (c < n_cmp)
    return np.where(ok, _t5_bucket_np(dist), MASKED_BUCKET).astype(np.int32)


def _compress_kernel(t_ref, pos_ref, w1_ref, w2_ref, gain_ref, o_ref):
    half = t_ref.shape[3]
    a = t_ref[0, 0].astype(jnp.float32)
    lo = (a + pos_ref[0, :, :half]).astype(jnp.bfloat16)
    hi = (a + pos_ref[0, :, half:]).astype(jnp.bfloat16)
    p = jnp.dot(lo, w1_ref[0, :half, :], preferred_element_type=jnp.float32)
    q = jnp.dot(hi, w1_ref[0, half:, :], preferred_element_type=jnp.float32)
    n = q.shape[0]
    q_next = jnp.concatenate([q[1:], q[:1]], axis=0)
    hid = _gelu(p + q_next)
    out = jnp.dot(hid.astype(jnp.bfloat16), w2_ref[0], preferred_element_type=jnp.float32)
    is_key = pl.program_id(1) < NSA_KV_GROUPS
    r = lax.rsqrt(jnp.mean(out * out, axis=-1, keepdims=True) + EPS)
    normed = out * r * gain_ref[...]
    o_ref[0, 0] = jnp.where(is_key, normed, out).astype(o_ref.dtype)


def _compress(t_chunks, cmp_pos, cmp_w1, cmp_w2, k_gain0):
    b, nh, nchunk, width = t_chunks.shape
    dh = HEAD_DIM
    g = NSA_KV_GROUPS
    return pl.pallas_call(
        _compress_kernel,
        grid=(b, nh),
        in_specs=[pl.BlockSpec((1, 1, nchunk, width), lambda i, h: (i, h, 0, 0)),
                  pl.BlockSpec((1, 1, 2 * width), lambda i, h: (h // g, 0, 0)),
                  pl.BlockSpec((1, 2 * width, dh), lambda i, h: (h // g, 0, 0)),
                  pl.BlockSpec((1, dh, dh), lambda i, h: (h // g, 0, 0)),
                  pl.BlockSpec((1, dh), lambda i, h: (0, 0))],
        out_specs=pl.BlockSpec((1, 1, nchunk, dh), lambda i, h: (i, h, 0, 0)),
        out_shape=jax.ShapeDtypeStruct((b, nh, nchunk, dh), jnp.bfloat16),
        compiler_params=_cparams(("arbitrary", "arbitrary")),
        name="nsa_compress",
    )(t_chunks, cmp_pos.reshape(2, 1, 2 * width), cmp_w1.astype(jnp.bfloat16),
      cmp_w2.astype(jnp.bfloat16), k_gain0.reshape(1, dh))


def _stack_heads(q):
    return jnp.concatenate([q[:, r * HEAD_DIM:(r + 1) * HEAD_DIM] for r in range(NSA_REP)], axis=0)


def _cmp_select_kernel(q_ref, kc_ref, vc_ref, bias_ref, ov_ref, o_ref, sel_ref, *, n_sel):
    tq = q_ref.shape[1]
    n = pl.program_id(2)
    scale = HEAD_DIM ** -0.5
    qs = _stack_heads(q_ref[0])
    s = lax.dot_general(qs, kc_ref[0, 0], (((1,), (1,)), ((), ())),
                        preferred_element_type=jnp.float32) * scale
    bias = jnp.concatenate([bias_ref[r, 0] for r in range(NSA_REP)], axis=0)
    valid = bias > 0.5 * NEG
    s = jnp.where(valid, s + bias, NEG)
    m = jnp.max(s, axis=-1, keepdims=True)
    e = jnp.exp(s - m)
    p = e / jnp.sum(e, axis=-1, keepdims=True) * valid.astype(jnp.float32)
    o = jnp.dot(p.astype(jnp.bfloat16), vc_ref[0, 0], preferred_element_type=jnp.float32)
    for r in range(NSA_REP):
        o_ref[0, :, r * HEAD_DIM:(r + 1) * HEAD_DIM] = o[r * tq:(r + 1) * tq].astype(o_ref.dtype)

    psum = p[0:tq]
    for r in range(1, NSA_REP):
        psum = psum + p[r * tq:(r + 1) * tq]
    imp = lax.dot_general(ov_ref[...], psum, (((1,), (1,)), ((), ())), preferred_element_type=jnp.float32,
                          precision=lax.Precision.HIGHEST)
    n_blocks = sel_ref.shape[2]
    jb = lax.broadcasted_iota(jnp.int32, (n_blocks, tq), 0).astype(jnp.float32)
    qpos = lax.broadcasted_iota(jnp.int32, (n_blocks, tq), 1)
    cur = ((n * tq + qpos) // NSA_SEL_LEN).astype(jnp.float32)
    forced = (jb == 0.0) | (jb == cur) | (jb == cur - 1.0)
    score = jnp.where(forced, SEL_FORCE, jnp.where(jb > cur, -SEL_FORCE, imp))
    chosen = jnp.zeros((n_blocks, tq), jnp.float32)
    for _ in range(n_sel):
        mx = jnp.max(score, axis=0, keepdims=True)
        first = jnp.min(jnp.where(score == mx, jb, float(n_blocks)), axis=0, keepdims=True)
        hit = jb == first
        chosen = jnp.where(hit, 1.0, chosen)
        score = jnp.where(hit, -jnp.inf, score)
    sel_ref[0, 0] = chosen


def _cmp_select(proj3, kcvc, cbias, overlap, lay, seq):
    b = proj3.shape[0]
    g = NSA_KV_GROUPS
    ncp = kcvc.shape[2]
    ns = seq // NSA_SEL_LEN
    n_sel = min(NSA_TOP_N, ns)
    qw = NSA_REP * HEAD_DIM
    return pl.pallas_call(
        functools.partial(_cmp_select_kernel, n_sel=n_sel),
        grid=(b, g, seq // QT),
        in_specs=[pl.BlockSpec((1, QT, qw), lambda i, j, n: (i, n, lay["q"] // NSA_REP + j)),
                  pl.BlockSpec((1, 1, ncp, HEAD_DIM), lambda i, j, n: (i, j, 0, 0)),
                  pl.BlockSpec((1, 1, ncp, HEAD_DIM), lambda i, j, n: (i, g + j, 0, 0)),
                  pl.BlockSpec((NSA_REP, 1, QT, ncp), lambda i, j, n: (j, n, 0, 0)),
                  pl.BlockSpec((ns, ncp), lambda i, j, n: (0, 0))],
        out_specs=[pl.BlockSpec((1, QT, qw), lambda i, j, n: (i, n, j)),
                   pl.BlockSpec((1, 1, ns, QT), lambda i, j, n: (i, j, 0, n))],
        out_shape=[jax.ShapeDtypeStruct((b, seq, NSA_HEADS * HEAD_DIM), jnp.bfloat16),
                   jax.ShapeDtypeStruct((b, g, ns, seq), jnp.float32)],
        compiler_params=_cparams(("arbitrary", "arbitrary", "arbitrary")),
        name="nsa_cmp_select",
    )(proj3, kcvc, kcvc, cbias, overlap)


def _sel_attn_kernel(q_ref, k_ref, vt_ref, sel_ref, exp_ref, bias_ref, o_ref, madd_ref):
    tq = q_ref.shape[1]
    n = pl.program_id(2)
    scale = HEAD_DIM ** -0.5
    n_bias = bias_ref.shape[1]
    q = q_ref[0].astype(jnp.float32)
    qt = jnp.concatenate([q[:, r * HEAD_DIM:(r + 1) * HEAD_DIM].T for r in range(NSA_REP)],
                         axis=1).astype(jnp.bfloat16)
    km = jnp.dot(exp_ref[...], sel_ref[0, 0].astype(jnp.bfloat16), preferred_element_type=jnp.float32)
    madd_ref[...] = (km - 1.0) * (-NEG)

    last_tile = k_ref.shape[1] // QT - 1

    def tile_start(kt):
        return pl.multiple_of(jnp.minimum(kt, last_tile) * QT, QT)

    def logits(kt):
        start = tile_start(kt)
        s = jnp.dot(k_ref[0, pl.ds(start, QT), :], qt, preferred_element_type=jnp.float32) * scale
        o = n - kt
        ob = jnp.clip(o, 0, n_bias - 1)
        bias = jnp.concatenate([bias_ref[r, ob] for r in range(NSA_REP)], axis=1)
        ma = jnp.where(o >= 0, madd_ref[pl.ds(start, QT), :], NEG)
        return s + bias + jnp.concatenate([ma] * NSA_REP, axis=1)

    def body(kp, carry):
        m, l, acc, s0, s1 = carry
        n0, n1 = logits(2 * kp + 2), logits(2 * kp + 3)
        m_new = jnp.maximum(m, jnp.maximum(jnp.max(s0, axis=0, keepdims=True),
                                           jnp.max(s1, axis=0, keepdims=True)))
        alpha = jnp.exp(m - m_new)
        p0 = jnp.exp(s0 - m_new)
        p1 = jnp.exp(s1 - m_new)
        l = alpha * l + jnp.sum(p0, axis=0, keepdims=True) + jnp.sum(p1, axis=0, keepdims=True)
        acc = (alpha * acc
               + jnp.dot(vt_ref[0, 0, :, pl.ds(tile_start(2 * kp), QT)], p0.astype(jnp.bfloat16),
                         preferred_element_type=jnp.float32)
               + jnp.dot(vt_ref[0, 0, :, pl.ds(tile_start(2 * kp + 1), QT)], p1.astype(jnp.bfloat16),
                         preferred_element_type=jnp.float32))
        return m_new, l, acc, n0, n1

    cols = NSA_REP * tq
    init = (jnp.full((1, cols), NEG, jnp.float32), jnp.zeros((1, cols), jnp.float32),
            jnp.zeros((HEAD_DIM, cols), jnp.float32), logits(0), logits(1))
    m, l, acc, _, _ = lax.fori_loop(0, n // 2 + 1, body, init)
    out = acc / l
    for r in range(NSA_REP):
        o_ref[0, :, r * HEAD_DIM:(r + 1) * HEAD_DIM] = out[:, r * tq:(r + 1) * tq].T.astype(o_ref.dtype)


def _sel_attention(proj3, sel, expand_t, bias_sel_t, lay, seq):
    b = proj3.shape[0]
    g = NSA_KV_GROUPS
    ns = seq // NSA_SEL_LEN
    qw = NSA_REP * HEAD_DIM
    nb = bias_sel_t.shape[1]
    cb_k = lay["kv"] + 2 * g
    cb_v = lay["kv"] + 3 * g
    vt = proj3[:, :, cb_v * LANES:(cb_v + g) * LANES].reshape(b, seq, g, HEAD_DIM).transpose(0, 2, 3, 1)
    return pl.pallas_call(
        _sel_attn_kernel,
        grid=(b, g, seq // QT),
        in_specs=[pl.BlockSpec((1, QT, qw), lambda i, j, n: (i, n, lay["q"] // NSA_REP + j)),
                  pl.BlockSpec((1, seq, HEAD_DIM), lambda i, j, n: (i, 0, cb_k + j)),
                  pl.BlockSpec((1, 1, HEAD_DIM, seq), lambda i, j, n: (i, j, 0, 0)),
                  pl.BlockSpec((1, 1, ns, QT), lambda i, j, n: (i, j, 0, n)),
                  pl.BlockSpec((seq, ns), lambda i, j, n: (0, 0)),
                  pl.BlockSpec((NSA_REP, nb, QT, QT), lambda i, j, n: (j, 0, 0, 0))],
        out_specs=pl.BlockSpec((1, QT, qw), lambda i, j, n: (i, n, j)),
        out_shape=jax.ShapeDtypeStruct((b, seq, NSA_HEADS * HEAD_DIM), jnp.bfloat16),
        scratch_shapes=[pltpu.VMEM((seq, QT), jnp.float32)],
        compiler_params=_cparams(("arbitrary", "arbitrary", "arbitrary")),
        name="nsa_selected",
    )(proj3, proj3, vt, sel, expand_t, bias_sel_t)


def _band_softmax(qs, k_tile, v_tile, bias, n, n_prev, scale):
    logits = []
    for j in range(n_prev + 1):
        kb = n - n_prev + j
        k = k_tile(jnp.maximum(kb, 0))
        s = lax.dot_general(qs, k, (((1,), (1,)), ((), ())), preferred_element_type=jnp.float32) * scale
        bj = jnp.where(kb >= 0, bias[:, j * QT:(j + 1) * QT], NEG)
        logits.append(jnp.where(bj > 0.5 * NEG, s + bj, NEG))
    m = logits[0].max(axis=-1, keepdims=True)
    for s in logits[1:]:
        m = jnp.maximum(m, s.max(axis=-1, keepdims=True))
    l = jnp.zeros_like(m)
    acc = jnp.zeros((qs.shape[0], HEAD_DIM), jnp.float32)
    for j, s in enumerate(logits):
        p = jnp.exp(s - m)
        l = l + jnp.sum(p, axis=-1, keepdims=True)
        acc = acc + jnp.dot(p.astype(jnp.bfloat16), v_tile(jnp.maximum(n - n_prev + j, 0)),
                            preferred_element_type=jnp.float32)
    return acc, m, l


def _win_attn_kernel(q_ref, k_ref, v_ref, bias_ref, o_ref, *, n_prev):
    tq = q_ref.shape[1]
    n = pl.program_id(2)
    qs = _stack_heads(q_ref[0])
    bias = jnp.concatenate([bias_ref[r, 0] for r in range(NSA_REP)], axis=0)
    tile = lambda ref: (lambda kb: ref[0, pl.ds(pl.multiple_of(kb * QT, QT), QT), :])
    acc, _, l = _band_softmax(qs, tile(k_ref), tile(v_ref), bias, n, n_prev, HEAD_DIM ** -0.5)
    o = acc / l
    for r in range(NSA_REP):
        o_ref[0, :, r * HEAD_DIM:(r + 1) * HEAD_DIM] = o[r * tq:(r + 1) * tq].astype(o_ref.dtype)


def _win_attention(proj3, bias_win, lay, seq):
    b = proj3.shape[0]
    g = NSA_KV_GROUPS
    qw = NSA_REP * HEAD_DIM
    n_prev = -(-(NSA_WINDOW - 1) // QT)
    cb_k = lay["kv"] + 4 * g
    cb_v = lay["kv"] + 5 * g
    return pl.pallas_call(
        functools.partial(_win_attn_kernel, n_prev=n_prev),
        grid=(b, g, seq // QT),
        in_specs=[pl.BlockSpec((1, QT, qw), lambda i, j, n: (i, n, lay["q"] // NSA_REP + j)),
                  pl.BlockSpec((1, seq, HEAD_DIM), lambda i, j, n: (i, 0, cb_k + j)),
                  pl.BlockSpec((1, seq, HEAD_DIM), lambda i, j, n: (i, 0, cb_v + j)),
                  pl.BlockSpec((NSA_REP, 1, QT, (n_prev + 1) * QT), lambda i, j, n: (j, 0, 0, 0))],
        out_specs=pl.BlockSpec((1, QT, qw), lambda i, j, n: (i, n, j)),
        out_shape=jax.ShapeDtypeStruct((b, seq, NSA_HEADS * HEAD_DIM), jnp.bfloat16),
        compiler_params=_cparams(("arbitrary", "arbitrary", "arbitrary")),
        name="nsa_window",
    )(proj3, proj3, proj3, bias_win)


DIL_TILES_PER_STEP = 4


def _dil_attn_kernel(q_ref, k_ref, v_ref, bias_ref, o_ref, lse_ref, *, n_prev, dil):
    tiles = q_ref.shape[1] // (QT * dil)
    for sub in range(DIL_TILES_PER_STEP):
        t = pl.program_id(2) * DIL_TILES_PER_STEP + sub
        r = t // tiles
        n = t % tiles
        rows = lambda kb, r=r: pl.ds(kb * (QT * dil) + r, QT, stride=dil) if dil > 1 else pl.ds(
            pl.multiple_of(kb * QT, QT), QT)
        tile = lambda ref, rows=rows: (lambda kb: ref[0, rows(kb), :].astype(jnp.bfloat16))
        acc, m, l = _band_softmax(tile(q_ref)(n), tile(k_ref), tile(v_ref), bias_ref[0, 0], n, n_prev,
                                  HEAD_DIM ** -0.5)
        o_ref[0, rows(n), :] = acc / l
        lse_ref[0, rows(n), :] = jnp.broadcast_to(m + jnp.log(l), (QT, HEAD_DIM))


def _dil_attention(projd3, bias_dil, seq, gi):
    b = projd3.shape[0]
    window, dil = DIL_CONFIGS[gi]
    p_heads = DIL_HEADS_PER_GROUP
    n_prev = -(-(window // dil) // QT)
    cb_q = gi * p_heads
    cb_k = cb_q + DIL_HEADS
    cb_v = cb_k + DIL_HEADS
    out_w = p_heads * HEAD_DIM
    whole = lambda cb: pl.BlockSpec((1, seq, HEAD_DIM), lambda i, p, t: (i, 0, cb + p))
    return pl.pallas_call(
        functools.partial(_dil_attn_kernel, n_prev=n_prev, dil=dil),
        grid=(b, p_heads, seq // (QT * DIL_TILES_PER_STEP)),
        in_specs=[whole(cb_q), whole(cb_k), whole(cb_v),
                  pl.BlockSpec((1, 1, QT, (n_prev + 1) * QT), lambda i, p, t: (p, 0, 0, 0))],
        out_specs=[whole(0), whole(0)],
        out_shape=[jax.ShapeDtypeStruct((b, seq, out_w), jnp.float32),
                   jax.ShapeDtypeStruct((b, seq, out_w), jnp.float32)],
        compiler_params=_cparams(("arbitrary",) * 3),
        name=f"dilated_attn_{gi}",
    )(projd3, projd3, projd3, bias_dil)


def _post_kernel(ocmp_ref, osel_ref, owin_ref, gate_ref, od0_ref, od1_ref, od2_ref, l0_ref, l1_ref, l2_ref,
                 mg1_ref, mg2_ref, x_ref, g1_ref, sc2_ref, sh2_ref, n2_ref, wn_ref, wd_ref, wo_ref,
                 x1_ref, h2_ref):
    gates = _sigmoid(gate_ref[...].astype(jnp.float32))
    parts = []
    for h in range(NSA_HEADS):
        sl = slice(h * HEAD_DIM, (h + 1) * HEAD_DIM)
        y = (gates[:, 3 * h:3 * h + 1] * ocmp_ref[:, sl].astype(jnp.float32)
             + gates[:, 3 * h + 1:3 * h + 2] * osel_ref[:, sl].astype(jnp.float32)
             + gates[:, 3 * h + 2:3 * h + 3] * owin_ref[:, sl].astype(jnp.float32))
        parts.append(y.astype(jnp.bfloat16))
    y_nsa = jnp.concatenate(parts, axis=1)

    l0, l1, l2 = l0_ref[...], l1_ref[...], l2_ref[...]
    mx = jnp.maximum(jnp.maximum(l0, l1), l2)
    e0, e1, e2 = jnp.exp(l0 - mx), jnp.exp(l1 - mx), jnp.exp(l2 - mx)
    den = e0 + e1 + e2
    y_dil = ((e0 / den) * od0_ref[...].astype(jnp.float32) + (e1 / den) * od1_ref[...].astype(jnp.float32)
             + (e2 / den) * od2_ref[...].astype(jnp.float32)).astype(jnp.bfloat16)

    a = jnp.dot(y_nsa, wn_ref[...], preferred_element_type=jnp.float32)
    bb = jnp.dot(y_dil, wd_ref[...], preferred_element_type=jnp.float32)
    merged = (_sigmoid(mg1_ref[...].astype(jnp.float32)) * a
              + _sigmoid(mg2_ref[...].astype(jnp.float32)) * bb).astype(jnp.bfloat16)
    mix = jnp.dot(merged, wo_ref[...], preferred_element_type=jnp.float32)
    x1 = x_ref[...] + g1_ref[0] * mix
    x1_ref[...] = x1
    y = x1 * lax.rsqrt(jnp.mean(x1 * x1, axis=-1, keepdims=True) + EPS) * n2_ref[...]
    h2_ref[...] = y * (1.0 + sc2_ref[0]) + sh2_ref[0]


def _post(ocmp, osel, owin, proj2d, odil, lsedil, x2d, g1, sc2, sh2, norm2_g, wn, wd, wo, lay, seq):
    t, d = x2d.shape
    tm = 256
    per_batch = seq // tm
    nw = NSA_HEADS * HEAD_DIM
    dw = DIL_HEADS_PER_GROUP * HEAD_DIM
    row = lambda w: pl.BlockSpec((tm, w), lambda i: (i, 0))
    mod = pl.BlockSpec((1, 1, d), lambda i: (i // per_batch, 0, 0))
    full = lambda a: pl.BlockSpec(a.shape, lambda i: (0, 0))
    return pl.pallas_call(
        _post_kernel,
        grid=(t // tm,),
        in_specs=[row(nw), row(nw), row(nw),
                  pl.BlockSpec((tm, LANES), lambda i: (i, lay["gate"])),
                  row(dw), row(dw), row(dw), row(dw), row(dw), row(dw),
                  pl.BlockSpec((tm, d), lambda i: (i, 0)),
                  pl.BlockSpec((tm, d), lambda i: (i, 1)),
                  row(d), mod, mod, mod,
                  pl.BlockSpec((1, d), lambda i: (0, 0)),
                  full(wn), full(wd), full(wo)],
        out_specs=[row(d), row(d)],
        out_shape=[jax.ShapeDtypeStruct((t, d), jnp.float32), jax.ShapeDtypeStruct((t, d), jnp.float32)],
        compiler_params=_cparams(("arbitrary",)),
        name="merge_out_proj",
    )(ocmp, osel, owin, proj2d, odil[0], odil[1], odil[2], lsedil[0], lsedil[1], lsedil[2],
      proj2d, proj2d, x2d, g1, sc2, sh2, norm2_g.reshape(1, d), wn, wd, wo)


def _topk_rows(sc, k, payload=None):
    n = sc.shape[0]
    row = lax.broadcasted_iota(jnp.int32, sc.shape, 0).astype(jnp.float32)
    vals, picks = [], []
    for _ in range(k):
        mx = jnp.max(sc, axis=0, keepdims=True)
        first = jnp.min(jnp.where(sc == mx, row, float(n)), axis=0, keepdims=True)
        hit = row == first
        vals.append(mx)
        if payload is None:
            picks.append(first)
        else:
            picks.append(jnp.sum(jnp.where(hit, payload, 0.0), axis=0, keepdims=True))
        sc = jnp.where(hit, -jnp.inf, sc)
    return jnp.concatenate(vals, axis=0), jnp.concatenate(picks, axis=0)


def _route_kernel(h_ref, wq_ref, gain_ref, keys_ref, idx_ref, gate_ref):
    q = jnp.dot(h_ref[...].astype(jnp.bfloat16), wq_ref[...], preferred_element_type=jnp.float32)
    k = PEER_TOPK
    for head in range(PEER_HEADS):
        tops = []
        for part in range(2):
            hp = head * 2 + part
            a = q[:, hp * PEER_HALF:(hp + 1) * PEER_HALF]
            qn = (a * lax.rsqrt(jnp.mean(a * a, axis=-1, keepdims=True) + EPS)
                  * gain_ref[hp:hp + 1, :]).astype(jnp.bfloat16)
            sc = lax.dot_general(keys_ref[hp], qn, (((1,), (1,)), ((), ())),
                                 preferred_element_type=jnp.float32)
            tops.append(_topk_rows(sc, k))
        (s1, i1), (s2, i2) = tops
        half = k // 2
        sub = lax.broadcasted_iota(jnp.int32, (half, s1.shape[1]), 0)
        vals = [s1[0:1] + s2]
        ids = [i1[0:1] * float(PEER_N_KEYS) + i2]
        for a in range(1, half):
            vals.append(jnp.where(sub < k // (a + 1), s1[a:a + 1] + s2[:half], -jnp.inf))
            ids.append(i1[a:a + 1] * float(PEER_N_KEYS) + i2[:half])
        vals.append(s1[half:] + s2[0:1])
        ids.append(i1[half:] * float(PEER_N_KEYS) + i2[0:1])
        top_s, expert = _topk_rows(jnp.concatenate(vals, axis=0), k, payload=jnp.concatenate(ids, axis=0))
        e = jnp.exp(top_s - top_s[0:1])
        gate = e / jnp.sum(e, axis=0, keepdims=True)
        idx_ref[head * k:(head + 1) * k, :] = expert.astype(jnp.int32)
        gate_ref[head * k:(head + 1) * k, :] = gate


def _route(h2, wq_bf16, q_gain, sub_keys_bf16):
    t, d = h2.shape
    tm = LANES
    hk = PEER_HEADS * PEER_TOPK
    nq = wq_bf16.shape[1]
    return pl.pallas_call(
        _route_kernel,
        grid=(t // tm,),
        in_specs=[pl.BlockSpec((tm, d), lambda i: (i, 0)),
                  pl.BlockSpec((d, nq), lambda i: (0, 0)),
                  pl.BlockSpec((2 * PEER_HEADS, PEER_HALF), lambda i: (0, 0)),
                  pl.BlockSpec((2 * PEER_HEADS, PEER_N_KEYS, PEER_HALF), lambda i: (0, 0, 0))],
        out_specs=[pl.BlockSpec((hk, tm), lambda i: (0, i)),
                   pl.BlockSpec((hk, tm), lambda i: (0, i))],
        out_shape=[jax.ShapeDtypeStruct((hk, t), jnp.int32), jax.ShapeDtypeStruct((hk, t), jnp.float32)],
        compiler_params=_cparams(("arbitrary",)),
        name="peer_route",
    )(h2, wq_bf16, q_gain.reshape(2 * PEER_HEADS, PEER_HALF), sub_keys_bf16)


PEER_TB = 32
SUBLANES = 8


PEER_STAGES = 4
PEER_LOOKAHEAD = 2


def _expert_kernel(idx_ref, nidx_ref, h_ref, gate_ref, x1_ref, g2_ref, tab_ref, o_ref, *scratch):
    stages, sem = scratch[:PEER_STAGES], scratch[PEER_STAGES]
    tb, d = h_ref.shape
    k = PEER_TOPK
    hk = PEER_HEADS * k
    nr, nc = tb // SUBLANES, d // LANES
    step = pl.program_id(0)
    hi_mask = jnp.uint32(0xFFFF0000)

    def start_rows(ids, head, j, r):
        for s in range(SUBLANES):
            e = ids[0, 0, (r * SUBLANES + s) * hk + head * k + j]
            pltpu.make_async_copy(tab_ref.at[pl.ds(pl.multiple_of(e * nc, nc), nc), :],
                                  stages[head % PEER_STAGES].at[j, r, :, s, :],
                                  sem.at[head % PEER_STAGES]).start(priority=s % 2)

    def words(stage, j, r, c):
        return stage[j, r, c]

    def start_head(ids, head):
        for j in range(k):
            for r in range(nr):
                start_rows(ids, head, j, r)

    def wait_head(head):
        buf = stages[head % PEER_STAGES]
        pltpu.make_async_copy(buf, buf, sem.at[head % PEER_STAGES]).wait()

    @pl.when(step == 0)
    def _():
        for head in range(PEER_LOOKAHEAD):
            start_head(idx_ref, head)

    lane = lax.broadcasted_iota(jnp.int32, (SUBLANES, LANES), 1)
    for head in range(PEER_HEADS):
        stage = stages[head % PEER_STAGES]
        wait_head(head)
        ahead = head + PEER_LOOKAHEAD
        ahead_ids, ahead_head = (idx_ref, ahead) if ahead < PEER_HEADS else (nidx_ref, ahead - PEER_HEADS)
        acts = []
        for r in range(nr):
            rows = slice(r * SUBLANES, (r + 1) * SUBLANES)
            pre = jnp.zeros((SUBLANES, LANES), jnp.float32)
            for j in range(k):
                start_rows(ahead_ids, ahead_head, j, r)
                acc = None
                for c in range(nc):
                    u = lax.bitcast_convert_type(words(stage, j, r, c) & hi_mask, jnp.float32)
                    prod = h_ref[rows, c * LANES:(c + 1) * LANES] * u
                    acc = prod if acc is None else acc + prod
                pre = jnp.where(lane == head * k + j, jnp.sum(acc, axis=-1, keepdims=True), pre)
            acts.append(gate_ref[rows, :] * _gelu(pre))
        for r in range(nr):
            rows = slice(r * SUBLANES, (r + 1) * SUBLANES)
            accs = [None] * nc
            for j in range(k):
                a = jnp.broadcast_to(acts[r][:, head * k + j:head * k + j + 1], (SUBLANES, LANES))
                for c in range(nc):
                    term = a * lax.bitcast_convert_type(words(stage, j, r, c) << 16, jnp.float32)
                    accs[c] = term if accs[c] is None else accs[c] + term
            for c in range(nc):
                cs = slice(c * LANES, (c + 1) * LANES)
                if head == 0:
                    o_ref[rows, cs] = accs[c]
                else:
                    o_ref[rows, cs] = o_ref[rows, cs] + accs[c]

    @pl.when(step + 1 == pl.num_programs(0))
    def _():
        for head in range(PEER_LOOKAHEAD):
            wait_head(head)
    o_ref[...] = x1_ref[...] + g2_ref[0] * o_ref[...]


def _experts(idx_tok, gate_tok, h2, x1, g2, table, seq):
    t, d = h2.shape
    tb = PEER_TB
    hk = PEER_HEADS * PEER_TOPK
    per_batch = seq // tb
    nsteps = t // tb
    ids = idx_tok.reshape(nsteps, 1, tb * hk)
    stage = pltpu.VMEM((PEER_TOPK, tb // SUBLANES, d // LANES, SUBLANES, LANES), jnp.uint32)
    return pl.pallas_call(
        _expert_kernel,
        grid=(nsteps,),
        in_specs=[pl.BlockSpec((1, 1, tb * hk), lambda i: (i, 0, 0), memory_space=pltpu.SMEM),
                  pl.BlockSpec((1, 1, tb * hk), lambda i: (jnp.minimum(i + 1, nsteps - 1), 0, 0),
                               memory_space=pltpu.SMEM),
                  pl.BlockSpec((tb, d), lambda i: (i, 0)),
                  pl.BlockSpec((tb, hk), lambda i: (i, 0)),
                  pl.BlockSpec((tb, d), lambda i: (i, 0)),
                  pl.BlockSpec((1, 1, d), lambda i: (i // per_batch, 0, 0)),
                  pl.BlockSpec(memory_space=pl.ANY)],
        out_specs=pl.BlockSpec((tb, d), lambda i: (i, 0)),
        out_shape=jax.ShapeDtypeStruct((t, d), jnp.float32),
        scratch_shapes=[stage] * PEER_STAGES + [pltpu.SemaphoreType.DMA((PEER_STAGES,))],
        compiler_params=_cparams(("arbitrary",)),
        name="peer_experts",
    )(ids, ids, h2, gate_tok, x1, g2, table)


PACK_ROWS = 64


def _pack_kernel(u_ref, v_ref, o_ref):
    nc = u_ref.shape[1] // LANES
    as_bits = lambda x: lax.bitcast_convert_type(x.astype(jnp.bfloat16).astype(jnp.float32), jnp.uint32)
    w = as_bits(u_ref[...]) | (as_bits(v_ref[...]) >> 16)
    for c in range(nc):
        o_ref[pl.ds(c, PACK_ROWS, stride=nc), :] = w[:, c * LANES:(c + 1) * LANES]


def _pack_expert_table(u, v):
    e, d = u.shape
    nc = d // LANES
    return pl.pallas_call(
        _pack_kernel,
        grid=(e // PACK_ROWS,),
        in_specs=[pl.BlockSpec((PACK_ROWS, d), lambda i: (i, 0)),
                  pl.BlockSpec((PACK_ROWS, d), lambda i: (i, 0))],
        out_specs=pl.BlockSpec((PACK_ROWS * nc, LANES), lambda i: (i, 0)),
        out_shape=jax.ShapeDtypeStruct((e * nc, LANES), jnp.uint32),
        compiler_params=_cparams(("arbitrary",)),
        name="peer_pack_table",
    )(u, v)


def _proj_weights(w_in, nsa_q_gain, nsa_k_gain, dil_q_gain, dil_k_gain, d_model, lay):
    dh = HEAD_DIM
    off_kv = NSA_HEADS * dh
    off_gate = off_kv + 6 * NSA_KV_GROUPS * dh
    off_dil = off_gate + 3 * NSA_HEADS
    off_merge = off_dil + 3 * DIL_HEADS * dh
    n_cols = lay["n"] * LANES
    pad = n_cols - (lay["gate"] * LANES + 3 * NSA_HEADS)
    w = jnp.concatenate([w_in[:, off_merge:], w_in[:, :off_kv], w_in[:, off_kv:off_gate],
                         w_in[:, off_gate:off_dil], jnp.zeros((d_model, pad), w_in.dtype)],
                        axis=1).astype(jnp.bfloat16)
    w_dil = w_in[:, off_dil:off_merge].astype(jnp.bfloat16)
    ones = jnp.ones((dh,), jnp.float32)

    def rows(spec):
        gains = jnp.concatenate([g for g, _, n in spec for _ in range(n)])
        flags = jnp.concatenate([jnp.full((dh,), f, jnp.float32) for _, f, n in spec for _ in range(n)])
        return flags.reshape(1, -1), gains.reshape(1, -1)

    g = NSA_KV_GROUPS
    main = [(ones, 0.0, lay["q"]), (nsa_q_gain, 1.0, NSA_HEADS), (ones, 0.0, 2 * g),
            (nsa_k_gain[1], 1.0, g), (ones, 0.0, g), (nsa_k_gain[2], 1.0, g), (ones, 0.0, g),
            (ones, 0.0, lay["n"] - lay["gate"])]
    dil = [(dil_q_gain, 1.0, DIL_HEADS), (dil_k_gain, 1.0, DIL_HEADS), (ones, 0.0, DIL_HEADS)]
    return (w,) + rows(main), (w_dil,) + rows(dil)


def _token_mixer_and_norm2(x, mod, norm1_g, w_in, nsa_q_gain, nsa_k_gain, cmp_pos, cmp_w1, cmp_w2,
                           dil_q_gain, dil_k_gain, w_br_nsa, w_br_dil, w_out, norm2_g, rel_bias):
    b, seq, d = x.shape
    t = b * seq
    lay = _layout(d)
    sh1, sc1, g1, sh2, sc2, g2 = [m.reshape(b, 1, d) for m in jnp.split(mod, 6, axis=-1)]
    x2d = x.reshape(t, d)

    main_w, dil_w = _proj_weights(w_in, nsa_q_gain, nsa_k_gain, dil_q_gain, dil_k_gain, d, lay)
    blocks_per_tile = 4
    proj2d = _projection(x2d, norm1_g, sc1, sh1, *main_w, seq, jnp.bfloat16, (0, lay["q"] // blocks_per_tile))
    proj3 = proj2d.reshape(b, seq, lay["n"] * LANES)
    projd3 = _projection(x2d, norm1_g, sc1, sh1, *dil_w, seq, jnp.float32,
                         (2 * DIL_HEADS // blocks_per_tile, 3 * DIL_HEADS // blocks_per_tile)).reshape(b, seq, -1)

    rel_flat = rel_bias.reshape(-1)
    nchunk = seq // NSA_CMP_STRIDE
    ns = seq // NSA_SEL_LEN
    bias_win = _bias_table(rel_flat, _band_buckets(-(-(NSA_WINDOW - 1) // QT), NSA_WINDOW - 1, 1), 0, NSA_HEADS)
    bias_sel = _bias_table(rel_flat, _causal_buckets(_n_causal_tiles(seq)), 0, NSA_HEADS)
    bias_cmp = _bias_table(rel_flat, _cmp_buckets(seq, nchunk), 0, NSA_HEADS)
    bias_dil = [_bias_table(rel_flat, _band_buckets(-(-(wnd // dil) // QT), wnd // dil, dil),
                            NSA_HEADS + gi * DIL_HEADS_PER_GROUP, DIL_HEADS_PER_GROUP)
                for gi, (wnd, dil) in enumerate(DIL_CONFIGS)]

    g = NSA_KV_GROUPS
    c0 = lay["kv"] * LANES
    tc = proj3[:, :, c0:c0 + 2 * g * HEAD_DIM].reshape(b, nchunk, NSA_CMP_STRIDE, 2 * g, HEAD_DIM)
    tc = tc.transpose(0, 3, 1, 2, 4).reshape(b, 2 * g, nchunk, NSA_CMP_STRIDE * HEAD_DIM)
    kcvc = _compress(tc, cmp_pos, cmp_w1, cmp_w2, nsa_k_gain[0])

    cstart = np.arange(nchunk) * NSA_CMP_STRIDE
    sstart = np.arange(ns) * NSA_SEL_LEN
    n_cmp = (seq - NSA_CMP_LEN) // NSA_CMP_STRIDE + 1
    overlap = ((cstart[:, None] < sstart[None, :] + NSA_SEL_LEN) & (cstart[:, None] + NSA_CMP_LEN > sstart[None, :])
               & (np.arange(nchunk)[:, None] < n_cmp)).astype(np.float32)
    expand_t = (np.arange(seq)[:, None] // NSA_SEL_LEN == np.arange(ns)[None, :]).astype(np.float32)

    ocmp, sel = _cmp_select(proj3, kcvc, bias_cmp, jnp.asarray(overlap.T), lay, seq)
    osel = _sel_attention(proj3, sel, jnp.asarray(expand_t, jnp.bfloat16), bias_sel, lay, seq)
    owin = _win_attention(proj3, bias_win, lay, seq)
    odil, lsedil = zip(*[_dil_attention(projd3, bias_dil[gi], seq, gi) for gi in range(len(DIL_CONFIGS))])

    x1, h2 = _post(ocmp.reshape(t, -1), osel.reshape(t, -1), owin.reshape(t, -1), proj2d,
                   [o.reshape(t, -1) for o in odil], [l.reshape(t, -1) for l in lsedil],
                   x2d, g1, sc2, sh2, norm2_g, w_br_nsa.astype(jnp.bfloat16), w_br_dil.astype(jnp.bfloat16),
                   w_out.astype(jnp.bfloat16), lay, seq)
    return x1, h2, g2


def kernel(x, c, w_ada, b_ada, norm1_g, w_in, nsa_q_gain, nsa_k_gain, cmp_pos, cmp_w1, cmp_w2,
           dil_q_gain, dil_k_gain, w_br_nsa, w_br_dil, w_out, norm2_g, peer_w_q, peer_q_gain,
           peer_sub_keys, peer_u, peer_v, rel_bias):
    b, seq, d = x.shape
    depth = w_ada.shape[0]
    for layer in range(depth):
        mod = _modulation(c, w_ada[layer], b_ada[layer])
        x1, h2, g2 = _token_mixer_and_norm2(
            x, mod, norm1_g[layer], w_in[layer], nsa_q_gain[layer], nsa_k_gain[layer], cmp_pos[layer],
            cmp_w1[layer], cmp_w2[layer], dil_q_gain[layer], dil_k_gain[layer], w_br_nsa[layer],
            w_br_dil[layer], w_out[layer], norm2_g[layer], rel_bias)
        idx, gate = _route(h2, peer_w_q[layer].astype(jnp.bfloat16), peer_q_gain[layer],
                           peer_sub_keys[layer].reshape(2 * PEER_HEADS, PEER_N_KEYS, PEER_HALF).astype(jnp.bfloat16))
        table = _pack_expert_table(peer_u[layer], peer_v[layer])
        out = _experts(idx.T, gate.T, h2, x1, g2, table, seq)
        x = out.reshape(b, seq, d)
    return x
```

```python
import functools
import math

import numpy as np
import jax
import jax.numpy as jnp
from jax import lax
from jax.experimental import pallas as pl
from jax.experimental.pallas import tpu as pltpu

HEAD_DIM = 128
LANES = 128
NSA_HEADS = 8
NSA_KV_GROUPS = 2
NSA_REP = NSA_HEADS // NSA_KV_GROUPS
NSA_CMP_LEN = 32
NSA_CMP_STRIDE = 16
NSA_SEL_LEN = 64
NSA_TOP_N = 16
NSA_WINDOW = 512
DIL_CONFIGS = ((128, 1), (512, 4), (2048, 16))
DIL_HEADS_PER_GROUP = 4
DIL_HEADS = DIL_HEADS_PER_GROUP * len(DIL_CONFIGS)
REL_BUCKETS = 32
REL_MAX_DIST = 2048
PEER_HEADS = 8
PEER_N_KEYS = 128
PEER_TOPK = 16
PEER_HALF = 128
EPS = 1e-6
NEG = -1e30
SEL_FORCE = 1e4
MASKED_BUCKET = REL_BUCKETS

QT = 128
VMEM_LIMIT = 48 * 1024 * 1024

CB_MERGE = 0
def _layout(d_model):
    nmerge = 2 * d_model // LANES
    cb_q = nmerge
    cb_kv = cb_q + NSA_HEADS
    cb_gate = cb_kv + 6 * NSA_KV_GROUPS
    n_used = cb_gate + 1
    n_blocks = -(-n_used // 4) * 4
    return dict(q=cb_q, kv=cb_kv, gate=cb_gate, n=n_blocks)


def _cparams(sem):
    return pltpu.CompilerParams(dimension_semantics=sem, vmem_limit_bytes=VMEM_LIMIT)


def _t5_bucket_np(dist):
    n = np.maximum(dist, 0)
    max_exact = REL_BUCKETS // 2
    nf = np.maximum(n, max_exact).astype(np.float32)
    large = max_exact + (np.log(nf / np.float32(max_exact)) / np.float32(math.log(REL_MAX_DIST / max_exact))
                         * np.float32(REL_BUCKETS - max_exact)).astype(np.int32)
    large = np.minimum(large, REL_BUCKETS - 1)
    return np.where(n < max_exact, n, large).astype(np.int32)


def _gelu(x):
    return jax.nn.gelu(x)


def _sigmoid(x):
    return 1.0 / (1.0 + jnp.exp(-x))


def _mod_kernel(c_ref, w_ref, b_ref, o_ref):
    c = c_ref[...]
    cond = c * _sigmoid(c)
    o_ref[...] = jnp.dot(cond.astype(jnp.bfloat16), w_ref[...].astype(jnp.bfloat16),
                         preferred_element_type=jnp.float32) + b_ref[...]


def _modulation(c, w_ada, b_ada):
    b, d = c.shape
    n = w_ada.shape[1]
    rows = 8
    cp = jnp.zeros((rows, d), jnp.float32).at[:b].set(c)
    tn = 512
    out = pl.pallas_call(
        _mod_kernel,
        grid=(n // tn,),
        in_specs=[pl.BlockSpec((rows, d), lambda j: (0, 0)),
                  pl.BlockSpec((d, tn), lambda j: (0, j)),
                  pl.BlockSpec((1, tn), lambda j: (0, j))],
        out_specs=pl.BlockSpec((rows, tn), lambda j: (0, j)),
        out_shape=jax.ShapeDtypeStruct((rows, n), jnp.float32),
        compiler_params=_cparams(("arbitrary",)),
        name="adaln_mod",
    )(cp, w_ada, b_ada.reshape(1, n))
    return out[:b]


def _proj_kernel(x_ref, g_ref, sc_ref, sh_ref, w_ref, flag_ref, gain_ref, o_ref, h_ref, *, plain):
    j = pl.program_id(1)

    @pl.when(j == 0)
    def _():
        x = x_ref[...]
        y = x * lax.rsqrt(jnp.mean(x * x, axis=-1, keepdims=True) + EPS) * g_ref[...]
        h_ref[...] = (y * (1.0 + sc_ref[0]) + sh_ref[0]).astype(jnp.bfloat16)

    acc = jnp.dot(h_ref[...], w_ref[...], preferred_element_type=jnp.float32)
    is_plain = (j >= plain[0]) & (j < plain[1])

    @pl.when(is_plain)
    def _():
        o_ref[...] = acc.astype(o_ref.dtype)

    @pl.when(jnp.logical_not(is_plain))
    def _():
        for k in range(acc.shape[1] // LANES):
            a = acc[:, k * LANES:(k + 1) * LANES]
            f = flag_ref[:, k * LANES:(k + 1) * LANES]
            r = lax.rsqrt(jnp.mean(a * a, axis=-1, keepdims=True) + EPS)
            scale = f * r + (1.0 - f)
            o_ref[:, k * LANES:(k + 1) * LANES] = (a * scale * gain_ref[:, k * LANES:(k + 1) * LANES]).astype(o_ref.dtype)


def _projection(x2d, norm_g, sc, sh, w_bf16, flags, gains, seq, out_dtype, plain):
    t, d = x2d.shape
    n = w_bf16.shape[1]
    tm = min(1024, seq)
    tn = 512
    per_batch = seq // tm
    return pl.pallas_call(
        functools.partial(_proj_kernel, plain=plain),
        grid=(t // tm, n // tn),
        in_specs=[pl.BlockSpec((tm, d), lambda i, j: (i, 0)),
                  pl.BlockSpec((1, d), lambda i, j: (0, 0)),
                  pl.BlockSpec((1, 1, d), lambda i, j: (i // per_batch, 0, 0)),
                  pl.BlockSpec((1, 1, d), lambda i, j: (i // per_batch, 0, 0)),
                  pl.BlockSpec((d, tn), lambda i, j: (0, j)),
                  pl.BlockSpec((1, tn), lambda i, j: (0, j)),
                  pl.BlockSpec((1, tn), lambda i, j: (0, j))],
        out_specs=pl.BlockSpec((tm, tn), lambda i, j: (i, j)),
        out_shape=jax.ShapeDtypeStruct((t, n), out_dtype),
        scratch_shapes=[pltpu.VMEM((tm, d), jnp.bfloat16)],
        compiler_params=_cparams(("arbitrary", "arbitrary")),
        name="norm_in_proj",
    )(x2d, norm_g.reshape(1, d), sc, sh, w_bf16, flags, gains)


def _bias_kernel(tbl_ref, idx_ref, o_ref, *, head0):
    h = pl.program_id(0) + head0
    b = idx_ref[0]
    out = jnp.full(b.shape, NEG, jnp.float32)
    for k in range(REL_BUCKETS):
        out = jnp.where(b == k, tbl_ref[h * REL_BUCKETS + k], out)
    o_ref[0, 0] = out


def _bias_table(rel_bias_flat, bucket_idx, head0, n_heads):
    nt, r, c = bucket_idx.shape
    return pl.pallas_call(
        functools.partial(_bias_kernel, head0=head0),
        grid=(n_heads, nt),
        in_specs=[pl.BlockSpec(memory_space=pltpu.SMEM),
                  pl.BlockSpec((1, r, c), lambda h, t: (t, 0, 0))],
        out_specs=pl.BlockSpec((1, 1, r, c), lambda h, t: (h, t, 0, 0)),
        out_shape=jax.ShapeDtypeStruct((n_heads, nt, r, c), jnp.float32),
        compiler_params=_cparams(("arbitrary", "arbitrary")),
        name="rel_bias_table",
    )(rel_bias_flat, jnp.asarray(bucket_idx))


def _band_buckets(n_prev, span, dist_scale):
    i = np.arange(QT)[:, None]
    c = np.arange((n_prev + 1) * QT)[None, :]
    dist = n_prev * QT + i - c
    ok = (dist >= 0) & (dist <= span)
    return np.where(ok, _t5_bucket_np(dist * dist_scale), MASKED_BUCKET).astype(np.int32)[None]


def _causal_buckets(n_tiles):
    i = np.arange(QT)[None, :]
    j = np.arange(QT)[:, None]
    out = []
    for o in range(n_tiles):
        dist = QT * o + i - j
        out.append(np.where(dist >= 0, _t5_bucket_np(dist), MASKED_BUCKET))
    return np.stack(out).astype(np.int32)


def _n_causal_tiles(seq):
    first_last = int(np.argmax(_t5_bucket_np(np.arange(4 * REL_MAX_DIST)) == REL_BUCKETS - 1))
    o_const = -(-(first_last + QT - 1) // QT)
    return min(seq // QT, o_const + 1)


def _cmp_buckets(seq, n_cmp_pad):
    nq = seq // QT
    n_cmp = (seq - NSA_CMP_LEN) // NSA_CMP_STRIDE + 1
    i = np.arange(QT)[None, :, None]
    n = np.arange(nq)[:, None, None]
    c = np.arange(n_cmp_pad)[None, None, :]
    dist = n * QT + i - (c * NSA_CMP_STRIDE + NSA_CMP_LEN - 1)
    ok = (dist >= 0) & (c < n_cmp)
    return np.where(ok, _t5_bucket_np(dist), MASKED_BUCKET).astype(np.int32)


def _compress_kernel(t_ref, pos_ref, w1_ref, w2_ref, gain_ref, o_ref):
    half = t_ref.shape[3]
    a = t_ref[0, 0].astype(jnp.float32)
    lo = (a + pos_ref[0, :, :half]).astype(jnp.bfloat16)
    hi = (a + pos_ref[0, :, half:]).astype(jnp.bfloat16)
    p = jnp.dot(lo, w1_ref[0, :half, :], preferred_element_type=jnp.float32)
    q = jnp.dot(hi, w1_ref[0, half:, :], preferred_element_type=jnp.float32)
    n = q.shape[0]
    q_next = jnp.concatenate([q[1:], q[:1]], axis=0)
    hid = _gelu(p + q_next)
    out = jnp.dot(hid.astype(jnp.bfloat16), w2_ref[0], preferred_element_type=jnp.float32)
    is_key = pl.program_id(1) < NSA_KV_GROUPS
    r = lax.rsqrt(jnp.mean(out * out, axis=-1, keepdims=True) + EPS)
    normed = out * r * gain_ref[...]
    o_ref[0, 0] = jnp.where(is_key, normed, out).astype(o_ref.dtype)


def _compress(t_chunks, cmp_pos, cmp_w1, cmp_w2, k_gain0):
    b, nh, nchunk, width = t_chunks.shape
    dh = HEAD_DIM
    g = NSA_KV_GROUPS
    return pl.pallas_call(
        _compress_kernel,
        grid=(b, nh),
        in_specs=[pl.BlockSpec((1, 1, nchunk, width), lambda i, h: (i, h, 0, 0)),
                  pl.BlockSpec((1, 1, 2 * width), lambda i, h: (h // g, 0, 0)),
                  pl.BlockSpec((1, 2 * width, dh), lambda i, h: (h // g, 0, 0)),
                  pl.BlockSpec((1, dh, dh), lambda i, h: (h // g, 0, 0)),
                  pl.BlockSpec((1, dh), lambda i, h: (0, 0))],
        out_specs=pl.BlockSpec((1, 1, nchunk, dh), lambda i, h: (i, h, 0, 0)),
        out_shape=jax.ShapeDtypeStruct((b, nh, nchunk, dh), jnp.bfloat16),
        compiler_params=_cparams(("arbitrary", "arbitrary")),
        name="nsa_compress",
    )(t_chunks, cmp_pos.reshape(2, 1, 2 * width), cmp_w1.astype(jnp.bfloat16),
      cmp_w2.astype(jnp.bfloat16), k_gain0.reshape(1, dh))


def _stack_heads(q):
    return jnp.concatenate([q[:, r * HEAD_DIM:(r + 1) * HEAD_DIM] for r in range(NSA_REP)], axis=0)


def _cmp_select_kernel(q_ref, kc_ref, vc_ref, bias_ref, ov_ref, o_ref, sel_ref, *, n_sel):
    tq = q_ref.shape[1]
    n = pl.program_id(2)
    scale = HEAD_DIM ** -0.5
    qs = _stack_heads(q_ref[0])
    s = lax.dot_general(qs, kc_ref[0, 0], (((1,), (1,)), ((), ())),
                        preferred_element_type=jnp.float32) * scale
    bias = jnp.concatenate([bias_ref[r, 0] for r in range(NSA_REP)], axis=0)
    valid = bias > 0.5 * NEG
    s = jnp.where(valid, s + bias, NEG)
    m = jnp.max(s, axis=-1, keepdims=True)
    e = jnp.exp(s - m)
    p = e / jnp.sum(e, axis=-1, keepdims=True) * valid.astype(jnp.float32)
    o = jnp.dot(p.astype(jnp.bfloat16), vc_ref[0, 0], preferred_element_type=jnp.float32)
    for r in range(NSA_REP):
        o_ref[0, :, r * HEAD_DIM:(r + 1) * HEAD_DIM] = o[r * tq:(r + 1) * tq].astype(o_ref.dtype)

    psum = p[0:tq]
    for r in range(1, NSA_REP):
        psum = psum + p[r * tq:(r + 1) * tq]
    imp = lax.dot_general(ov_ref[...], psum, (((1,), (1,)), ((), ())), preferred_element_type=jnp.float32,
                          precision=lax.Precision.HIGHEST)
    n_blocks = sel_ref.shape[2]
    jb = lax.broadcasted_iota(jnp.int32, (n_blocks, tq), 0).astype(jnp.float32)
    qpos = lax.broadcasted_iota(jnp.int32, (n_blocks, tq), 1)
    cur = ((n * tq + qpos) // NSA_SEL_LEN).astype(jnp.float32)
    forced = (jb == 0.0) | (jb == cur) | (jb == cur - 1.0)
    score = jnp.where(forced, SEL_FORCE, jnp.where(jb > cur, -SEL_FORCE, imp))
    chosen = jnp.zeros((n_blocks, tq), jnp.float32)
    for _ in range(n_sel):
        mx = jnp.max(score, axis=0, keepdims=True)
        first = jnp.min(jnp.where(score == mx, jb, float(n_blocks)), axis=0, keepdims=True)
        hit = jb == first
        chosen = jnp.where(hit, 1.0, chosen)
        score = jnp.where(hit, -jnp.inf, score)
    sel_ref[0, 0] = chosen


def _cmp_select(proj3, kcvc, cbias, overlap, lay, seq):
    b = proj3.shape[0]
    g = NSA_KV_GROUPS
    ncp = kcvc.shape[2]
    ns = seq // NSA_SEL_LEN
    n_sel = min(NSA_TOP_N, ns)
    qw = NSA_REP * HEAD_DIM
    return pl.pallas_call(
        functools.partial(_cmp_select_kernel, n_sel=n_sel),
        grid=(b, g, seq // QT),
        in_specs=[pl.BlockSpec((1, QT, qw), lambda i, j, n: (i, n, lay["q"] // NSA_REP + j)),
                  pl.BlockSpec((1, 1, ncp, HEAD_DIM), lambda i, j, n: (i, j, 0, 0)),
                  pl.BlockSpec((1, 1, ncp, HEAD_DIM), lambda i, j, n: (i, g + j, 0, 0)),
                  pl.BlockSpec((NSA_REP, 1, QT, ncp), lambda i, j, n: (j, n, 0, 0)),
                  pl.BlockSpec((ns, ncp), lambda i, j, n: (0, 0))],
        out_specs=[pl.BlockSpec((1, QT, qw), lambda i, j, n: (i, n, j)),
                   pl.BlockSpec((1, 1, ns, QT), lambda i, j, n: (i, j, 0, n))],
        out_shape=[jax.ShapeDtypeStruct((b, seq, NSA_HEADS * HEAD_DIM), jnp.bfloat16),
                   jax.ShapeDtypeStruct((b, g, ns, seq), jnp.float32)],
        compiler_params=_cparams(("arbitrary", "arbitrary", "arbitrary")),
        name="nsa_cmp_select",
    )(proj3, kcvc, kcvc, cbias, overlap)


def _sel_attn_kernel(q_ref, k_ref, vt_ref, sel_ref, exp_ref, bias_ref, o_ref, madd_ref):
    tq = q_ref.shape[1]
    n = pl.program_id(2)
    scale = HEAD_DIM ** -0.5
    n_bias = bias_ref.shape[1]
    q = q_ref[0].astype(jnp.float32)
    qt = jnp.concatenate([q[:, r * HEAD_DIM:(r + 1) * HEAD_DIM].T for r in range(NSA_REP)],
                         axis=1).astype(jnp.bfloat16)
    km = jnp.dot(exp_ref[...], sel_ref[0, 0].astype(jnp.bfloat16), preferred_element_type=jnp.float32)
    madd_ref[...] = (km - 1.0) * (-NEG)

    last_tile = k_ref.shape[1] // QT - 1

    def tile_start(kt):
        return pl.multiple_of(jnp.minimum(kt, last_tile) * QT, QT)

    def logits(kt):
        start = tile_start(kt)
        s = jnp.dot(k_ref[0, pl.ds(start, QT), :], qt, preferred_element_type=jnp.float32) * scale
        o = n - kt
        ob = jnp.clip(o, 0, n_bias - 1)
        bias = jnp.concatenate([bias_ref[r, ob] for r in range(NSA_REP)], axis=1)
        ma = jnp.where(o >= 0, madd_ref[pl.ds(start, QT), :], NEG)
        return s + bias + jnp.concatenate([ma] * NSA_REP, axis=1)

    def body(kp, carry):
        m, l, acc, s0, s1 = carry
        n0, n1 = logits(2 * kp + 2), logits(2 * kp + 3)
        m_new = jnp.maximum(m, jnp.maximum(jnp.max(s0, axis=0, keepdims=True),
                                           jnp.max(s1, axis=0, keepdims=True)))
        alpha = jnp.exp(m - m_new)
        p0 = jnp.exp(s0 - m_new)
        p1 = jnp.exp(s1 - m_new)
        l = alpha * l + jnp.sum(p0, axis=0, keepdims=True) + jnp.sum(p1, axis=0, keepdims=True)
        acc = (alpha * acc
               + jnp.dot(vt_ref[0, 0, :, pl.ds(tile_start(2 * kp), QT)], p0.astype(jnp.bfloat16),
                         preferred_element_type=jnp.float32)
               + jnp.dot(vt_ref[0, 0, :, pl.ds(tile_start(2 * kp + 1), QT)], p1.astype(jnp.bfloat16),
                         preferred_element_type=jnp.float32))
        return m_new, l, acc, n0, n1

    cols = NSA_REP * tq
    init = (jnp.full((1, cols), NEG, jnp.float32), jnp.zeros((1, cols), jnp.float32),
            jnp.zeros((HEAD_DIM, cols), jnp.float32), logits(0), logits(1))
    m, l, acc, _, _ = lax.fori_loop(0, n // 2 + 1, body, init)
    out = acc / l
    for r in range(NSA_REP):
        o_ref[0, :, r * HEAD_DIM:(r + 1) * HEAD_DIM] = out[:, r * tq:(r + 1) * tq].T.astype(o_ref.dtype)


def _sel_attention(proj3, sel, expand_t, bias_sel_t, lay, seq):
    b = proj3.shape[0]
    g = NSA_KV_GROUPS
    ns = seq // NSA_SEL_LEN
    qw = NSA_REP * HEAD_DIM
    nb = bias_sel_t.shape[1]
    cb_k = lay["kv"] + 2 * g
    cb_v = lay["kv"] + 3 * g
    vt = proj3[:, :, cb_v * LANES:(cb_v + g) * LANES].reshape(b, seq, g, HEAD_DIM).transpose(0, 2, 3, 1)
    return pl.pallas_call(
        _sel_attn_kernel,
        grid=(b, g, seq // QT),
        in_specs=[pl.BlockSpec((1, QT, qw), lambda i, j, n: (i, n, lay["q"] // NSA_REP + j)),
                  pl.BlockSpec((1, seq, HEAD_DIM), lambda i, j, n: (i, 0, cb_k + j)),
                  pl.BlockSpec((1, 1, HEAD_DIM, seq), lambda i, j, n: (i, j, 0, 0)),
                  pl.BlockSpec((1, 1, ns, QT), lambda i, j, n: (i, j, 0, n)),
                  pl.BlockSpec((seq, ns), lambda i, j, n: (0, 0)),
                  pl.BlockSpec((NSA_REP, nb, QT, QT), lambda i, j, n: (j, 0, 0, 0))],
        out_specs=pl.BlockSpec((1, QT, qw), lambda i, j, n: (i, n, j)),
        out_shape=jax.ShapeDtypeStruct((b, seq, NSA_HEADS * HEAD_DIM), jnp.bfloat16),
        scratch_shapes=[pltpu.VMEM((seq, QT), jnp.float32)],
        compiler_params=_cparams(("arbitrary", "arbitrary", "arbitrary")),
        name="nsa_selected",
    )(proj3, proj3, vt, sel, expand_t, bias_sel_t)


def _band_softmax(qs, k_tile, v_tile, bias, n, n_prev, scale):
    logits = []
    for j in range(n_prev + 1):
        kb = n - n_prev + j
        k = k_tile(jnp.maximum(kb, 0))
        s = lax.dot_general(qs, k, (((1,), (1,)), ((), ())), preferred_element_type=jnp.float32) * scale
        bj = jnp.where(kb >= 0, bias[:, j * QT:(j + 1) * QT], NEG)
        logits.append(jnp.where(bj > 0.5 * NEG, s + bj, NEG))
    m = logits[0].max(axis=-1, keepdims=True)
    for s in logits[1:]:
        m = jnp.maximum(m, s.max(axis=-1, keepdims=True))
    l = jnp.zeros_like(m)
    acc = jnp.zeros((qs.shape[0], HEAD_DIM), jnp.float32)
    for j, s in enumerate(logits):
        p = jnp.exp(s - m)
        l = l + jnp.sum(p, axis=-1, keepdims=True)
        acc = acc + jnp.dot(p.astype(jnp.bfloat16), v_tile(jnp.maximum(n - n_prev + j, 0)),
                            preferred_element_type=jnp.float32)
    return acc, m, l


def _win_attn_kernel(q_ref, k_ref, v_ref, bias_ref, o_ref, *, n_prev):
    tq = q_ref.shape[1]
    n = pl.program_id(2)
    qs = _stack_heads(q_ref[0])
    bias = jnp.concatenate([bias_ref[r, 0] for r in range(NSA_REP)], axis=0)
    tile = lambda ref: (lambda kb: ref[0, pl.ds(pl.multiple_of(kb * QT, QT), QT), :])
    acc, _, l = _band_softmax(qs, tile(k_ref), tile(v_ref), bias, n, n_prev, HEAD_DIM ** -0.5)
    o = acc / l
    for r in range(NSA_REP):
        o_ref[0, :, r * HEAD_DIM:(r + 1) * HEAD_DIM] = o[r * tq:(r + 1) * tq].astype(o_ref.dtype)


def _win_attention(proj3, bias_win, lay, seq):
    b = proj3.shape[0]
    g = NSA_KV_GROUPS
    qw = NSA_REP * HEAD_DIM
    n_prev = -(-(NSA_WINDOW - 1) // QT)
    cb_k = lay["kv"] + 4 * g
    cb_v = lay["kv"] + 5 * g
    return pl.pallas_call(
        functools.partial(_win_attn_kernel, n_prev=n_prev),
        grid=(b, g, seq // QT),
        in_specs=[pl.BlockSpec((1, QT, qw), lambda i, j, n: (i, n, lay["q"] // NSA_REP + j)),
                  pl.BlockSpec((1, seq, HEAD_DIM), lambda i, j, n: (i, 0, cb_k + j)),
                  pl.BlockSpec((1, seq, HEAD_DIM), lambda i, j, n: (i, 0, cb_v + j)),
                  pl.BlockSpec((NSA_REP, 1, QT, (n_prev + 1) * QT), lambda i, j, n: (j, 0, 0, 0))],
        out_specs=pl.BlockSpec((1, QT, qw), lambda i, j, n: (i, n, j)),
        out_shape=jax.ShapeDtypeStruct((b, seq, NSA_HEADS * HEAD_DIM), jnp.bfloat16),
        compiler_params=_cparams(("arbitrary", "arbitrary", "arbitrary")),
        name="nsa_window",
    )(proj3, proj3, proj3, bias_win)


DIL_TILES_PER_STEP = 4


def _dil_attn_kernel(q_ref, k_ref, v_ref, bias_ref, o_ref, lse_ref, *, n_prev, dil):
    tiles = q_ref.shape[1] // (QT * dil)
    for sub in range(DIL_TILES_PER_STEP):
        t = pl.program_id(2) * DIL_TILES_PER_STEP + sub
        r = t // tiles
        n = t % tiles
        rows = lambda kb, r=r: pl.ds(kb * (QT * dil) + r, QT, stride=dil) if dil > 1 else pl.ds(
            pl.multiple_of(kb * QT, QT), QT)
        tile = lambda ref, rows=rows: (lambda kb: ref[0, rows(kb), :].astype(jnp.bfloat16))
        acc, m, l = _band_softmax(tile(q_ref)(n), tile(k_ref), tile(v_ref), bias_ref[0, 0], n, n_prev,
                                  HEAD_DIM ** -0.5)
        o_ref[0, rows(n), :] = acc / l
        lse_ref[0, rows(n), :] = jnp.broadcast_to(m + jnp.log(l), (QT, HEAD_DIM))


def _dil_attention(projd3, bias_dil, seq, gi):
    b = projd3.shape[0]
    window, dil = DIL_CONFIGS[gi]
    p_heads = DIL_HEADS_PER_GROUP
    n_prev = -(-(window // dil) // QT)
    cb_q = gi * p_heads
    cb_k = cb_q + DIL_HEADS
    cb_v = cb_k + DIL_HEADS
    out_w = p_heads * HEAD_DIM
    whole = lambda cb: pl.BlockSpec((1, seq, HEAD_DIM), lambda i, p, t: (i, 0, cb + p))
    return pl.pallas_call(
        functools.partial(_dil_attn_kernel, n_prev=n_prev, dil=dil),
        grid=(b, p_heads, seq // (QT * DIL_TILES_PER_STEP)),
        in_specs=[whole(cb_q), whole(cb_k), whole(cb_v),
                  pl.BlockSpec((1, 1, QT, (n_prev + 1) * QT), lambda i, p, t: (p, 0, 0, 0))],
        out_specs=[whole(0), whole(0)],
        out_shape=[jax.ShapeDtypeStruct((b, seq, out_w), jnp.float32),
                   jax.ShapeDtypeStruct((b, seq, out_w), jnp.float32)],
        compiler_params=_cparams(("arbitrary",) * 3),
        name=f"dilated_attn_{gi}",
    )(projd3, projd3, projd3, bias_dil)


def _post_kernel(ocmp_ref, osel_ref, owin_ref, gate_ref, od0_ref, od1_ref, od2_ref, l0_ref, l1_ref, l2_ref,
                 mg1_ref, mg2_ref, x_ref, g1_ref, sc2_ref, sh2_ref, n2_ref, wn_ref, wd_ref, wo_ref,
                 x1_ref, h2_ref):
    gates = _sigmoid(gate_ref[...].astype(jnp.float32))
    parts = []
    for h in range(NSA_HEADS):
        sl = slice(h * HEAD_DIM, (h + 1) * HEAD_DIM)
        y = (gates[:, 3 * h:3 * h + 1] * ocmp_ref[:, sl].astype(jnp.float32)
             + gates[:, 3 * h + 1:3 * h + 2] * osel_ref[:, sl].astype(jnp.float32)
             + gates[:, 3 * h + 2:3 * h + 3] * owin_ref[:, sl].astype(jnp.float32))
        parts.append(y.astype(jnp.bfloat16))
    y_nsa = jnp.concatenate(parts, axis=1)

    l0, l1, l2 = l0_ref[...], l1_ref[...], l2_ref[...]
    mx = jnp.maximum(jnp.maximum(l0, l1), l2)
    e0, e1, e2 = jnp.exp(l0 - mx), jnp.exp(l1 - mx), jnp.exp(l2 - mx)
    den = e0 + e1 + e2
    y_dil = ((e0 / den) * od0_ref[...].astype(jnp.float32) + (e1 / den) * od1_ref[...].astype(jnp.float32)
             + (e2 / den) * od2_ref[...].astype(jnp.float32)).astype(jnp.bfloat16)

    a = jnp.dot(y_nsa, wn_ref[...], preferred_element_type=jnp.float32)
    bb = jnp.dot(y_dil, wd_ref[...], preferred_element_type=jnp.float32)
    merged = (_sigmoid(mg1_ref[...].astype(jnp.float32)) * a
              + _sigmoid(mg2_ref[...].astype(jnp.float32)) * bb).astype(jnp.bfloat16)
    mix = jnp.dot(merged, wo_ref[...], preferred_element_type=jnp.float32)
    x1 = x_ref[...] + g1_ref[0] * mix
    x1_ref[...] = x1
    y = x1 * lax.rsqrt(jnp.mean(x1 * x1, axis=-1, keepdims=True) + EPS) * n2_ref[...]
    h2_ref[...] = y * (1.0 + sc2_ref[0]) + sh2_ref[0]


def _post(ocmp, osel, owin, proj2d, odil, lsedil, x2d, g1, sc2, sh2, norm2_g, wn, wd, wo, lay, seq):
    t, d = x2d.shape
    tm = 256
    per_batch = seq // tm
    nw = NSA_HEADS * HEAD_DIM
    dw = DIL_HEADS_PER_GROUP * HEAD_DIM
    row = lambda w: pl.BlockSpec((tm, w), lambda i: (i, 0))
    mod = pl.BlockSpec((1, 1, d), lambda i: (i // per_batch, 0, 0))
    full = lambda a: pl.BlockSpec(a.shape, lambda i: (0, 0))
    return pl.pallas_call(
        _post_kernel,
        grid=(t // tm,),
        in_specs=[row(nw), row(nw), row(nw),
                  pl.BlockSpec((tm, LANES), lambda i: (i, lay["gate"])),
                  row(dw), row(dw), row(dw), row(dw), row(dw), row(dw),
                  pl.BlockSpec((tm, d), lambda i: (i, 0)),
                  pl.BlockSpec((tm, d), lambda i: (i, 1)),
                  row(d), mod, mod, mod,
                  pl.BlockSpec((1, d), lambda i: (0, 0)),
                  full(wn), full(wd), full(wo)],
        out_specs=[row(d), row(d)],
        out_shape=[jax.ShapeDtypeStruct((t, d), jnp.float32), jax.ShapeDtypeStruct((t, d), jnp.float32)],
        compiler_params=_cparams(("arbitrary",)),
        name="merge_out_proj",
    )(ocmp, osel, owin, proj2d, odil[0], odil[1], odil[2], lsedil[0], lsedil[1], lsedil[2],
      proj2d, proj2d, x2d, g1, sc2, sh2, norm2_g.reshape(1, d), wn, wd, wo)


def _topk_rows(sc, k, payload=None):
    n = sc.shape[0]
    row = lax.broadcasted_iota(jnp.int32, sc.shape, 0).astype(jnp.float32)
    vals, picks = [], []
    for _ in range(k):
        mx = jnp.max(sc, axis=0, keepdims=True)
        first = jnp.min(jnp.where(sc == mx, row, float(n)), axis=0, keepdims=True)
        hit = row == first
        vals.append(mx)
        if payload is None:
            picks.append(first)
        else:
            picks.append(jnp.sum(jnp.where(hit, payload, 0.0), axis=0, keepdims=True))
        sc = jnp.where(hit, -jnp.inf, sc)
    return jnp.concatenate(vals, axis=0), jnp.concatenate(picks, axis=0)


def _route_kernel(h_ref, wq_ref, gain_ref, keys_ref, idx_ref, gate_ref):
    q = jnp.dot(h_ref[...].astype(jnp.bfloat16), wq_ref[...], preferred_element_type=jnp.float32)
    k = PEER_TOPK
    for head in range(PEER_HEADS):
        tops = []
        for part in range(2):
            hp = head * 2 + part
            a = q[:, hp * PEER_HALF:(hp + 1) * PEER_HALF]
            qn = (a * lax.rsqrt(jnp.mean(a * a, axis=-1, keepdims=True) + EPS)
                  * gain_ref[hp:hp + 1, :]).astype(jnp.bfloat16)
            sc = lax.dot_general(keys_ref[hp], qn, (((1,), (1,)), ((), ())),
                                 preferred_element_type=jnp.float32)
            tops.append(_topk_rows(sc, k))
        (s1, i1), (s2, i2) = tops
        half = k // 2
        sub = lax.broadcasted_iota(jnp.int32, (half, s1.shape[1]), 0)
        vals = [s1[0:1] + s2]
        ids = [i1[0:1] * float(PEER_N_KEYS) + i2]
        for a in range(1, half):
            vals.append(jnp.where(sub < k // (a + 1), s1[a:a + 1] + s2[:half], -jnp.inf))
            ids.append(i1[a:a + 1] * float(PEER_N_KEYS) + i2[:half])
        vals.append(s1[half:] + s2[0:1])
        ids.append(i1[half:] * float(PEER_N_KEYS) + i2[0:1])
        top_s, expert = _topk_rows(jnp.concatenate(vals, axis=0), k, payload=jnp.concatenate(ids, axis=0))
        e = jnp.exp(top_s - top_s[0:1])
        gate = e / jnp.sum(e, axis=0, keepdims=True)
        idx_ref[head * k:(head + 1) * k, :] = expert.astype(jnp.int32)
        gate_ref[head * k:(head + 1) * k, :] = gate


def _route(h2, wq_bf16, q_gain, sub_keys_bf16):
    t, d = h2.shape
    tm = LANES
    hk = PEER_HEADS * PEER_TOPK
    nq = wq_bf16.shape[1]
    return pl.pallas_call(
        _route_kernel,
        grid=(t // tm,),
        in_specs=[pl.BlockSpec((tm, d), lambda i: (i, 0)),
                  pl.BlockSpec((d, nq), lambda i: (0, 0)),
                  pl.BlockSpec((2 * PEER_HEADS, PEER_HALF), lambda i: (0, 0)),
                  pl.BlockSpec((2 * PEER_HEADS, PEER_N_KEYS, PEER_HALF), lambda i: (0, 0, 0))],
        out_specs=[pl.BlockSpec((hk, tm), lambda i: (0, i)),
                   pl.BlockSpec((hk, tm), lambda i: (0, i))],
        out_shape=[jax.ShapeDtypeStruct((hk, t), jnp.int32), jax.ShapeDtypeStruct((hk, t), jnp.float32)],
        compiler_params=_cparams(("arbitrary",)),
        name="peer_route",
    )(h2, wq_bf16, q_gain.reshape(2 * PEER_HEADS, PEER_HALF), sub_keys_bf16)


PEER_TB = 32
SUBLANES = 8


PEER_STAGES = 4
PEER_LOOKAHEAD = 3


def _expert_kernel(idx_ref, nidx_ref, h_ref, gate_ref, x1_ref, g2_ref, tab_ref, o_ref, *scratch):
    stages, sem = scratch[:PEER_STAGES], scratch[PEER_STAGES]
    tb, d = h_ref.shape
    k = PEER_TOPK
    hk = PEER_HEADS * k
    nr, nc = tb // SUBLANES, d // LANES
    step = pl.program_id(0)
    hi_mask = jnp.uint32(0xFFFF0000)

    def start_rows(ids, head, j, r):
        for s in range(SUBLANES):
            e = ids[0, 0, (r * SUBLANES + s) * hk + head * k + j]
            pltpu.make_async_copy(tab_ref.at[pl.ds(pl.multiple_of(e * nc, nc), nc), :],
                                  stages[head % PEER_STAGES].at[j, r, :, s, :],
                                  sem.at[head % PEER_STAGES]).start(priority=s % 2)

    def words(stage, j, r, c):
        return stage[j, r, c]

    def start_head(ids, head):
        for j in range(k):
            for r in range(nr):
                start_rows(ids, head, j, r)

    def wait_head(head):
        buf = stages[head % PEER_STAGES]
        pltpu.make_async_copy(buf, buf, sem.at[head % PEER_STAGES]).wait()

    @pl.when(step == 0)
    def _():
        for head in range(PEER_LOOKAHEAD):
            start_head(idx_ref, head)

    lane = lax.broadcasted_iota(jnp.int32, (SUBLANES, LANES), 1)
    for head in range(PEER_HEADS):
        stage = stages[head % PEER_STAGES]
        wait_head(head)
        ahead = head + PEER_LOOKAHEAD
        ahead_ids, ahead_head = (idx_ref, ahead) if ahead < PEER_HEADS else (nidx_ref, ahead - PEER_HEADS)
        acts = []
        for r in range(nr):
            rows = slice(r * SUBLANES, (r + 1) * SUBLANES)
            pre = jnp.zeros((SUBLANES, LANES), jnp.float32)
            for j in range(k):
                start_rows(ahead_ids, ahead_head, j, r)
                acc = None
                for c in range(nc):
                    u = lax.bitcast_convert_type(words(stage, j, r, c) & hi_mask, jnp.float32)
                    prod = h_ref[rows, c * LANES:(c + 1) * LANES] * u
                    acc = prod if acc is None else acc + prod
                pre = jnp.where(lane == head * k + j, jnp.sum(acc, axis=-1, keepdims=True), pre)
            acts.append(gate_ref[rows, :] * _gelu(pre))
        for r in range(nr):
            rows = slice(r * SUBLANES, (r + 1) * SUBLANES)
            accs = [None] * nc
            for j in range(k):
                a = jnp.broadcast_to(acts[r][:, head * k + j:head * k + j + 1], (SUBLANES, LANES))
                for c in range(nc):
                    term = a * lax.bitcast_convert_type(words(stage, j, r, c) << 16, jnp.float32)
                    accs[c] = term if accs[c] is None else accs[c] + term
            for c in range(nc):
                cs = slice(c * LANES, (c + 1) * LANES)
                if head == 0:
                    o_ref[rows, cs] = accs[c]
                else:
                    o_ref[rows, cs] = o_ref[rows, cs] + accs[c]

    @pl.when(step + 1 == pl.num_programs(0))
    def _():
        for head in range(PEER_LOOKAHEAD):
            wait_head(head)
    o_ref[...] = x1_ref[...] + g2_ref[0] * o_ref[...]


def _experts(idx_tok, gate_tok, h2, x1, g2, table, seq):
    t, d = h2.shape
    tb = PEER_TB
    hk = PEER_HEADS * PEER_TOPK
    per_batch = seq // tb
    nsteps = t // tb
    ids = idx_tok.reshape(nsteps, 1, tb * hk)
    stage = pltpu.VMEM((PEER_TOPK, tb // SUBLANES, d // LANES, SUBLANES, LANES), jnp.uint32)
    return pl.pallas_call(
        _expert_kernel,
        grid=(nsteps,),
        in_specs=[pl.BlockSpec((1, 1, tb * hk), lambda i: (i, 0, 0), memory_space=pltpu.SMEM),
                  pl.BlockSpec((1, 1, tb * hk), lambda i: (jnp.minimum(i + 1, nsteps - 1), 0, 0),
                               memory_space=pltpu.SMEM),
                  pl.BlockSpec((tb, d), lambda i: (i, 0)),
                  pl.BlockSpec((tb, hk), lambda i: (i, 0)),
                  pl.BlockSpec((tb, d), lambda i: (i, 0)),
                  pl.BlockSpec((1, 1, d), lambda i: (i // per_batch, 0, 0)),
                  pl.BlockSpec(memory_space=pl.ANY)],
        out_specs=pl.BlockSpec((tb, d), lambda i: (i, 0)),
        out_shape=jax.ShapeDtypeStruct((t, d), jnp.float32),
        scratch_shapes=[stage] * PEER_STAGES + [pltpu.SemaphoreType.DMA((PEER_STAGES,))],
        compiler_params=_cparams(("arbitrary",)),
        name="peer_experts",
    )(ids, ids, h2, gate_tok, x1, g2, table)


PACK_ROWS = 256


def _pack_kernel(u_ref, v_ref, o_ref):
    nc = u_ref.shape[1] // LANES
    as_bits = lambda x: lax.bitcast_convert_type(x.astype(jnp.bfloat16).astype(jnp.float32), jnp.uint32)
    w = as_bits(u_ref[...]) | (as_bits(v_ref[...]) >> 16)
    for c in range(nc):
        o_ref[pl.ds(c, PACK_ROWS, stride=nc), :] = w[:, c * LANES:(c + 1) * LANES]


def _pack_expert_table(u, v):
    e, d = u.shape
    nc = d // LANES
    return pl.pallas_call(
        _pack_kernel,
        grid=(e // PACK_ROWS,),
        in_specs=[pl.BlockSpec((PACK_ROWS, d), lambda i: (i, 0)),
                  pl.BlockSpec((PACK_ROWS, d), lambda i: (i, 0))],
        out_specs=pl.BlockSpec((PACK_ROWS * nc, LANES), lambda i: (i, 0)),
        out_shape=jax.ShapeDtypeStruct((e * nc, LANES), jnp.uint32),
        compiler_params=_cparams(("arbitrary",)),
        name="peer_pack_table",
    )(u, v)


def _proj_weights(w_in, nsa_q_gain, nsa_k_gain, dil_q_gain, dil_k_gain, d_model, lay):
    dh = HEAD_DIM
    off_kv = NSA_HEADS * dh
    off_gate = off_kv + 6 * NSA_KV_GROUPS * dh
    off_dil = off_gate + 3 * NSA_HEADS
    off_merge = off_dil + 3 * DIL_HEADS * dh
    n_cols = lay["n"] * LANES
    pad = n_cols - (lay["gate"] * LANES + 3 * NSA_HEADS)
    w = jnp.concatenate([w_in[:, off_merge:], w_in[:, :off_kv], w_in[:, off_kv:off_gate],
                         w_in[:, off_gate:off_dil], jnp.zeros((d_model, pad), w_in.dtype)],
                        axis=1).astype(jnp.bfloat16)
    w_dil = w_in[:, off_dil:off_merge].astype(jnp.bfloat16)
    ones = jnp.ones((dh,), jnp.float32)

    def rows(spec):
        gains = jnp.concatenate([g for g, _, n in spec for _ in range(n)])
        flags = jnp.concatenate([jnp.full((dh,), f, jnp.float32) for _, f, n in spec for _ in range(n)])
        return flags.reshape(1, -1), gains.reshape(1, -1)

    g = NSA_KV_GROUPS
    main = [(ones, 0.0, lay["q"]), (nsa_q_gain, 1.0, NSA_HEADS), (ones, 0.0, 2 * g),
            (nsa_k_gain[1], 1.0, g), (ones, 0.0, g), (nsa_k_gain[2], 1.0, g), (ones, 0.0, g),
            (ones, 0.0, lay["n"] - lay["gate"])]
    dil = [(dil_q_gain, 1.0, DIL_HEADS), (dil_k_gain, 1.0, DIL_HEADS), (ones, 0.0, DIL_HEADS)]
    return (w,) + rows(main), (w_dil,) + rows(dil)


def _token_mixer_and_norm2(x, mod, norm1_g, w_in, nsa_q_gain, nsa_k_gain, cmp_pos, cmp_w1, cmp_w2,
                           dil_q_gain, dil_k_gain, w_br_nsa, w_br_dil, w_out, norm2_g, rel_bias):
    b, seq, d = x.shape
    t = b * seq
    lay = _layout(d)
    sh1, sc1, g1, sh2, sc2, g2 = [m.reshape(b, 1, d) for m in jnp.split(mod, 6, axis=-1)]
    x2d = x.reshape(t, d)

    main_w, dil_w = _proj_weights(w_in, nsa_q_gain, nsa_k_gain, dil_q_gain, dil_k_gain, d, lay)
    blocks_per_tile = 4
    proj2d = _projection(x2d, norm1_g, sc1, sh1, *main_w, seq, jnp.bfloat16, (0, lay["q"] // blocks_per_tile))
    proj3 = proj2d.reshape(b, seq, lay["n"] * LANES)
    projd3 = _projection(x2d, norm1_g, sc1, sh1, *dil_w, seq, jnp.float32,
                         (2 * DIL_HEADS // blocks_per_tile, 3 * DIL_HEADS // blocks_per_tile)).reshape(b, seq, -1)

    rel_flat = rel_bias.reshape(-1)
    nchunk = seq // NSA_CMP_STRIDE
    ns = seq // NSA_SEL_LEN
    bias_win = _bias_table(rel_flat, _band_buckets(-(-(NSA_WINDOW - 1) // QT), NSA_WINDOW - 1, 1), 0, NSA_HEADS)
    bias_sel = _bias_table(rel_flat, _causal_buckets(_n_causal_tiles(seq)), 0, NSA_HEADS)
    bias_cmp = _bias_table(rel_flat, _cmp_buckets(seq, nchunk), 0, NSA_HEADS)
    bias_dil = [_bias_table(rel_flat, _band_buckets(-(-(wnd // dil) // QT), wnd // dil, dil),
                            NSA_HEADS + gi * DIL_HEADS_PER_GROUP, DIL_HEADS_PER_GROUP)
                for gi, (wnd, dil) in enumerate(DIL_CONFIGS)]

    g = NSA_KV_GROUPS
    c0 = lay["kv"] * LANES
    tc = proj3[:, :, c0:c0 + 2 * g * HEAD_DIM].reshape(b, nchunk, NSA_CMP_STRIDE, 2 * g, HEAD_DIM)
    tc = tc.transpose(0, 3, 1, 2, 4).reshape(b, 2 * g, nchunk, NSA_CMP_STRIDE * HEAD_DIM)
    kcvc = _compress(tc, cmp_pos, cmp_w1, cmp_w2, nsa_k_gain[0])

    cstart = np.arange(nchunk) * NSA_CMP_STRIDE
    sstart = np.arange(ns) * NSA_SEL_LEN
    n_cmp = (seq - NSA_CMP_LEN) // NSA_CMP_STRIDE + 1
    overlap = ((cstart[:, None] < sstart[None, :] + NSA_SEL_LEN) & (cstart[:, None] + NSA_CMP_LEN > sstart[None, :])
               & (np.arange(nchunk)[:, None] < n_cmp)).astype(np.float32)
    expand_t = (np.arange(seq)[:, None] // NSA_SEL_LEN == np.arange(ns)[None, :]).astype(np.float32)

    ocmp, sel = _cmp_select(proj3, kcvc, bias_cmp, jnp.asarray(overlap.T), lay, seq)
    osel = _sel_attention(proj3, sel, jnp.asarray(expand_t, jnp.bfloat16), bias_sel, lay, seq)
    owin = _win_attention(proj3, bias_win, lay, seq)
    odil, lsedil = zip(*[_dil_attention(projd3, bias_dil[gi], seq, gi) for gi in range(len(DIL_CONFIGS))])

    x1, h2 = _post(ocmp.reshape(t, -1), osel.reshape(t, -1), owin.reshape(t, -1), proj2d,
                   [o.reshape(t, -1) for o in odil], [l.reshape(t, -1) for l in lsedil],
                   x2d, g1, sc2, sh2, norm2_g, w_br_nsa.astype(jnp.bfloat16), w_br_dil.astype(jnp.bfloat16),
                   w_out.astype(jnp.bfloat16), lay, seq)
    return x1, h2, g2


def kernel(x, c, w_ada, b_ada, norm1_g, w_in, nsa_q_gain, nsa_k_gain, cmp_pos, cmp_w1, cmp_w2,
           dil_q_gain, dil_k_gain, w_br_nsa, w_br_dil, w_out, norm2_g, peer_w_q, peer_q_gain,
           peer_sub_keys, peer_u, peer_v, rel_bias):
    b, seq, d = x.shape
    depth = w_ada.shape[0]
    for layer in range(depth):
        mod = _modulation(c, w_ada[layer], b_ada[layer])
        x1, h2, g2 = _token_mixer_and_norm2(
            x, mod, norm1_g[layer], w_in[layer], nsa_q_gain[layer], nsa_k_gain[layer], cmp_pos[layer],
            cmp_w1[layer], cmp_w2[layer], dil_q_gain[layer], dil_k_gain[layer], w_br_nsa[layer],
            w_br_dil[layer], w_out[layer], norm2_g[layer], rel_bias)
        idx, gate = _route(h2, peer_w_q[layer].astype(jnp.bfloat16), peer_q_gain[layer],
                           peer_sub_keys[layer].reshape(2 * PEER_HEADS, PEER_N_KEYS, PEER_HALF).astype(jnp.bfloat16))
        table = _pack_expert_table(peer_u[layer], peer_v[layer])
        out = _experts(idx.T, gate.T, h2, x1, g2, table, seq)
        x = out.reshape(b, seq, d)
    return x
```

```python
import functools
import math

import numpy as np
import jax
import jax.numpy as jnp
from jax import lax
from jax.experimental import pallas as pl
from jax.experimental.pallas import tpu as pltpu

HEAD_DIM = 128
LANES = 128
NSA_HEADS = 8
NSA_KV_GROUPS = 2
NSA_REP = NSA_HEADS // NSA_KV_GROUPS
NSA_CMP_LEN = 32
NSA_CMP_STRIDE = 16
NSA_SEL_LEN = 64
NSA_TOP_N = 16
NSA_WINDOW = 512
DIL_CONFIGS = ((128, 1), (512, 4), (2048, 16))
DIL_HEADS_PER_GROUP = 4
DIL_HEADS = DIL_HEADS_PER_GROUP * len(DIL_CONFIGS)
REL_BUCKETS = 32
REL_MAX_DIST = 2048
PEER_HEADS = 8
PEER_N_KEYS = 128
PEER_TOPK = 16
PEER_HALF = 128
EPS = 1e-6
NEG = -1e30
SEL_FORCE = 1e4
LOG2E = math.log2(math.e)
LN2 = math.log(2.0)
QK_SCALE = HEAD_DIM ** -0.5 * LOG2E
MASKED_BUCKET = REL_BUCKETS

QT = 128
VMEM_LIMIT = 48 * 1024 * 1024

CB_MERGE = 0
def _layout(d_model):
    nmerge = 2 * d_model // LANES
    cb_q = nmerge
    cb_kv = cb_q + NSA_HEADS
    cb_gate = cb_kv + 6 * NSA_KV_GROUPS
    n_used = cb_gate + 1
    n_blocks = -(-n_used // 4) * 4
    return dict(q=cb_q, kv=cb_kv, gate=cb_gate, n=n_blocks)


def _cparams(sem):
    return pltpu.CompilerParams(dimension_semantics=sem, vmem_limit_bytes=VMEM_LIMIT)


def _t5_bucket_np(dist):
    n = np.maximum(dist, 0)
    max_exact = REL_BUCKETS // 2
    nf = np.maximum(n, max_exact).astype(np.float32)
    large = max_exact + (np.log(nf / np.float32(max_exact)) / np.float32(math.log(REL_MAX_DIST / max_exact))
                         * np.float32(REL_BUCKETS - max_exact)).astype(np.int32)
    large = np.minimum(large, REL_BUCKETS - 1)
    return np.where(n < max_exact, n, large).astype(np.int32)


def _gelu(x):
    return jax.nn.gelu(x)


def _sigmoid(x):
    return 1.0 / (1.0 + jnp.exp(-x))


def _mod_kernel(c_ref, w_ref, b_ref, o_ref):
    c = c_ref[...]
    cond = c * _sigmoid(c)
    o_ref[...] = jnp.dot(cond.astype(jnp.bfloat16), w_ref[...].astype(jnp.bfloat16),
                         preferred_element_type=jnp.float32) + b_ref[...]


def _modulation(c, w_ada, b_ada):
    b, d = c.shape
    n = w_ada.shape[1]
    rows = 8
    cp = jnp.zeros((rows, d), jnp.float32).at[:b].set(c)
    tn = 512
    out = pl.pallas_call(
        _mod_kernel,
        grid=(n // tn,),
        in_specs=[pl.BlockSpec((rows, d), lambda j: (0, 0)),
                  pl.BlockSpec((d, tn), lambda j: (0, j)),
                  pl.BlockSpec((1, tn), lambda j: (0, j))],
        out_specs=pl.BlockSpec((rows, tn), lambda j: (0, j)),
        out_shape=jax.ShapeDtypeStruct((rows, n), jnp.float32),
        compiler_params=_cparams(("arbitrary",)),
        name="adaln_mod",
    )(cp, w_ada, b_ada.reshape(1, n))
    return out[:b]


def _proj_kernel(x_ref, g_ref, sc_ref, sh_ref, w_ref, flag_ref, gain_ref, o_ref, h_ref, *, plain):
    j = pl.program_id(1)

    @pl.when(j == 0)
    def _():
        x = x_ref[...]
        y = x * lax.rsqrt(jnp.mean(x * x, axis=-1, keepdims=True) + EPS) * g_ref[...]
        h_ref[...] = (y * (1.0 + sc_ref[0]) + sh_ref[0]).astype(jnp.bfloat16)

    acc = jnp.dot(h_ref[...], w_ref[...], preferred_element_type=jnp.float32)
    is_plain = (j >= plain[0]) & (j < plain[1])

    @pl.when(is_plain)
    def _():
        o_ref[...] = acc.astype(o_ref.dtype)

    @pl.when(jnp.logical_not(is_plain))
    def _():
        for k in range(acc.shape[1] // LANES):
            a = acc[:, k * LANES:(k + 1) * LANES]
            f = flag_ref[:, k * LANES:(k + 1) * LANES]
            r = lax.rsqrt(jnp.mean(a * a, axis=-1, keepdims=True) + EPS)
            scale = f * r + (1.0 - f)
            o_ref[:, k * LANES:(k + 1) * LANES] = (a * scale * gain_ref[:, k * LANES:(k + 1) * LANES]).astype(o_ref.dtype)


def _projection(x2d, norm_g, sc, sh, w_bf16, flags, gains, seq, out_dtype, plain):
    t, d = x2d.shape
    n = w_bf16.shape[1]
    tm = min(1024, seq)
    tn = 512
    per_batch = seq // tm
    return pl.pallas_call(
        functools.partial(_proj_kernel, plain=plain),
        grid=(t // tm, n // tn),
        in_specs=[pl.BlockSpec((tm, d), lambda i, j: (i, 0)),
                  pl.BlockSpec((1, d), lambda i, j: (0, 0)),
                  pl.BlockSpec((1, 1, d), lambda i, j: (i // per_batch, 0, 0)),
                  pl.BlockSpec((1, 1, d), lambda i, j: (i // per_batch, 0, 0)),
                  pl.BlockSpec((d, tn), lambda i, j: (0, j)),
                  pl.BlockSpec((1, tn), lambda i, j: (0, j)),
                  pl.BlockSpec((1, tn), lambda i, j: (0, j))],
        out_specs=pl.BlockSpec((tm, tn), lambda i, j: (i, j)),
        out_shape=jax.ShapeDtypeStruct((t, n), out_dtype),
        scratch_shapes=[pltpu.VMEM((tm, d), jnp.bfloat16)],
        compiler_params=_cparams(("arbitrary", "arbitrary")),
        name="norm_in_proj",
    )(x2d, norm_g.reshape(1, d), sc, sh, w_bf16, flags, gains)


def _bias_kernel(tbl_ref, idx_ref, o_ref, *, head0):
    h = pl.program_id(0) + head0
    b = idx_ref[0]
    out = jnp.full(b.shape, NEG, jnp.float32)
    for k in range(REL_BUCKETS):
        out = jnp.where(b == k, tbl_ref[h * REL_BUCKETS + k], out)
    o_ref[0, 0] = out


def _bias_table(rel_bias_flat, bucket_idx, head0, n_heads):
    nt, r, c = bucket_idx.shape
    return pl.pallas_call(
        functools.partial(_bias_kernel, head0=head0),
        grid=(n_heads, nt),
        in_specs=[pl.BlockSpec(memory_space=pltpu.SMEM),
                  pl.BlockSpec((1, r, c), lambda h, t: (t, 0, 0))],
        out_specs=pl.BlockSpec((1, 1, r, c), lambda h, t: (h, t, 0, 0)),
        out_shape=jax.ShapeDtypeStruct((n_heads, nt, r, c), jnp.float32),
        compiler_params=_cparams(("arbitrary", "arbitrary")),
        name="rel_bias_table",
    )(rel_bias_flat, jnp.asarray(bucket_idx))


def _band_buckets(n_prev, span, dist_scale):
    i = np.arange(QT)[:, None]
    c = np.arange((n_prev + 1) * QT)[None, :]
    dist = n_prev * QT + i - c
    ok = (dist >= 0) & (dist <= span)
    return np.where(ok, _t5_bucket_np(dist * dist_scale), MASKED_BUCKET).astype(np.int32)[None]


def _causal_buckets(n_tiles):
    i = np.arange(QT)[None, :]
    j = np.arange(QT)[:, None]
    out = []
    for o in range(n_tiles):
        dist = QT * o + i - j
        out.append(np.where(dist >= 0, _t5_bucket_np(dist), MASKED_BUCKET))
    return np.stack(out).astype(np.int32)


def _n_causal_tiles(seq):
    first_last = int(np.argmax(_t5_bucket_np(np.arange(4 * REL_MAX_DIST)) == REL_BUCKETS - 1))
    o_const = -(-(first_last + QT - 1) // QT)
    return min(seq // QT, o_const + 1)


def _cmp_buckets(seq, n_cmp_pad):
    nq = seq // QT
    n_cmp = (seq - NSA_CMP_LEN) // NSA_CMP_STRIDE + 1
    i = np.arange(QT)[None, :, None]
    n = np.arange(nq)[:, None, None]
    c = np.arange(n_cmp_pad)[None, None, :]
    dist = n * QT + i - (c * NSA_CMP_STRIDE + NSA_CMP_LEN - 1)
    ok = (dist >= 0) & (c < n_cmp)
    return np.where(ok, _t5_bucket_np(dist), MASKED_BUCKET).astype(np.int32)


def _compress_kernel(t_ref, pos_ref, w1_ref, w2_ref, gain_ref, o_ref):
    half = t_ref.shape[3]
    a = t_ref[0, 0].astype(jnp.float32)
    lo = (a + pos_ref[0, :, :half]).astype(jnp.bfloat16)
    hi = (a + pos_ref[0, :, half:]).astype(jnp.bfloat16)
    p = jnp.dot(lo, w1_ref[0, :half, :], preferred_element_type=jnp.float32)
    q = jnp.dot(hi, w1_ref[0, half:, :], preferred_element_type=jnp.float32)
    n = q.shape[0]
    q_next = jnp.concatenate([q[1:], q[:1]], axis=0)
    hid = _gelu(p + q_next)
    out = jnp.dot(hid.astype(jnp.bfloat16), w2_ref[0], preferred_element_type=jnp.float32)
    is_key = pl.program_id(1) < NSA_KV_GROUPS
    r = lax.rsqrt(jnp.mean(out * out, axis=-1, keepdims=True) + EPS)
    normed = out * r * gain_ref[...]
    o_ref[0, 0] = jnp.where(is_key, normed, out).astype(o_ref.dtype)


def _compress(t_chunks, cmp_pos, cmp_w1, cmp_w2, k_gain0):
    b, nh, nchunk, width = t_chunks.shape
    dh = HEAD_DIM
    g = NSA_KV_GROUPS
    return pl.pallas_call(
        _compress_kernel,
        grid=(b, nh),
        in_specs=[pl.BlockSpec((1, 1, nchunk, width), lambda i, h: (i, h, 0, 0)),
                  pl.BlockSpec((1, 1, 2 * width), lambda i, h: (h // g, 0, 0)),
                  pl.BlockSpec((1, 2 * width, dh), lambda i, h: (h // g, 0, 0)),
                  pl.BlockSpec((1, dh, dh), lambda i, h: (h // g, 0, 0)),
                  pl.BlockSpec((1, dh), lambda i, h: (0, 0))],
        out_specs=pl.BlockSpec((1, 1, nchunk, dh), lambda i, h: (i, h, 0, 0)),
        out_shape=jax.ShapeDtypeStruct((b, nh, nchunk, dh), jnp.bfloat16),
        compiler_params=_cparams(("arbitrary", "arbitrary")),
        name="nsa_compress",
    )(t_chunks, cmp_pos.reshape(2, 1, 2 * width), cmp_w1.astype(jnp.bfloat16),
      cmp_w2.astype(jnp.bfloat16), k_gain0.reshape(1, dh))


def _stack_heads(q):
    return jnp.concatenate([q[:, r * HEAD_DIM:(r + 1) * HEAD_DIM] for r in range(NSA_REP)], axis=0)


def _cmp_select_kernel(q_ref, kc_ref, vc_ref, bias_ref, ov_ref, o_ref, sel_ref, *, n_sel):
    tq = q_ref.shape[1]
    n = pl.program_id(2)
    qs = _stack_heads(q_ref[0])
    s = lax.dot_general(qs, kc_ref[0, 0], (((1,), (1,)), ((), ())),
                        preferred_element_type=jnp.float32)
    bias = jnp.concatenate([bias_ref[r, 0] for r in range(NSA_REP)], axis=0)
    valid = bias > 0.5 * NEG
    s = jnp.where(valid, s + bias, NEG)
    m = jnp.max(s, axis=-1, keepdims=True)
    e = jnp.exp2(s - m)
    p = e / jnp.sum(e, axis=-1, keepdims=True) * valid.astype(jnp.float32)
    o = jnp.dot(p.astype(jnp.bfloat16), vc_ref[0, 0], preferred_element_type=jnp.float32)
    for r in range(NSA_REP):
        o_ref[0, :, r * HEAD_DIM:(r + 1) * HEAD_DIM] = o[r * tq:(r + 1) * tq].astype(o_ref.dtype)

    psum = p[0:tq]
    for r in range(1, NSA_REP):
        psum = psum + p[r * tq:(r + 1) * tq]
    imp = lax.dot_general(ov_ref[...], psum, (((1,), (1,)), ((), ())), preferred_element_type=jnp.float32,
                          precision=lax.Precision.HIGHEST)
    n_blocks = sel_ref.shape[2]
    jb = lax.broadcasted_iota(jnp.int32, (n_blocks, tq), 0).astype(jnp.float32)
    qpos = lax.broadcasted_iota(jnp.int32, (n_blocks, tq), 1)
    cur = ((n * tq + qpos) // NSA_SEL_LEN).astype(jnp.float32)
    forced = (jb == 0.0) | (jb == cur) | (jb == cur - 1.0)
    score = jnp.where(forced, SEL_FORCE, jnp.where(jb > cur, -SEL_FORCE, imp))
    chosen = jnp.zeros((n_blocks, tq), jnp.float32)
    for _ in range(n_sel):
        mx = jnp.max(score, axis=0, keepdims=True)
        first = jnp.min(jnp.where(score == mx, jb, float(n_blocks)), axis=0, keepdims=True)
        hit = jb == first
        chosen = jnp.where(hit, 1.0, chosen)
        score = jnp.where(hit, -jnp.inf, score)
    sel_ref[0, 0] = chosen


def _cmp_select(proj3, kcvc, cbias, overlap, lay, seq):
    b = proj3.shape[0]
    g = NSA_KV_GROUPS
    ncp = kcvc.shape[2]
    ns = seq // NSA_SEL_LEN
    n_sel = min(NSA_TOP_N, ns)
    qw = NSA_REP * HEAD_DIM
    return pl.pallas_call(
        functools.partial(_cmp_select_kernel, n_sel=n_sel),
        grid=(b, g, seq // QT),
        in_specs=[pl.BlockSpec((1, QT, qw), lambda i, j, n: (i, n, lay["q"] // NSA_REP + j)),
                  pl.BlockSpec((1, 1, ncp, HEAD_DIM), lambda i, j, n: (i, j, 0, 0)),
                  pl.BlockSpec((1, 1, ncp, HEAD_DIM), lambda i, j, n: (i, g + j, 0, 0)),
                  pl.BlockSpec((NSA_REP, 1, QT, ncp), lambda i, j, n: (j, n, 0, 0)),
                  pl.BlockSpec((ns, ncp), lambda i, j, n: (0, 0))],
        out_specs=[pl.BlockSpec((1, QT, qw), lambda i, j, n: (i, n, j)),
                   pl.BlockSpec((1, 1, ns, QT), lambda i, j, n: (i, j, 0, n))],
        out_shape=[jax.ShapeDtypeStruct((b, seq, NSA_HEADS * HEAD_DIM), jnp.bfloat16),
                   jax.ShapeDtypeStruct((b, g, ns, seq), jnp.float32)],
        compiler_params=_cparams(("arbitrary", "arbitrary", "arbitrary")),
        name="nsa_cmp_select",
    )(proj3, kcvc, kcvc, cbias, overlap)


def _sel_attn_kernel(q_ref, k_ref, vt_ref, sel_ref, exp_ref, bias_ref, o_ref, madd_ref):
    tq = q_ref.shape[1]
    n = pl.program_id(2)
    n_bias = bias_ref.shape[1]
    q = q_ref[0].astype(jnp.float32)
    qt = jnp.concatenate([q[:, r * HEAD_DIM:(r + 1) * HEAD_DIM].T for r in range(NSA_REP)],
                         axis=1).astype(jnp.bfloat16)
    km = jnp.dot(exp_ref[...], sel_ref[0, 0].astype(jnp.bfloat16), preferred_element_type=jnp.float32)
    madd_ref[...] = (km - 1.0) * (-NEG)

    last_tile = k_ref.shape[1] // QT - 1

    def tile_start(kt):
        return pl.multiple_of(jnp.minimum(kt, last_tile) * QT, QT)

    def logits(kt):
        start = tile_start(kt)
        s = jnp.dot(k_ref[0, pl.ds(start, QT), :], qt, preferred_element_type=jnp.float32)
        o = n - kt
        ob = jnp.clip(o, 0, n_bias - 1)
        bias = jnp.concatenate([bias_ref[r, ob] for r in range(NSA_REP)], axis=1)
        ma = jnp.where(o >= 0, madd_ref[pl.ds(start, QT), :], NEG)
        return s + bias + jnp.concatenate([ma] * NSA_REP, axis=1)

    def body(kp, carry):
        m, l, acc, s0, s1 = carry
        n0, n1 = logits(2 * kp + 2), logits(2 * kp + 3)
        m_new = jnp.maximum(m, jnp.maximum(jnp.max(s0, axis=0, keepdims=True),
                                           jnp.max(s1, axis=0, keepdims=True)))
        alpha = jnp.exp2(m - m_new)
        p0 = jnp.exp2(s0 - m_new)
        p1 = jnp.exp2(s1 - m_new)
        l = alpha * l + jnp.sum(p0, axis=0, keepdims=True) + jnp.sum(p1, axis=0, keepdims=True)
        acc = (alpha * acc
               + jnp.dot(vt_ref[0, 0, :, pl.ds(tile_start(2 * kp), QT)], p0.astype(jnp.bfloat16),
                         preferred_element_type=jnp.float32)
               + jnp.dot(vt_ref[0, 0, :, pl.ds(tile_start(2 * kp + 1), QT)], p1.astype(jnp.bfloat16),
                         preferred_element_type=jnp.float32))
        return m_new, l, acc, n0, n1

    cols = NSA_REP * tq
    init = (jnp.full((1, cols), NEG, jnp.float32), jnp.zeros((1, cols), jnp.float32),
            jnp.zeros((HEAD_DIM, cols), jnp.float32), logits(0), logits(1))
    m, l, acc, _, _ = lax.fori_loop(0, n // 2 + 1, body, init)
    out = acc / l
    for r in range(NSA_REP):
        o_ref[0, :, r * HEAD_DIM:(r + 1) * HEAD_DIM] = out[:, r * tq:(r + 1) * tq].T.astype(o_ref.dtype)


def _sel_attention(proj3, sel, expand_t, bias_sel_t, lay, seq):
    b = proj3.shape[0]
    g = NSA_KV_GROUPS
    ns = seq // NSA_SEL_LEN
    qw = NSA_REP * HEAD_DIM
    nb = bias_sel_t.shape[1]
    cb_k = lay["kv"] + 2 * g
    cb_v = lay["kv"] + 3 * g
    vt = proj3[:, :, cb_v * LANES:(cb_v + g) * LANES].reshape(b, seq, g, HEAD_DIM).transpose(0, 2, 3, 1)
    return pl.pallas_call(
        _sel_attn_kernel,
        grid=(b, g, seq // QT),
        in_specs=[pl.BlockSpec((1, QT, qw), lambda i, j, n: (i, n, lay["q"] // NSA_REP + j)),
                  pl.BlockSpec((1, seq, HEAD_DIM), lambda i, j, n: (i, 0, cb_k + j)),
                  pl.BlockSpec((1, 1, HEAD_DIM, seq), lambda i, j, n: (i, j, 0, 0)),
                  pl.BlockSpec((1, 1, ns, QT), lambda i, j, n: (i, j, 0, n)),
                  pl.BlockSpec((seq, ns), lambda i, j, n: (0, 0)),
                  pl.BlockSpec((NSA_REP, nb, QT, QT), lambda i, j, n: (j, 0, 0, 0))],
        out_specs=pl.BlockSpec((1, QT, qw), lambda i, j, n: (i, n, j)),
        out_shape=jax.ShapeDtypeStruct((b, seq, NSA_HEADS * HEAD_DIM), jnp.bfloat16),
        scratch_shapes=[pltpu.VMEM((seq, QT), jnp.float32)],
        compiler_params=_cparams(("arbitrary", "arbitrary", "arbitrary")),
        name="nsa_selected",
    )(proj3, proj3, vt, sel, expand_t, bias_sel_t)


def _band_softmax(qs, k_tile, v_tile, bias, n, n_prev):
    logits = []
    for j in range(n_prev + 1):
        kb = n - n_prev + j
        k = k_tile(jnp.maximum(kb, 0))
        s = lax.dot_general(qs, k, (((1,), (1,)), ((), ())), preferred_element_type=jnp.float32)
        bj = jnp.where(kb >= 0, bias[:, j * QT:(j + 1) * QT], NEG)
        logits.append(jnp.where(bj > 0.5 * NEG, s + bj, NEG))
    m = logits[0].max(axis=-1, keepdims=True)
    for s in logits[1:]:
        m = jnp.maximum(m, s.max(axis=-1, keepdims=True))
    l = jnp.zeros_like(m)
    acc = jnp.zeros((qs.shape[0], HEAD_DIM), jnp.float32)
    for j, s in enumerate(logits):
        p = jnp.exp2(s - m)
        l = l + jnp.sum(p, axis=-1, keepdims=True)
        acc = acc + jnp.dot(p.astype(jnp.bfloat16), v_tile(jnp.maximum(n - n_prev + j, 0)),
                            preferred_element_type=jnp.float32)
    return acc, m, l


def _win_attn_kernel(q_ref, k_ref, v_ref, bias_ref, o_ref, *, n_prev):
    tq = q_ref.shape[1]
    n = pl.program_id(2)
    qs = _stack_heads(q_ref[0])
    bias = jnp.concatenate([bias_ref[r, 0] for r in range(NSA_REP)], axis=0)
    tile = lambda ref: (lambda kb: ref[0, pl.ds(pl.multiple_of(kb * QT, QT), QT), :])
    acc, _, l = _band_softmax(qs, tile(k_ref), tile(v_ref), bias, n, n_prev)
    o = acc / l
    for r in range(NSA_REP):
        o_ref[0, :, r * HEAD_DIM:(r + 1) * HEAD_DIM] = o[r * tq:(r + 1) * tq].astype(o_ref.dtype)


def _win_attention(proj3, bias_win, lay, seq):
    b = proj3.shape[0]
    g = NSA_KV_GROUPS
    qw = NSA_REP * HEAD_DIM
    n_prev = -(-(NSA_WINDOW - 1) // QT)
    cb_k = lay["kv"] + 4 * g
    cb_v = lay["kv"] + 5 * g
    return pl.pallas_call(
        functools.partial(_win_attn_kernel, n_prev=n_prev),
        grid=(b, g, seq // QT),
        in_specs=[pl.BlockSpec((1, QT, qw), lambda i, j, n: (i, n, lay["q"] // NSA_REP + j)),
                  pl.BlockSpec((1, seq, HEAD_DIM), lambda i, j, n: (i, 0, cb_k + j)),
                  pl.BlockSpec((1, seq, HEAD_DIM), lambda i, j, n: (i, 0, cb_v + j)),
                  pl.BlockSpec((NSA_REP, 1, QT, (n_prev + 1) * QT), lambda i, j, n: (j, 0, 0, 0))],
        out_specs=pl.BlockSpec((1, QT, qw), lambda i, j, n: (i, n, j)),
        out_shape=jax.ShapeDtypeStruct((b, seq, NSA_HEADS * HEAD_DIM), jnp.bfloat16),
        compiler_params=_cparams(("arbitrary", "arbitrary", "arbitrary")),
        name="nsa_window",
    )(proj3, proj3, proj3, bias_win)


DIL_TILES_PER_STEP = 4


def _dil_attn_kernel(q_ref, k_ref, v_ref, bias_ref, o_ref, lse_ref, *, n_prev, dil):
    tiles = q_ref.shape[1] // (QT * dil)
    for sub in range(DIL_TILES_PER_STEP):
        t = pl.program_id(2) * DIL_TILES_PER_STEP + sub
        r = t // tiles
        n = t % tiles
        rows = lambda kb, r=r: pl.ds(kb * (QT * dil) + r, QT, stride=dil) if dil > 1 else pl.ds(
            pl.multiple_of(kb * QT, QT), QT)
        tile = lambda ref, rows=rows: (lambda kb: ref[0, rows(kb), :].astype(jnp.bfloat16))
        acc, m, l = _band_softmax(tile(q_ref)(n), tile(k_ref), tile(v_ref), bias_ref[0, 0], n, n_prev)
        o_ref[0, rows(n), :] = acc / l
        lse_ref[0, rows(n), :] = jnp.broadcast_to(m * LN2 + jnp.log(l), (QT, HEAD_DIM))


def _dil_attention(projd3, bias_dil, seq, gi):
    b = projd3.shape[0]
    window, dil = DIL_CONFIGS[gi]
    p_heads = DIL_HEADS_PER_GROUP
    n_prev = -(-(window // dil) // QT)
    cb_q = gi * p_heads
    cb_k = cb_q + DIL_HEADS
    cb_v = cb_k + DIL_HEADS
    out_w = p_heads * HEAD_DIM
    whole = lambda cb: pl.BlockSpec((1, seq, HEAD_DIM), lambda i, p, t: (i, 0, cb + p))
    return pl.pallas_call(
        functools.partial(_dil_attn_kernel, n_prev=n_prev, dil=dil),
        grid=(b, p_heads, seq // (QT * DIL_TILES_PER_STEP)),
        in_specs=[whole(cb_q), whole(cb_k), whole(cb_v),
                  pl.BlockSpec((1, 1, QT, (n_prev + 1) * QT), lambda i, p, t: (p, 0, 0, 0))],
        out_specs=[whole(0), whole(0)],
        out_shape=[jax.ShapeDtypeStruct((b, seq, out_w), jnp.float32),
                   jax.ShapeDtypeStruct((b, seq, out_w), jnp.float32)],
        compiler_params=_cparams(("arbitrary",) * 3),
        name=f"dilated_attn_{gi}",
    )(projd3, projd3, projd3, bias_dil)


def _post_kernel(ocmp_ref, osel_ref, owin_ref, gate_ref, od0_ref, od1_ref, od2_ref, l0_ref, l1_ref, l2_ref,
                 mg1_ref, mg2_ref, x_ref, g1_ref, sc2_ref, sh2_ref, n2_ref, wn_ref, wd_ref, wo_ref,
                 x1_ref, h2_ref):
    gates = _sigmoid(gate_ref[...].astype(jnp.float32))
    parts = []
    for h in range(NSA_HEADS):
        sl = slice(h * HEAD_DIM, (h + 1) * HEAD_DIM)
        y = (gates[:, 3 * h:3 * h + 1] * ocmp_ref[:, sl].astype(jnp.float32)
             + gates[:, 3 * h + 1:3 * h + 2] * osel_ref[:, sl].astype(jnp.float32)
             + gates[:, 3 * h + 2:3 * h + 3] * owin_ref[:, sl].astype(jnp.float32))
        parts.append(y.astype(jnp.bfloat16))
    y_nsa = jnp.concatenate(parts, axis=1)

    l0, l1, l2 = l0_ref[...], l1_ref[...], l2_ref[...]
    mx = jnp.maximum(jnp.maximum(l0, l1), l2)
    e0, e1, e2 = jnp.exp(l0 - mx), jnp.exp(l1 - mx), jnp.exp(l2 - mx)
    den = e0 + e1 + e2
    y_dil = ((e0 / den) * od0_ref[...].astype(jnp.float32) + (e1 / den) * od1_ref[...].astype(jnp.float32)
             + (e2 / den) * od2_ref[...].astype(jnp.float32)).astype(jnp.bfloat16)

    a = jnp.dot(y_nsa, wn_ref[...], preferred_element_type=jnp.float32)
    bb = jnp.dot(y_dil, wd_ref[...], preferred_element_type=jnp.float32)
    merged = (_sigmoid(mg1_ref[...].astype(jnp.float32)) * a
              + _sigmoid(mg2_ref[...].astype(jnp.float32)) * bb).astype(jnp.bfloat16)
    mix = jnp.dot(merged, wo_ref[...], preferred_element_type=jnp.float32)
    x1 = x_ref[...] + g1_ref[0] * mix
    x1_ref[...] = x1
    y = x1 * lax.rsqrt(jnp.mean(x1 * x1, axis=-1, keepdims=True) + EPS) * n2_ref[...]
    h2_ref[...] = y * (1.0 + sc2_ref[0]) + sh2_ref[0]


def _post(ocmp, osel, owin, proj2d, odil, lsedil, x2d, g1, sc2, sh2, norm2_g, wn, wd, wo, lay, seq):
    t, d = x2d.shape
    tm = 256
    per_batch = seq // tm
    nw = NSA_HEADS * HEAD_DIM
    dw = DIL_HEADS_PER_GROUP * HEAD_DIM
    row = lambda w: pl.BlockSpec((tm, w), lambda i: (i, 0))
    mod = pl.BlockSpec((1, 1, d), lambda i: (i // per_batch, 0, 0))
    full = lambda a: pl.BlockSpec(a.shape, lambda i: (0, 0))
    return pl.pallas_call(
        _post_kernel,
        grid=(t // tm,),
        in_specs=[row(nw), row(nw), row(nw),
                  pl.BlockSpec((tm, LANES), lambda i: (i, lay["gate"])),
                  row(dw), row(dw), row(dw), row(dw), row(dw), row(dw),
                  pl.BlockSpec((tm, d), lambda i: (i, 0)),
                  pl.BlockSpec((tm, d), lambda i: (i, 1)),
                  row(d), mod, mod, mod,
                  pl.BlockSpec((1, d), lambda i: (0, 0)),
                  full(wn), full(wd), full(wo)],
        out_specs=[row(d), row(d)],
        out_shape=[jax.ShapeDtypeStruct((t, d), jnp.float32), jax.ShapeDtypeStruct((t, d), jnp.float32)],
        compiler_params=_cparams(("arbitrary",)),
        name="merge_out_proj",
    )(ocmp, osel, owin, proj2d, odil[0], odil[1], odil[2], lsedil[0], lsedil[1], lsedil[2],
      proj2d, proj2d, x2d, g1, sc2, sh2, norm2_g.reshape(1, d), wn, wd, wo)


def _topk_rows(sc, k, payload=None):
    n = sc.shape[0]
    row = lax.broadcasted_iota(jnp.int32, sc.shape, 0).astype(jnp.float32)
    vals, picks = [], []
    for _ in range(k):
        mx = jnp.max(sc, axis=0, keepdims=True)
        first = jnp.min(jnp.where(sc == mx, row, float(n)), axis=0, keepdims=True)
        hit = row == first
        vals.append(mx)
        if payload is None:
            picks.append(first)
        else:
            picks.append(jnp.sum(jnp.where(hit, payload, 0.0), axis=0, keepdims=True))
        sc = jnp.where(hit, -jnp.inf, sc)
    return jnp.concatenate(vals, axis=0), jnp.concatenate(picks, axis=0)


def _topk_rows_paired(sc, k):
    n = sc.shape[0]
    half = n // 2
    row = lax.broadcasted_iota(jnp.int32, (half, sc.shape[1]), 0).astype(jnp.float32)
    a, b = sc[:half], sc[half:]
    a_first = a >= b
    hi, lo = jnp.where(a_first, a, b), jnp.where(a_first, b, a)
    hi_id = jnp.where(a_first, row, row + float(half))
    lo_id = jnp.where(a_first, row + float(half), row)
    vals, picks = [], []
    for _ in range(k):
        mx = jnp.max(hi, axis=0, keepdims=True)
        first = jnp.min(jnp.where(hi == mx, hi_id, float(n)), axis=0, keepdims=True)
        hit = hi_id == first
        vals.append(mx)
        picks.append(first)
        hi = jnp.where(hit, lo, hi)
        hi_id = jnp.where(hit, lo_id, hi_id)
        lo = jnp.where(hit, -jnp.inf, lo)
    return jnp.concatenate(vals, axis=0), jnp.concatenate(picks, axis=0)


def _route_kernel(h_ref, wq_ref, gain_ref, keys_ref, idx_ref, gate_ref):
    q = jnp.dot(h_ref[...].astype(jnp.bfloat16), wq_ref[...], preferred_element_type=jnp.float32)
    k = PEER_TOPK
    for head in range(PEER_HEADS):
        tops = []
        for part in range(2):
            hp = head * 2 + part
            a = q[:, hp * PEER_HALF:(hp + 1) * PEER_HALF]
            qn = (a * lax.rsqrt(jnp.mean(a * a, axis=-1, keepdims=True) + EPS)
                  * gain_ref[hp:hp + 1, :]).astype(jnp.bfloat16)
            sc = lax.dot_general(keys_ref[hp], qn, (((1,), (1,)), ((), ())),
                                 preferred_element_type=jnp.float32)
            tops.append(_topk_rows_paired(sc, k))
        (s1, i1), (s2, i2) = tops
        half = k // 2
        sub = lax.broadcasted_iota(jnp.int32, (half, s1.shape[1]), 0)
        vals = [s1[0:1] + s2]
        ids = [i1[0:1] * float(PEER_N_KEYS) + i2]
        for a in range(1, half):
            vals.append(jnp.where(sub < k // (a + 1), s1[a:a + 1] + s2[:half], -jnp.inf))
            ids.append(i1[a:a + 1] * float(PEER_N_KEYS) + i2[:half])
        vals.append(s1[half:] + s2[0:1])
        ids.append(i1[half:] * float(PEER_N_KEYS) + i2[0:1])
        top_s, expert = _topk_rows(jnp.concatenate(vals, axis=0), k, payload=jnp.concatenate(ids, axis=0))
        e = jnp.exp(top_s - top_s[0:1])
        gate = e / jnp.sum(e, axis=0, keepdims=True)
        idx_ref[head * k:(head + 1) * k, :] = expert.astype(jnp.int32)
        gate_ref[head * k:(head + 1) * k, :] = gate


def _route(h2, wq_bf16, q_gain, sub_keys_bf16):
    t, d = h2.shape
    tm = LANES
    hk = PEER_HEADS * PEER_TOPK
    nq = wq_bf16.shape[1]
    return pl.pallas_call(
        _route_kernel,
        grid=(t // tm,),
        in_specs=[pl.BlockSpec((tm, d), lambda i: (i, 0)),
                  pl.BlockSpec((d, nq), lambda i: (0, 0)),
                  pl.BlockSpec((2 * PEER_HEADS, PEER_HALF), lambda i: (0, 0)),
                  pl.BlockSpec((2 * PEER_HEADS, PEER_N_KEYS, PEER_HALF), lambda i: (0, 0, 0))],
        out_specs=[pl.BlockSpec((hk, tm), lambda i: (0, i)),
                   pl.BlockSpec((hk, tm), lambda i: (0, i))],
        out_shape=[jax.ShapeDtypeStruct((hk, t), jnp.int32), jax.ShapeDtypeStruct((hk, t), jnp.float32)],
        compiler_params=_cparams(("arbitrary",)),
        name="peer_route",
    )(h2, wq_bf16, q_gain.reshape(2 * PEER_HEADS, PEER_HALF), sub_keys_bf16)


PEER_TB = 32
SUBLANES = 8


PEER_STAGES = 4
PEER_LOOKAHEAD = 3


def _expert_kernel(idx_ref, nidx_ref, h_ref, gate_ref, x1_ref, g2_ref, tab_ref, o_ref, *scratch):
    stages, sem = scratch[:PEER_STAGES], scratch[PEER_STAGES]
    tb, d = h_ref.shape
    k = PEER_TOPK
    hk = PEER_HEADS * k
    nr, nc = tb // SUBLANES, d // LANES
    step = pl.program_id(0)
    hi_mask = jnp.uint32(0xFFFF0000)

    def start_rows(ids, head, j, r):
        for s in range(SUBLANES):
            e = ids[0, 0, (r * SUBLANES + s) * hk + head * k + j]
            pltpu.make_async_copy(tab_ref.at[pl.ds(pl.multiple_of(e * nc, nc), nc), :],
                                  stages[head % PEER_STAGES].at[j, r, :, s, :],
                                  sem.at[head % PEER_STAGES]).start(priority=s % 2)

    def words(stage, j, r, c):
        return stage[j, r, c]

    def start_head(ids, head):
        for j in range(k):
            for r in range(nr):
                start_rows(ids, head, j, r)

    def wait_head(head):
        buf = stages[head % PEER_STAGES]
        pltpu.make_async_copy(buf, buf, sem.at[head % PEER_STAGES]).wait()

    @pl.when(step == 0)
    def _():
        for head in range(PEER_LOOKAHEAD):
            start_head(idx_ref, head)

    lane = lax.broadcasted_iota(jnp.int32, (SUBLANES, LANES), 1)
    for head in range(PEER_HEADS):
        stage = stages[head % PEER_STAGES]
        wait_head(head)
        ahead = head + PEER_LOOKAHEAD
        ahead_ids, ahead_head = (idx_ref, ahead) if ahead < PEER_HEADS else (nidx_ref, ahead - PEER_HEADS)
        acts = []
        for r in range(nr):
            rows = slice(r * SUBLANES, (r + 1) * SUBLANES)
            pre = jnp.zeros((SUBLANES, LANES), jnp.float32)
            for j in range(k):
                start_rows(ahead_ids, ahead_head, j, r)
                acc = None
                for c in range(nc):
                    u = lax.bitcast_convert_type(words(stage, j, r, c) & hi_mask, jnp.float32)
                    prod = h_ref[rows, c * LANES:(c + 1) * LANES] * u
                    acc = prod if acc is None else acc + prod
                pre = jnp.where(lane == head * k + j, jnp.sum(acc, axis=-1, keepdims=True), pre)
            acts.append(gate_ref[rows, :] * _gelu(pre))
        for r in range(nr):
            rows = slice(r * SUBLANES, (r + 1) * SUBLANES)
            accs = [None] * nc
            for j in range(k):
                a = jnp.broadcast_to(acts[r][:, head * k + j:head * k + j + 1], (SUBLANES, LANES))
                for c in range(nc):
                    term = a * lax.bitcast_convert_type(words(stage, j, r, c) << 16, jnp.float32)
                    accs[c] = term if accs[c] is None else accs[c] + term
            for c in range(nc):
                cs = slice(c * LANES, (c + 1) * LANES)
                if head == 0:
                    o_ref[rows, cs] = accs[c]
                else:
                    o_ref[rows, cs] = o_ref[rows, cs] + accs[c]

    @pl.when(step + 1 == pl.num_programs(0))
    def _():
        for head in range(PEER_LOOKAHEAD):
            wait_head(head)
    o_ref[...] = x1_ref[...] + g2_ref[0] * o_ref[...]


def _experts(idx_tok, gate_tok, h2, x1, g2, table, seq):
    t, d = h2.shape
    tb = PEER_TB
    hk = PEER_HEADS * PEER_TOPK
    per_batch = seq // tb
    nsteps = t // tb
    ids = idx_tok.reshape(nsteps, 1, tb * hk)
    stage = pltpu.VMEM((PEER_TOPK, tb // SUBLANES, d // LANES, SUBLANES, LANES), jnp.uint32)
    return pl.pallas_call(
        _expert_kernel,
        grid=(nsteps,),
        in_specs=[pl.BlockSpec((1, 1, tb * hk), lambda i: (i, 0, 0), memory_space=pltpu.SMEM),
                  pl.BlockSpec((1, 1, tb * hk), lambda i: (jnp.minimum(i + 1, nsteps - 1), 0, 0),
                               memory_space=pltpu.SMEM),
                  pl.BlockSpec((tb, d), lambda i: (i, 0)),
                  pl.BlockSpec((tb, hk), lambda i: (i, 0)),
                  pl.BlockSpec((tb, d), lambda i: (i, 0)),
                  pl.BlockSpec((1, 1, d), lambda i: (i // per_batch, 0, 0)),
                  pl.BlockSpec(memory_space=pl.ANY)],
        out_specs=pl.BlockSpec((tb, d), lambda i: (i, 0)),
        out_shape=jax.ShapeDtypeStruct((t, d), jnp.float32),
        scratch_shapes=[stage] * PEER_STAGES + [pltpu.SemaphoreType.DMA((PEER_STAGES,))],
        compiler_params=_cparams(("arbitrary",)),
        name="peer_experts",
    )(ids, ids, h2, gate_tok, x1, g2, table)


PACK_ROWS = 256


def _pack_kernel(u_ref, v_ref, o_ref):
    nc = u_ref.shape[1] // LANES
    as_bits = lambda x: lax.bitcast_convert_type(x.astype(jnp.bfloat16).astype(jnp.float32), jnp.uint32)
    w = as_bits(u_ref[...]) | (as_bits(v_ref[...]) >> 16)
    for c in range(nc):
        o_ref[pl.ds(c, PACK_ROWS, stride=nc), :] = w[:, c * LANES:(c + 1) * LANES]


def _pack_expert_table(u, v):
    e, d = u.shape
    nc = d // LANES
    return pl.pallas_call(
        _pack_kernel,
        grid=(e // PACK_ROWS,),
        in_specs=[pl.BlockSpec((PACK_ROWS, d), lambda i: (i, 0)),
                  pl.BlockSpec((PACK_ROWS, d), lambda i: (i, 0))],
        out_specs=pl.BlockSpec((PACK_ROWS * nc, LANES), lambda i: (i, 0)),
        out_shape=jax.ShapeDtypeStruct((e * nc, LANES), jnp.uint32),
        compiler_params=_cparams(("arbitrary",)),
        name="peer_pack_table",
    )(u, v)


def _proj_weights(w_in, nsa_q_gain, nsa_k_gain, dil_q_gain, dil_k_gain, d_model, lay):
    dh = HEAD_DIM
    off_kv = NSA_HEADS * dh
    off_gate = off_kv + 6 * NSA_KV_GROUPS * dh
    off_dil = off_gate + 3 * NSA_HEADS
    off_merge = off_dil + 3 * DIL_HEADS * dh
    n_cols = lay["n"] * LANES
    pad = n_cols - (lay["gate"] * LANES + 3 * NSA_HEADS)
    w = jnp.concatenate([w_in[:, off_merge:], w_in[:, :off_kv], w_in[:, off_kv:off_gate],
                         w_in[:, off_gate:off_dil], jnp.zeros((d_model, pad), w_in.dtype)],
                        axis=1).astype(jnp.bfloat16)
    w_dil = w_in[:, off_dil:off_merge].astype(jnp.bfloat16)
    ones = jnp.ones((dh,), jnp.float32)

    def rows(spec):
        gains = jnp.concatenate([g for g, _, n in spec for _ in range(n)])
        flags = jnp.concatenate([jnp.full((dh,), f, jnp.float32) for _, f, n in spec for _ in range(n)])
        return flags.reshape(1, -1), gains.reshape(1, -1)

    g = NSA_KV_GROUPS
    main = [(ones, 0.0, lay["q"]), (nsa_q_gain * QK_SCALE, 1.0, NSA_HEADS), (ones, 0.0, 2 * g),
            (nsa_k_gain[1], 1.0, g), (ones, 0.0, g), (nsa_k_gain[2], 1.0, g), (ones, 0.0, g),
            (ones, 0.0, lay["n"] - lay["gate"])]
    dil = [(dil_q_gain * QK_SCALE, 1.0, DIL_HEADS), (dil_k_gain, 1.0, DIL_HEADS), (ones, 0.0, DIL_HEADS)]
    return (w,) + rows(main), (w_dil,) + rows(dil)


def _token_mixer_and_norm2(x, mod, norm1_g, w_in, nsa_q_gain, nsa_k_gain, cmp_pos, cmp_w1, cmp_w2,
                           dil_q_gain, dil_k_gain, w_br_nsa, w_br_dil, w_out, norm2_g, rel_bias):
    b, seq, d = x.shape
    t = b * seq
    lay = _layout(d)
    sh1, sc1, g1, sh2, sc2, g2 = [m.reshape(b, 1, d) for m in jnp.split(mod, 6, axis=-1)]
    x2d = x.reshape(t, d)

    main_w, dil_w = _proj_weights(w_in, nsa_q_gain, nsa_k_gain, dil_q_gain, dil_k_gain, d, lay)
    blocks_per_tile = 4
    proj2d = _projection(x2d, norm1_g, sc1, sh1, *main_w, seq, jnp.bfloat16, (0, lay["q"] // blocks_per_tile))
    proj3 = proj2d.reshape(b, seq, lay["n"] * LANES)
    projd3 = _projection(x2d, norm1_g, sc1, sh1, *dil_w, seq, jnp.float32,
                         (2 * DIL_HEADS // blocks_per_tile, 3 * DIL_HEADS // blocks_per_tile)).reshape(b, seq, -1)

    rel_flat = rel_bias.reshape(-1) * LOG2E
    nchunk = seq // NSA_CMP_STRIDE
    ns = seq // NSA_SEL_LEN
    bias_win = _bias_table(rel_flat, _band_buckets(-(-(NSA_WINDOW - 1) // QT), NSA_WINDOW - 1, 1), 0, NSA_HEADS)
    bias_sel = _bias_table(rel_flat, _causal_buckets(_n_causal_tiles(seq)), 0, NSA_HEADS)
    bias_cmp = _bias_table(rel_flat, _cmp_buckets(seq, nchunk), 0, NSA_HEADS)
    bias_dil = [_bias_table(rel_flat, _band_buckets(-(-(wnd // dil) // QT), wnd // dil, dil),
                            NSA_HEADS + gi * DIL_HEADS_PER_GROUP, DIL_HEADS_PER_GROUP)
                for gi, (wnd, dil) in enumerate(DIL_CONFIGS)]

    g = NSA_KV_GROUPS
    c0 = lay["kv"] * LANES
    tc = proj3[:, :, c0:c0 + 2 * g * HEAD_DIM].reshape(b, nchunk, NSA_CMP_STRIDE, 2 * g, HEAD_DIM)
    tc = tc.transpose(0, 3, 1, 2, 4).reshape(b, 2 * g, nchunk, NSA_CMP_STRIDE * HEAD_DIM)
    kcvc = _compress(tc, cmp_pos, cmp_w1, cmp_w2, nsa_k_gain[0])

    cstart = np.arange(nchunk) * NSA_CMP_STRIDE
    sstart = np.arange(ns) * NSA_SEL_LEN
    n_cmp = (seq - NSA_CMP_LEN) // NSA_CMP_STRIDE + 1
    overlap = ((cstart[:, None] < sstart[None, :] + NSA_SEL_LEN) & (cstart[:, None] + NSA_CMP_LEN > sstart[None, :])
               & (np.arange(nchunk)[:, None] < n_cmp)).astype(np.float32)
    expand_t = (np.arange(seq)[:, None] // NSA_SEL_LEN == np.arange(ns)[None, :]).astype(np.float32)

    ocmp, sel = _cmp_select(proj3, kcvc, bias_cmp, jnp.asarray(overlap.T), lay, seq)
    osel = _sel_attention(proj3, sel, jnp.asarray(expand_t, jnp.bfloat16), bias_sel, lay, seq)
    owin = _win_attention(proj3, bias_win, lay, seq)
    odil, lsedil = zip(*[_dil_attention(projd3, bias_dil[gi], seq, gi) for gi in range(len(DIL_CONFIGS))])

    x1, h2 = _post(ocmp.reshape(t, -1), osel.reshape(t, -1), owin.reshape(t, -1), proj2d,
                   [o.reshape(t, -1) for o in odil], [l.reshape(t, -1) for l in lsedil],
                   x2d, g1, sc2, sh2, norm2_g, w_br_nsa.astype(jnp.bfloat16), w_br_dil.astype(jnp.bfloat16),
                   w_out.astype(jnp.bfloat16), lay, seq)
    return x1, h2, g2


def kernel(x, c, w_ada, b_ada, norm1_g, w_in, nsa_q_gain, nsa_k_gain, cmp_pos, cmp_w1, cmp_w2,
           dil_q_gain, dil_k_gain, w_br_nsa, w_br_dil, w_out, norm2_g, peer_w_q, peer_q_gain,
           peer_sub_keys, peer_u, peer_v, rel_bias):
    b, seq, d = x.shape
    depth = w_ada.shape[0]
    for layer in range(depth):
        mod = _modulation(c, w_ada[layer], b_ada[layer])
        x1, h2, g2 = _token_mixer_and_norm2(
            x, mod, norm1_g[layer], w_in[layer], nsa_q_gain[layer], nsa_k_gain[layer], cmp_pos[layer],
            cmp_w1[layer], cmp_w2[layer], dil_q_gain[layer], dil_k_gain[layer], w_br_nsa[layer],
            w_br_dil[layer], w_out[layer], norm2_g[layer], rel_bias)
        idx, gate = _route(h2, peer_w_q[layer].astype(jnp.bfloat16), peer_q_gain[layer],
                           peer_sub_keys[layer].reshape(2 * PEER_HEADS, PEER_N_KEYS, PEER_HALF).astype(jnp.bfloat16))
        table = _pack_expert_table(peer_u[layer], peer_v[layer])
        out = _experts(idx.T, gate.T, h2, x1, g2, table, seq)
        x = out.reshape(b, seq, d)
    return x
```

```python
import functools
import math

import numpy as np
import jax
import jax.numpy as jnp
from jax import lax
from jax.experimental import pallas as pl
from jax.experimental.pallas import tpu as pltpu

HEAD_DIM = 128
LANES = 128
NSA_HEADS = 8
NSA_KV_GROUPS = 2
NSA_REP = NSA_HEADS // NSA_KV_GROUPS
NSA_CMP_LEN = 32
NSA_CMP_STRIDE = 16
NSA_SEL_LEN = 64
NSA_TOP_N = 16
NSA_WINDOW = 512
DIL_CONFIGS = ((128, 1), (512, 4), (2048, 16))
DIL_HEADS_PER_GROUP = 4
DIL_HEADS = DIL_HEADS_PER_GROUP * len(DIL_CONFIGS)
REL_BUCKETS = 32
REL_MAX_DIST = 2048
PEER_HEADS = 8
PEER_N_KEYS = 128
PEER_TOPK = 16
PEER_HALF = 128
EPS = 1e-6
NEG = -1e30
SEL_FORCE = 1e4
LOG2E = math.log2(math.e)
LN2 = math.log(2.0)
QK_SCALE = HEAD_DIM ** -0.5 * LOG2E
MASKED_BUCKET = REL_BUCKETS

QT = 128
VMEM_LIMIT = 48 * 1024 * 1024

def _layout(d_model):
    nmerge = 2 * d_model // LANES
    cb_q = nmerge
    cb_kv = cb_q + NSA_HEADS
    cb_gate = cb_kv + 6 * NSA_KV_GROUPS
    n_used = cb_gate + 1
    n_blocks = -(-n_used // 4) * 4
    return dict(q=cb_q, kv=cb_kv, gate=cb_gate, n=n_blocks)


def _cparams(sem):
    return pltpu.CompilerParams(dimension_semantics=sem, vmem_limit_bytes=VMEM_LIMIT)


def _t5_bucket_np(dist):
    n = np.maximum(dist, 0)
    max_exact = REL_BUCKETS // 2
    nf = np.maximum(n, max_exact).astype(np.float32)
    large = max_exact + (np.log(nf / np.float32(max_exact)) / np.float32(math.log(REL_MAX_DIST / max_exact))
                         * np.float32(REL_BUCKETS - max_exact)).astype(np.int32)
    large = np.minimum(large, REL_BUCKETS - 1)
    return np.where(n < max_exact, n, large).astype(np.int32)


def _gelu(x):
    return jax.nn.gelu(x)


def _sigmoid(x):
    return 1.0 / (1.0 + jnp.exp(-x))


def _mod_kernel(c_ref, w_ref, b_ref, o_ref):
    c = c_ref[...]
    cond = c * _sigmoid(c)
    o_ref[...] = jnp.dot(cond.astype(jnp.bfloat16), w_ref[...].astype(jnp.bfloat16),
                         preferred_element_type=jnp.float32) + b_ref[...]


def _modulation(c, w_ada, b_ada):
    b, d = c.shape
    n = w_ada.shape[1]
    rows = 8
    cp = jnp.zeros((rows, d), jnp.float32).at[:b].set(c)
    tn = 512
    out = pl.pallas_call(
        _mod_kernel,
        grid=(n // tn,),
        in_specs=[pl.BlockSpec((rows, d), lambda j: (0, 0)),
                  pl.BlockSpec((d, tn), lambda j: (0, j)),
                  pl.BlockSpec((1, tn), lambda j: (0, j))],
        out_specs=pl.BlockSpec((rows, tn), lambda j: (0, j)),
        out_shape=jax.ShapeDtypeStruct((rows, n), jnp.float32),
        compiler_params=_cparams(("arbitrary",)),
        name="adaln_mod",
    )(cp, w_ada, b_ada.reshape(1, n))
    return out[:b]


def _proj_kernel(x_ref, g_ref, sc_ref, sh_ref, w_ref, flag_ref, gain_ref, o_ref, h_ref, *, plain):
    j = pl.program_id(1)

    @pl.when(j == 0)
    def _():
        x = x_ref[...]
        y = x * lax.rsqrt(jnp.mean(x * x, axis=-1, keepdims=True) + EPS) * g_ref[...]
        h_ref[...] = (y * (1.0 + sc_ref[0]) + sh_ref[0]).astype(jnp.bfloat16)

    acc = jnp.dot(h_ref[...], w_ref[...], preferred_element_type=jnp.float32)
    is_plain = (j >= plain[0]) & (j < plain[1])

    @pl.when(is_plain)
    def _():
        o_ref[...] = acc.astype(o_ref.dtype)

    @pl.when(jnp.logical_not(is_plain))
    def _():
        for k in range(acc.shape[1] // LANES):
            a = acc[:, k * LANES:(k + 1) * LANES]
            f = flag_ref[:, k * LANES:(k + 1) * LANES]
            r = lax.rsqrt(jnp.mean(a * a, axis=-1, keepdims=True) + EPS)
            scale = f * r + (1.0 - f)
            o_ref[:, k * LANES:(k + 1) * LANES] = (a * scale * gain_ref[:, k * LANES:(k + 1) * LANES]).astype(o_ref.dtype)


def _projection(x2d, norm_g, sc, sh, w_bf16, flags, gains, seq, out_dtype, plain):
    t, d = x2d.shape
    n = w_bf16.shape[1]
    tm = min(1024, seq)
    tn = 512
    per_batch = seq // tm
    return pl.pallas_call(
        functools.partial(_proj_kernel, plain=plain),
        grid=(t // tm, n // tn),
        in_specs=[pl.BlockSpec((tm, d), lambda i, j: (i, 0)),
                  pl.BlockSpec((1, d), lambda i, j: (0, 0)),
                  pl.BlockSpec((1, 1, d), lambda i, j: (i // per_batch, 0, 0)),
                  pl.BlockSpec((1, 1, d), lambda i, j: (i // per_batch, 0, 0)),
                  pl.BlockSpec((d, tn), lambda i, j: (0, j)),
                  pl.BlockSpec((1, tn), lambda i, j: (0, j)),
                  pl.BlockSpec((1, tn), lambda i, j: (0, j))],
        out_specs=pl.BlockSpec((tm, tn), lambda i, j: (i, j)),
        out_shape=jax.ShapeDtypeStruct((t, n), out_dtype),
        scratch_shapes=[pltpu.VMEM((tm, d), jnp.bfloat16)],
        compiler_params=_cparams(("arbitrary", "arbitrary")),
        name="norm_in_proj",
    )(x2d, norm_g.reshape(1, d), sc, sh, w_bf16, flags, gains)


def _bias_kernel(tbl_ref, idx_ref, o_ref, *, head0):
    h = pl.program_id(0) + head0
    b = idx_ref[0]
    out = jnp.full(b.shape, NEG, jnp.float32)
    for k in range(REL_BUCKETS):
        out = jnp.where(b == k, tbl_ref[h * REL_BUCKETS + k], out)
    o_ref[0, 0] = out


def _bias_table(rel_bias_flat, bucket_idx, head0, n_heads):
    nt, r, c = bucket_idx.shape
    return pl.pallas_call(
        functools.partial(_bias_kernel, head0=head0),
        grid=(n_heads, nt),
        in_specs=[pl.BlockSpec(memory_space=pltpu.SMEM),
                  pl.BlockSpec((1, r, c), lambda h, t: (t, 0, 0))],
        out_specs=pl.BlockSpec((1, 1, r, c), lambda h, t: (h, t, 0, 0)),
        out_shape=jax.ShapeDtypeStruct((n_heads, nt, r, c), jnp.float32),
        compiler_params=_cparams(("arbitrary", "arbitrary")),
        name="rel_bias_table",
    )(rel_bias_flat, jnp.asarray(bucket_idx))


def _band_buckets(n_prev, span, dist_scale):
    i = np.arange(QT)[:, None]
    c = np.arange((n_prev + 1) * QT)[None, :]
    dist = n_prev * QT + i - c
    ok = (dist >= 0) & (dist <= span)
    return np.where(ok, _t5_bucket_np(dist * dist_scale), MASKED_BUCKET).astype(np.int32)[None]


def _causal_buckets(n_tiles):
    i = np.arange(QT)[None, :]
    j = np.arange(QT)[:, None]
    out = []
    for o in range(n_tiles):
        dist = QT * o + i - j
        out.append(np.where(dist >= 0, _t5_bucket_np(dist), MASKED_BUCKET))
    return np.stack(out).astype(np.int32)


def _n_causal_tiles(seq):
    first_last = int(np.argmax(_t5_bucket_np(np.arange(4 * REL_MAX_DIST)) == REL_BUCKETS - 1))
    o_const = -(-(first_last + QT - 1) // QT)
    return min(seq // QT, o_const + 1)


def _cmp_buckets(seq, n_cmp_pad):
    nq = seq // QT
    n_cmp = (seq - NSA_CMP_LEN) // NSA_CMP_STRIDE + 1
    i = np.arange(QT)[None, :, None]
    n = np.arange(nq)[:, None, None]
    c = np.arange(n_cmp_pad)[None, None, :]
    dist = n * QT + i - (c * NSA_CMP_STRIDE + NSA_CMP_LEN - 1)
    ok = (dist >= 0) & (c < n_cmp)
    return np.where(ok, _t5_bucket_np(dist), MASKED_BUCKET).astype(np.int32)


def _compress_kernel(t_ref, pos_ref, w1_ref, w2_ref, gain_ref, o_ref):
    half = t_ref.shape[3]
    a = t_ref[0, 0].astype(jnp.float32)
    lo = (a + pos_ref[0, :, :half]).astype(jnp.bfloat16)
    hi = (a + pos_ref[0, :, half:]).astype(jnp.bfloat16)
    p = jnp.dot(lo, w1_ref[0, :half, :], preferred_element_type=jnp.float32)
    q = jnp.dot(hi, w1_ref[0, half:, :], preferred_element_type=jnp.float32)
    n = q.shape[0]
    q_next = jnp.concatenate([q[1:], q[:1]], axis=0)
    hid = _gelu(p + q_next)
    out = jnp.dot(hid.astype(jnp.bfloat16), w2_ref[0], preferred_element_type=jnp.float32)
    is_key = pl.program_id(1) < NSA_KV_GROUPS
    r = lax.rsqrt(jnp.mean(out * out, axis=-1, keepdims=True) + EPS)
    normed = out * r * gain_ref[...]
    o_ref[0, 0] = jnp.where(is_key, normed, out).astype(o_ref.dtype)


def _compress(t_chunks, cmp_pos, cmp_w1, cmp_w2, k_gain0):
    b, nh, nchunk, width = t_chunks.shape
    dh = HEAD_DIM
    g = NSA_KV_GROUPS
    return pl.pallas_call(
        _compress_kernel,
        grid=(b, nh),
        in_specs=[pl.BlockSpec((1, 1, nchunk, width), lambda i, h: (i, h, 0, 0)),
                  pl.BlockSpec((1, 1, 2 * width), lambda i, h: (h // g, 0, 0)),
                  pl.BlockSpec((1, 2 * width, dh), lambda i, h: (h // g, 0, 0)),
                  pl.BlockSpec((1, dh, dh), lambda i, h: (h // g, 0, 0)),
                  pl.BlockSpec((1, dh), lambda i, h: (0, 0))],
        out_specs=pl.BlockSpec((1, 1, nchunk, dh), lambda i, h: (i, h, 0, 0)),
        out_shape=jax.ShapeDtypeStruct((b, nh, nchunk, dh), jnp.bfloat16),
        compiler_params=_cparams(("arbitrary", "arbitrary")),
        name="nsa_compress",
    )(t_chunks, cmp_pos.reshape(2, 1, 2 * width), cmp_w1.astype(jnp.bfloat16),
      cmp_w2.astype(jnp.bfloat16), k_gain0.reshape(1, dh))


def _stack_heads(q):
    return jnp.concatenate([q[:, r * HEAD_DIM:(r + 1) * HEAD_DIM] for r in range(NSA_REP)], axis=0)


def _cmp_select_kernel(q_ref, kc_ref, vc_ref, bias_ref, ov_ref, o_ref, sel_ref, *, n_sel):
    tq = q_ref.shape[1]
    n = pl.program_id(2)
    qs = _stack_heads(q_ref[0])
    s = lax.dot_general(qs, kc_ref[0, 0], (((1,), (1,)), ((), ())),
                        preferred_element_type=jnp.float32)
    bias = jnp.concatenate([bias_ref[r, 0] for r in range(NSA_REP)], axis=0)
    valid = bias > 0.5 * NEG
    s = jnp.where(valid, s + bias, NEG)
    m = jnp.max(s, axis=-1, keepdims=True)
    e = jnp.exp2(s - m)
    p = e / jnp.sum(e, axis=-1, keepdims=True) * valid.astype(jnp.float32)
    o = jnp.dot(p.astype(jnp.bfloat16), vc_ref[0, 0], preferred_element_type=jnp.float32)
    for r in range(NSA_REP):
        o_ref[0, :, r * HEAD_DIM:(r + 1) * HEAD_DIM] = o[r * tq:(r + 1) * tq].astype(o_ref.dtype)

    psum = p[0:tq]
    for r in range(1, NSA_REP):
        psum = psum + p[r * tq:(r + 1) * tq]
    imp = lax.dot_general(ov_ref[...], psum, (((1,), (1,)), ((), ())), preferred_element_type=jnp.float32,
                          precision=lax.Precision.HIGHEST)
    n_blocks = sel_ref.shape[2]
    jb = lax.broadcasted_iota(jnp.int32, (n_blocks, tq), 0).astype(jnp.float32)
    qpos = lax.broadcasted_iota(jnp.int32, (n_blocks, tq), 1)
    cur = ((n * tq + qpos) // NSA_SEL_LEN).astype(jnp.float32)
    forced = (jb == 0.0) | (jb == cur) | (jb == cur - 1.0)
    score = jnp.where(forced, SEL_FORCE, jnp.where(jb > cur, -SEL_FORCE, imp))
    chosen = jnp.zeros((n_blocks, tq), jnp.float32)
    for _ in range(n_sel):
        mx = jnp.max(score, axis=0, keepdims=True)
        first = jnp.min(jnp.where(score == mx, jb, float(n_blocks)), axis=0, keepdims=True)
        hit = jb == first
        chosen = jnp.where(hit, 1.0, chosen)
        score = jnp.where(hit, -jnp.inf, score)
    sel_ref[0, 0] = chosen


def _cmp_select(proj3, kcvc, cbias, overlap, lay, seq):
    b = proj3.shape[0]
    g = NSA_KV_GROUPS
    ncp = kcvc.shape[2]
    ns = seq // NSA_SEL_LEN
    n_sel = min(NSA_TOP_N, ns)
    qw = NSA_REP * HEAD_DIM
    return pl.pallas_call(
        functools.partial(_cmp_select_kernel, n_sel=n_sel),
        grid=(b, g, seq // QT),
        in_specs=[pl.BlockSpec((1, QT, qw), lambda i, j, n: (i, n, lay["q"] // NSA_REP + j)),
                  pl.BlockSpec((1, 1, ncp, HEAD_DIM), lambda i, j, n: (i, j, 0, 0)),
                  pl.BlockSpec((1, 1, ncp, HEAD_DIM), lambda i, j, n: (i, g + j, 0, 0)),
                  pl.BlockSpec((NSA_REP, 1, QT, ncp), lambda i, j, n: (j, n, 0, 0)),
                  pl.BlockSpec((ns, ncp), lambda i, j, n: (0, 0))],
        out_specs=[pl.BlockSpec((1, QT, qw), lambda i, j, n: (i, n, j)),
                   pl.BlockSpec((1, 1, ns, QT), lambda i, j, n: (i, j, 0, n))],
        out_shape=[jax.ShapeDtypeStruct((b, seq, NSA_HEADS * HEAD_DIM), jnp.bfloat16),
                   jax.ShapeDtypeStruct((b, g, ns, seq), jnp.float32)],
        compiler_params=_cparams(("arbitrary", "arbitrary", "arbitrary")),
        name="nsa_cmp_select",
    )(proj3, kcvc, kcvc, cbias, overlap)


def _sel_attn_kernel(q_ref, k_ref, vt_ref, sel_ref, exp_ref, bias_ref, o_ref, madd_ref):
    tq = q_ref.shape[1]
    n = pl.program_id(2)
    n_bias = bias_ref.shape[1]
    q = q_ref[0].astype(jnp.float32)
    qt = jnp.concatenate([q[:, r * HEAD_DIM:(r + 1) * HEAD_DIM].T for r in range(NSA_REP)],
                         axis=1).astype(jnp.bfloat16)
    km = jnp.dot(exp_ref[...], sel_ref[0, 0].astype(jnp.bfloat16), preferred_element_type=jnp.float32)
    madd_ref[...] = (km - 1.0) * (-NEG)

    last_tile = k_ref.shape[1] // QT - 1

    def tile_start(kt):
        return pl.multiple_of(jnp.minimum(kt, last_tile) * QT, QT)

    def logits(kt):
        start = tile_start(kt)
        s = jnp.dot(k_ref[0, pl.ds(start, QT), :], qt, preferred_element_type=jnp.float32)
        o = n - kt
        ob = jnp.clip(o, 0, n_bias - 1)
        bias = jnp.concatenate([bias_ref[r, ob] for r in range(NSA_REP)], axis=1)
        ma = jnp.where(o >= 0, madd_ref[pl.ds(start, QT), :], NEG)
        return s + bias + jnp.concatenate([ma] * NSA_REP, axis=1)

    def body(kp, carry):
        m, l, acc, s0, s1 = carry
        n0, n1 = logits(2 * kp + 2), logits(2 * kp + 3)
        m_new = jnp.maximum(m, jnp.maximum(jnp.max(s0, axis=0, keepdims=True),
                                           jnp.max(s1, axis=0, keepdims=True)))
        alpha = jnp.exp2(m - m_new)
        p0 = jnp.exp2(s0 - m_new)
        p1 = jnp.exp2(s1 - m_new)
        l = alpha * l + jnp.sum(p0, axis=0, keepdims=True) + jnp.sum(p1, axis=0, keepdims=True)
        acc = (alpha * acc
               + jnp.dot(vt_ref[0, 0, :, pl.ds(tile_start(2 * kp), QT)], p0.astype(jnp.bfloat16),
                         preferred_element_type=jnp.float32)
               + jnp.dot(vt_ref[0, 0, :, pl.ds(tile_start(2 * kp + 1), QT)], p1.astype(jnp.bfloat16),
                         preferred_element_type=jnp.float32))
        return m_new, l, acc, n0, n1

    cols = NSA_REP * tq
    init = (jnp.full((1, cols), NEG, jnp.float32), jnp.zeros((1, cols), jnp.float32),
            jnp.zeros((HEAD_DIM, cols), jnp.float32), logits(0), logits(1))
    m, l, acc, _, _ = lax.fori_loop(0, n // 2 + 1, body, init)
    out = acc / l
    for r in range(NSA_REP):
        o_ref[0, :, r * HEAD_DIM:(r + 1) * HEAD_DIM] = out[:, r * tq:(r + 1) * tq].T.astype(o_ref.dtype)


def _sel_attention(proj3, sel, expand_t, bias_sel_t, lay, seq):
    b = proj3.shape[0]
    g = NSA_KV_GROUPS
    ns = seq // NSA_SEL_LEN
    qw = NSA_REP * HEAD_DIM
    nb = bias_sel_t.shape[1]
    cb_k = lay["kv"] + 2 * g
    cb_v = lay["kv"] + 3 * g
    vt = proj3[:, :, cb_v * LANES:(cb_v + g) * LANES].reshape(b, seq, g, HEAD_DIM).transpose(0, 2, 3, 1)
    return pl.pallas_call(
        _sel_attn_kernel,
        grid=(b, g, seq // QT),
        in_specs=[pl.BlockSpec((1, QT, qw), lambda i, j, n: (i, n, lay["q"] // NSA_REP + j)),
                  pl.BlockSpec((1, seq, HEAD_DIM), lambda i, j, n: (i, 0, cb_k + j)),
                  pl.BlockSpec((1, 1, HEAD_DIM, seq), lambda i, j, n: (i, j, 0, 0)),
                  pl.BlockSpec((1, 1, ns, QT), lambda i, j, n: (i, j, 0, n)),
                  pl.BlockSpec((seq, ns), lambda i, j, n: (0, 0)),
                  pl.BlockSpec((NSA_REP, nb, QT, QT), lambda i, j, n: (j, 0, 0, 0))],
        out_specs=pl.BlockSpec((1, QT, qw), lambda i, j, n: (i, n, j)),
        out_shape=jax.ShapeDtypeStruct((b, seq, NSA_HEADS * HEAD_DIM), jnp.bfloat16),
        scratch_shapes=[pltpu.VMEM((seq, QT), jnp.float32)],
        compiler_params=_cparams(("arbitrary", "arbitrary", "arbitrary")),
        name="nsa_selected",
    )(proj3, proj3, vt, sel, expand_t, bias_sel_t)


def _band_softmax(qs, k_tile, v_tile, bias, n, n_prev):
    logits = []
    for j in range(n_prev + 1):
        kb = n - n_prev + j
        k = k_tile(jnp.maximum(kb, 0))
        s = lax.dot_general(qs, k, (((1,), (1,)), ((), ())), preferred_element_type=jnp.float32)
        bj = jnp.where(kb >= 0, bias[:, j * QT:(j + 1) * QT], NEG)
        logits.append(jnp.where(bj > 0.5 * NEG, s + bj, NEG))
    m = logits[0].max(axis=-1, keepdims=True)
    for s in logits[1:]:
        m = jnp.maximum(m, s.max(axis=-1, keepdims=True))
    l = jnp.zeros_like(m)
    acc = jnp.zeros((qs.shape[0], HEAD_DIM), jnp.float32)
    for j, s in enumerate(logits):
        p = jnp.exp2(s - m)
        l = l + jnp.sum(p, axis=-1, keepdims=True)
        acc = acc + jnp.dot(p.astype(jnp.bfloat16), v_tile(jnp.maximum(n - n_prev + j, 0)),
                            preferred_element_type=jnp.float32)
    return acc, m, l


def _win_attn_kernel(q_ref, k_ref, v_ref, bias_ref, o_ref, *, n_prev):
    tq = q_ref.shape[1]
    n = pl.program_id(2)
    qs = _stack_heads(q_ref[0])
    bias = jnp.concatenate([bias_ref[r, 0] for r in range(NSA_REP)], axis=0)
    tile = lambda ref: (lambda kb: ref[0, pl.ds(pl.multiple_of(kb * QT, QT), QT), :])
    acc, _, l = _band_softmax(qs, tile(k_ref), tile(v_ref), bias, n, n_prev)
    o = acc / l
    for r in range(NSA_REP):
        o_ref[0, :, r * HEAD_DIM:(r + 1) * HEAD_DIM] = o[r * tq:(r + 1) * tq].astype(o_ref.dtype)


def _win_attention(proj3, bias_win, lay, seq):
    b = proj3.shape[0]
    g = NSA_KV_GROUPS
    qw = NSA_REP * HEAD_DIM
    n_prev = -(-(NSA_WINDOW - 1) // QT)
    cb_k = lay["kv"] + 4 * g
    cb_v = lay["kv"] + 5 * g
    return pl.pallas_call(
        functools.partial(_win_attn_kernel, n_prev=n_prev),
        grid=(b, g, seq // QT),
        in_specs=[pl.BlockSpec((1, QT, qw), lambda i, j, n: (i, n, lay["q"] // NSA_REP + j)),
                  pl.BlockSpec((1, seq, HEAD_DIM), lambda i, j, n: (i, 0, cb_k + j)),
                  pl.BlockSpec((1, seq, HEAD_DIM), lambda i, j, n: (i, 0, cb_v + j)),
                  pl.BlockSpec((NSA_REP, 1, QT, (n_prev + 1) * QT), lambda i, j, n: (j, 0, 0, 0))],
        out_specs=pl.BlockSpec((1, QT, qw), lambda i, j, n: (i, n, j)),
        out_shape=jax.ShapeDtypeStruct((b, seq, NSA_HEADS * HEAD_DIM), jnp.bfloat16),
        compiler_params=_cparams(("arbitrary", "arbitrary", "arbitrary")),
        name="nsa_window",
    )(proj3, proj3, proj3, bias_win)


DIL_TILES_PER_STEP = 4


def _dil_attn_kernel(q_ref, k_ref, v_ref, bias_ref, o_ref, lse_ref, *, n_prev, dil):
    tiles = q_ref.shape[1] // (QT * dil)
    for sub in range(DIL_TILES_PER_STEP):
        t = pl.program_id(2) * DIL_TILES_PER_STEP + sub
        r = t // tiles
        n = t % tiles
        rows = lambda kb, r=r: pl.ds(kb * (QT * dil) + r, QT, stride=dil) if dil > 1 else pl.ds(
            pl.multiple_of(kb * QT, QT), QT)
        tile = lambda ref, rows=rows: (lambda kb: ref[0, rows(kb), :].astype(jnp.bfloat16))
        acc, m, l = _band_softmax(tile(q_ref)(n), tile(k_ref), tile(v_ref), bias_ref[0, 0], n, n_prev)
        o_ref[0, rows(n), :] = acc / l
        lse_ref[0, rows(n), :] = jnp.broadcast_to(m * LN2 + jnp.log(l), (QT, HEAD_DIM))


def _dil_attention(projd3, bias_dil, seq, gi):
    b = projd3.shape[0]
    window, dil = DIL_CONFIGS[gi]
    p_heads = DIL_HEADS_PER_GROUP
    n_prev = -(-(window // dil) // QT)
    cb_q = gi * p_heads
    cb_k = cb_q + DIL_HEADS
    cb_v = cb_k + DIL_HEADS
    out_w = p_heads * HEAD_DIM
    whole = lambda cb: pl.BlockSpec((1, seq, HEAD_DIM), lambda i, p, t: (i, 0, cb + p))
    return pl.pallas_call(
        functools.partial(_dil_attn_kernel, n_prev=n_prev, dil=dil),
        grid=(b, p_heads, seq // (QT * DIL_TILES_PER_STEP)),
        in_specs=[whole(cb_q), whole(cb_k), whole(cb_v),
                  pl.BlockSpec((1, 1, QT, (n_prev + 1) * QT), lambda i, p, t: (p, 0, 0, 0))],
        out_specs=[whole(0), whole(0)],
        out_shape=[jax.ShapeDtypeStruct((b, seq, out_w), jnp.float32),
                   jax.ShapeDtypeStruct((b, seq, out_w), jnp.float32)],
        compiler_params=_cparams(("arbitrary",) * 3),
        name=f"dilated_attn_{gi}",
    )(projd3, projd3, projd3, bias_dil)


def _post_kernel(ocmp_ref, osel_ref, owin_ref, gate_ref, od0_ref, od1_ref, od2_ref, l0_ref, l1_ref, l2_ref,
                 mg1_ref, mg2_ref, x_ref, g1_ref, sc2_ref, sh2_ref, n2_ref, wn_ref, wd_ref, wo_ref,
                 x1_ref, h2_ref):
    gates = _sigmoid(gate_ref[...].astype(jnp.float32))
    parts = []
    for h in range(NSA_HEADS):
        sl = slice(h * HEAD_DIM, (h + 1) * HEAD_DIM)
        y = (gates[:, 3 * h:3 * h + 1] * ocmp_ref[:, sl].astype(jnp.float32)
             + gates[:, 3 * h + 1:3 * h + 2] * osel_ref[:, sl].astype(jnp.float32)
             + gates[:, 3 * h + 2:3 * h + 3] * owin_ref[:, sl].astype(jnp.float32))
        parts.append(y.astype(jnp.bfloat16))
    y_nsa = jnp.concatenate(parts, axis=1)

    l0, l1, l2 = l0_ref[...], l1_ref[...], l2_ref[...]
    mx = jnp.maximum(jnp.maximum(l0, l1), l2)
    e0, e1, e2 = jnp.exp(l0 - mx), jnp.exp(l1 - mx), jnp.exp(l2 - mx)
    den = e0 + e1 + e2
    y_dil = ((e0 / den) * od0_ref[...].astype(jnp.float32) + (e1 / den) * od1_ref[...].astype(jnp.float32)
             + (e2 / den) * od2_ref[...].astype(jnp.float32)).astype(jnp.bfloat16)

    a = jnp.dot(y_nsa, wn_ref[...], preferred_element_type=jnp.float32)
    bb = jnp.dot(y_dil, wd_ref[...], preferred_element_type=jnp.float32)
    merged = (_sigmoid(mg1_ref[...].astype(jnp.float32)) * a
              + _sigmoid(mg2_ref[...].astype(jnp.float32)) * bb).astype(jnp.bfloat16)
    mix = jnp.dot(merged, wo_ref[...], preferred_element_type=jnp.float32)
    x1 = x_ref[...] + g1_ref[0] * mix
    x1_ref[...] = x1
    y = x1 * lax.rsqrt(jnp.mean(x1 * x1, axis=-1, keepdims=True) + EPS) * n2_ref[...]
    h2_ref[...] = y * (1.0 + sc2_ref[0]) + sh2_ref[0]


def _post(ocmp, osel, owin, proj2d, odil, lsedil, x2d, g1, sc2, sh2, norm2_g, wn, wd, wo, lay, seq):
    t, d = x2d.shape
    tm = 256
    per_batch = seq // tm
    nw = NSA_HEADS * HEAD_DIM
    dw = DIL_HEADS_PER_GROUP * HEAD_DIM
    row = lambda w: pl.BlockSpec((tm, w), lambda i: (i, 0))
    mod = pl.BlockSpec((1, 1, d), lambda i: (i // per_batch, 0, 0))
    full = lambda a: pl.BlockSpec(a.shape, lambda i: (0, 0))
    return pl.pallas_call(
        _post_kernel,
        grid=(t // tm,),
        in_specs=[row(nw), row(nw), row(nw),
                  pl.BlockSpec((tm, LANES), lambda i: (i, lay["gate"])),
                  row(dw), row(dw), row(dw), row(dw), row(dw), row(dw),
                  pl.BlockSpec((tm, d), lambda i: (i, 0)),
                  pl.BlockSpec((tm, d), lambda i: (i, 1)),
                  row(d), mod, mod, mod,
                  pl.BlockSpec((1, d), lambda i: (0, 0)),
                  full(wn), full(wd), full(wo)],
        out_specs=[row(d), row(d)],
        out_shape=[jax.ShapeDtypeStruct((t, d), jnp.float32), jax.ShapeDtypeStruct((t, d), jnp.float32)],
        compiler_params=_cparams(("arbitrary",)),
        name="merge_out_proj",
    )(ocmp, osel, owin, proj2d, odil[0], odil[1], odil[2], lsedil[0], lsedil[1], lsedil[2],
      proj2d, proj2d, x2d, g1, sc2, sh2, norm2_g.reshape(1, d), wn, wd, wo)


def _topk_rows(sc, k, payload=None):
    n = sc.shape[0]
    half = n // 2
    row = lax.broadcasted_iota(jnp.int32, (half, sc.shape[1]), 0).astype(jnp.float32)
    a, b = sc[:half], sc[half:]
    a_first = a >= b
    hi, lo = jnp.where(a_first, a, b), jnp.where(a_first, b, a)
    hi_id = jnp.where(a_first, row, row + float(half))
    lo_id = jnp.where(a_first, row + float(half), row)
    if payload is not None:
        hi_pay = jnp.where(a_first, payload[:half], payload[half:])
        lo_pay = jnp.where(a_first, payload[half:], payload[:half])
    vals, picks = [], []
    for _ in range(k):
        mx = jnp.max(hi, axis=0, keepdims=True)
        first = jnp.min(jnp.where(hi == mx, hi_id, float(n)), axis=0, keepdims=True)
        hit = hi_id == first
        vals.append(mx)
        if payload is None:
            picks.append(first)
        else:
            picks.append(jnp.sum(jnp.where(hit, hi_pay, 0.0), axis=0, keepdims=True))
            hi_pay = jnp.where(hit, lo_pay, hi_pay)
        hi = jnp.where(hit, lo, hi)
        hi_id = jnp.where(hit, lo_id, hi_id)
        lo = jnp.where(hit, -jnp.inf, lo)
    return jnp.concatenate(vals, axis=0), jnp.concatenate(picks, axis=0)


def _route_kernel(h_ref, wq_ref, gain_ref, keys_ref, idx_ref, gate_ref):
    q = jnp.dot(h_ref[...].astype(jnp.bfloat16), wq_ref[...], preferred_element_type=jnp.float32)
    k = PEER_TOPK
    for head in range(PEER_HEADS):
        tops = []
        for part in range(2):
            hp = head * 2 + part
            a = q[:, hp * PEER_HALF:(hp + 1) * PEER_HALF]
            qn = (a * lax.rsqrt(jnp.mean(a * a, axis=-1, keepdims=True) + EPS)
                  * gain_ref[hp:hp + 1, :]).astype(jnp.bfloat16)
            sc = lax.dot_general(keys_ref[hp], qn, (((1,), (1,)), ((), ())),
                                 preferred_element_type=jnp.float32)
            tops.append(_topk_rows(sc, k))
        (s1, i1), (s2, i2) = tops
        half = k // 2
        sub = lax.broadcasted_iota(jnp.int32, (half, s1.shape[1]), 0)
        vals = [s1[0:1] + s2]
        ids = [i1[0:1] * float(PEER_N_KEYS) + i2]
        for a in range(1, half):
            vals.append(jnp.where(sub < k // (a + 1), s1[a:a + 1] + s2[:half], -jnp.inf))
            ids.append(i1[a:a + 1] * float(PEER_N_KEYS) + i2[:half])
        vals.append(s1[half:] + s2[0:1])
        ids.append(i1[half:] * float(PEER_N_KEYS) + i2[0:1])
        top_s, expert = _topk_rows(jnp.concatenate(vals, axis=0), k, payload=jnp.concatenate(ids, axis=0))
        e = jnp.exp(top_s - top_s[0:1])
        gate = e / jnp.sum(e, axis=0, keepdims=True)
        idx_ref[head * k:(head + 1) * k, :] = expert.astype(jnp.int32)
        gate_ref[head * k:(head + 1) * k, :] = gate


def _route(h2, wq_bf16, q_gain, sub_keys_bf16):
    t, d = h2.shape
    tm = LANES
    hk = PEER_HEADS * PEER_TOPK
    nq = wq_bf16.shape[1]
    return pl.pallas_call(
        _route_kernel,
        grid=(t // tm,),
        in_specs=[pl.BlockSpec((tm, d), lambda i: (i, 0)),
                  pl.BlockSpec((d, nq), lambda i: (0, 0)),
                  pl.BlockSpec((2 * PEER_HEADS, PEER_HALF), lambda i: (0, 0)),
                  pl.BlockSpec((2 * PEER_HEADS, PEER_N_KEYS, PEER_HALF), lambda i: (0, 0, 0))],
        out_specs=[pl.BlockSpec((hk, tm), lambda i: (0, i)),
                   pl.BlockSpec((hk, tm), lambda i: (0, i))],
        out_shape=[jax.ShapeDtypeStruct((hk, t), jnp.int32), jax.ShapeDtypeStruct((hk, t), jnp.float32)],
        compiler_params=_cparams(("arbitrary",)),
        name="peer_route",
    )(h2, wq_bf16, q_gain.reshape(2 * PEER_HEADS, PEER_HALF), sub_keys_bf16)


PEER_TB = 32
SUBLANES = 8


PEER_STAGES = 4
PEER_LOOKAHEAD = 3


def _expert_kernel(idx_ref, nidx_ref, h_ref, gate_ref, x1_ref, g2_ref, tab_ref, o_ref, *scratch):
    stages, sem = scratch[:PEER_STAGES], scratch[PEER_STAGES]
    tb, d = h_ref.shape
    k = PEER_TOPK
    hk = PEER_HEADS * k
    nr, nc = tb // SUBLANES, d // LANES
    step = pl.program_id(0)
    hi_mask = jnp.uint32(0xFFFF0000)

    def start_rows(ids, head, j, r):
        for s in range(SUBLANES):
            e = ids[0, 0, (r * SUBLANES + s) * hk + head * k + j]
            pltpu.make_async_copy(tab_ref.at[pl.ds(pl.multiple_of(e * nc, nc), nc), :],
                                  stages[head % PEER_STAGES].at[j, r, :, s, :],
                                  sem.at[head % PEER_STAGES]).start(priority=s % 2)

    def words(stage, j, r, c):
        return stage[j, r, c]

    def start_head(ids, head):
        for j in range(k):
            for r in range(nr):
                start_rows(ids, head, j, r)

    def wait_head(head):
        buf = stages[head % PEER_STAGES]
        pltpu.make_async_copy(buf, buf, sem.at[head % PEER_STAGES]).wait()

    @pl.when(step == 0)
    def _():
        for head in range(PEER_LOOKAHEAD):
            start_head(idx_ref, head)

    lane = lax.broadcasted_iota(jnp.int32, (SUBLANES, LANES), 1)
    for head in range(PEER_HEADS):
        stage = stages[head % PEER_STAGES]
        wait_head(head)
        ahead = head + PEER_LOOKAHEAD
        ahead_ids, ahead_head = (idx_ref, ahead) if ahead < PEER_HEADS else (nidx_ref, ahead - PEER_HEADS)
        acts = []
        for r in range(nr):
            rows = slice(r * SUBLANES, (r + 1) * SUBLANES)
            pre = jnp.zeros((SUBLANES, LANES), jnp.float32)
            for j in range(k):
                start_rows(ahead_ids, ahead_head, j, r)
                acc = None
                for c in range(nc):
                    u = lax.bitcast_convert_type(words(stage, j, r, c) & hi_mask, jnp.float32)
                    prod = h_ref[rows, c * LANES:(c + 1) * LANES] * u
                    acc = prod if acc is None else acc + prod
                pre = jnp.where(lane == head * k + j, jnp.sum(acc, axis=-1, keepdims=True), pre)
            acts.append(gate_ref[rows, :] * _gelu(pre))
        for r in range(nr):
            rows = slice(r * SUBLANES, (r + 1) * SUBLANES)
            accs = [None] * nc
            for j in range(k):
                a = jnp.broadcast_to(acts[r][:, head * k + j:head * k + j + 1], (SUBLANES, LANES))
                for c in range(nc):
                    term = a * lax.bitcast_convert_type(words(stage, j, r, c) << 16, jnp.float32)
                    accs[c] = term if accs[c] is None else accs[c] + term
            for c in range(nc):
                cs = slice(c * LANES, (c + 1) * LANES)
                if head == 0:
                    o_ref[rows, cs] = accs[c]
                else:
                    o_ref[rows, cs] = o_ref[rows, cs] + accs[c]

    @pl.when(step + 1 == pl.num_programs(0))
    def _():
        for head in range(PEER_LOOKAHEAD):
            wait_head(head)
    o_ref[...] = x1_ref[...] + g2_ref[0] * o_ref[...]


def _experts(idx_tok, gate_tok, h2, x1, g2, table, seq):
    t, d = h2.shape
    tb = PEER_TB
    hk = PEER_HEADS * PEER_TOPK
    per_batch = seq // tb
    nsteps = t // tb
    ids = idx_tok.reshape(nsteps, 1, tb * hk)
    stage = pltpu.VMEM((PEER_TOPK, tb // SUBLANES, d // LANES, SUBLANES, LANES), jnp.uint32)
    return pl.pallas_call(
        _expert_kernel,
        grid=(nsteps,),
        in_specs=[pl.BlockSpec((1, 1, tb * hk), lambda i: (i, 0, 0), memory_space=pltpu.SMEM),
                  pl.BlockSpec((1, 1, tb * hk), lambda i: (jnp.minimum(i + 1, nsteps - 1), 0, 0),
                               memory_space=pltpu.SMEM),
                  pl.BlockSpec((tb, d), lambda i: (i, 0)),
                  pl.BlockSpec((tb, hk), lambda i: (i, 0)),
                  pl.BlockSpec((tb, d), lambda i: (i, 0)),
                  pl.BlockSpec((1, 1, d), lambda i: (i // per_batch, 0, 0)),
                  pl.BlockSpec(memory_space=pl.ANY)],
        out_specs=pl.BlockSpec((tb, d), lambda i: (i, 0)),
        out_shape=jax.ShapeDtypeStruct((t, d), jnp.float32),
        scratch_shapes=[stage] * PEER_STAGES + [pltpu.SemaphoreType.DMA((PEER_STAGES,))],
        compiler_params=_cparams(("arbitrary",)),
        name="peer_experts",
    )(ids, ids, h2, gate_tok, x1, g2, table)


PACK_ROWS = 256


def _pack_kernel(u_ref, v_ref, o_ref):
    nc = u_ref.shape[1] // LANES
    as_bits = lambda x: lax.bitcast_convert_type(x.astype(jnp.bfloat16).astype(jnp.float32), jnp.uint32)
    w = as_bits(u_ref[...]) | (as_bits(v_ref[...]) >> 16)
    for c in range(nc):
        o_ref[pl.ds(c, PACK_ROWS, stride=nc), :] = w[:, c * LANES:(c + 1) * LANES]


def _pack_expert_table(u, v):
    e, d = u.shape
    nc = d // LANES
    return pl.pallas_call(
        _pack_kernel,
        grid=(e // PACK_ROWS,),
        in_specs=[pl.BlockSpec((PACK_ROWS, d), lambda i: (i, 0)),
                  pl.BlockSpec((PACK_ROWS, d), lambda i: (i, 0))],
        out_specs=pl.BlockSpec((PACK_ROWS * nc, LANES), lambda i: (i, 0)),
        out_shape=jax.ShapeDtypeStruct((e * nc, LANES), jnp.uint32),
        compiler_params=_cparams(("arbitrary",)),
        name="peer_pack_table",
    )(u, v)


def _proj_weights(w_in, nsa_q_gain, nsa_k_gain, dil_q_gain, dil_k_gain, d_model, lay):
    dh = HEAD_DIM
    off_kv = NSA_HEADS * dh
    off_gate = off_kv + 6 * NSA_KV_GROUPS * dh
    off_dil = off_gate + 3 * NSA_HEADS
    off_merge = off_dil + 3 * DIL_HEADS * dh
    n_cols = lay["n"] * LANES
    pad = n_cols - (lay["gate"] * LANES + 3 * NSA_HEADS)
    w = jnp.concatenate([w_in[:, off_merge:], w_in[:, :off_kv], w_in[:, off_kv:off_gate],
                         w_in[:, off_gate:off_dil], jnp.zeros((d_model, pad), w_in.dtype)],
                        axis=1).astype(jnp.bfloat16)
    w_dil = w_in[:, off_dil:off_merge].astype(jnp.bfloat16)
    ones = jnp.ones((dh,), jnp.float32)

    def rows(spec):
        gains = jnp.concatenate([g for g, _, n in spec for _ in range(n)])
        flags = jnp.concatenate([jnp.full((dh,), f, jnp.float32) for _, f, n in spec for _ in range(n)])
        return flags.reshape(1, -1), gains.reshape(1, -1)

    g = NSA_KV_GROUPS
    main = [(ones, 0.0, lay["q"]), (nsa_q_gain * QK_SCALE, 1.0, NSA_HEADS), (ones, 0.0, 2 * g),
            (nsa_k_gain[1], 1.0, g), (ones, 0.0, g), (nsa_k_gain[2], 1.0, g), (ones, 0.0, g),
            (ones, 0.0, lay["n"] - lay["gate"])]
    dil = [(dil_q_gain * QK_SCALE, 1.0, DIL_HEADS), (dil_k_gain, 1.0, DIL_HEADS), (ones, 0.0, DIL_HEADS)]
    return (w,) + rows(main), (w_dil,) + rows(dil)


def _token_mixer_and_norm2(x, mod, norm1_g, w_in, nsa_q_gain, nsa_k_gain, cmp_pos, cmp_w1, cmp_w2,
                           dil_q_gain, dil_k_gain, w_br_nsa, w_br_dil, w_out, norm2_g, rel_bias):
    b, seq, d = x.shape
    t = b * seq
    lay = _layout(d)
    sh1, sc1, g1, sh2, sc2, g2 = [m.reshape(b, 1, d) for m in jnp.split(mod, 6, axis=-1)]
    x2d = x.reshape(t, d)

    main_w, dil_w = _proj_weights(w_in, nsa_q_gain, nsa_k_gain, dil_q_gain, dil_k_gain, d, lay)
    blocks_per_tile = 4
    proj2d = _projection(x2d, norm1_g, sc1, sh1, *main_w, seq, jnp.bfloat16, (0, lay["q"] // blocks_per_tile))
    proj3 = proj2d.reshape(b, seq, lay["n"] * LANES)
    projd3 = _projection(x2d, norm1_g, sc1, sh1, *dil_w, seq, jnp.float32,
                         (2 * DIL_HEADS // blocks_per_tile, 3 * DIL_HEADS // blocks_per_tile)).reshape(b, seq, -1)

    rel_flat = rel_bias.reshape(-1) * LOG2E
    nchunk = seq // NSA_CMP_STRIDE
    ns = seq // NSA_SEL_LEN
    bias_win = _bias_table(rel_flat, _band_buckets(-(-(NSA_WINDOW - 1) // QT), NSA_WINDOW - 1, 1), 0, NSA_HEADS)
    bias_sel = _bias_table(rel_flat, _causal_buckets(_n_causal_tiles(seq)), 0, NSA_HEADS)
    bias_cmp = _bias_table(rel_flat, _cmp_buckets(seq, nchunk), 0, NSA_HEADS)
    bias_dil = [_bias_table(rel_flat, _band_buckets(-(-(wnd // dil) // QT), wnd // dil, dil),
                            NSA_HEADS + gi * DIL_HEADS_PER_GROUP, DIL_HEADS_PER_GROUP)
                for gi, (wnd, dil) in enumerate(DIL_CONFIGS)]

    g = NSA_KV_GROUPS
    c0 = lay["kv"] * LANES
    tc = proj3[:, :, c0:c0 + 2 * g * HEAD_DIM].reshape(b, nchunk, NSA_CMP_STRIDE, 2 * g, HEAD_DIM)
    tc = tc.transpose(0, 3, 1, 2, 4).reshape(b, 2 * g, nchunk, NSA_CMP_STRIDE * HEAD_DIM)
    kcvc = _compress(tc, cmp_pos, cmp_w1, cmp_w2, nsa_k_gain[0])

    cstart = np.arange(nchunk) * NSA_CMP_STRIDE
    sstart = np.arange(ns) * NSA_SEL_LEN
    n_cmp = (seq - NSA_CMP_LEN) // NSA_CMP_STRIDE + 1
    overlap = ((cstart[:, None] < sstart[None, :] + NSA_SEL_LEN) & (cstart[:, None] + NSA_CMP_LEN > sstart[None, :])
               & (np.arange(nchunk)[:, None] < n_cmp)).astype(np.float32)
    expand_t = (np.arange(seq)[:, None] // NSA_SEL_LEN == np.arange(ns)[None, :]).astype(np.float32)

    ocmp, sel = _cmp_select(proj3, kcvc, bias_cmp, jnp.asarray(overlap.T), lay, seq)
    osel = _sel_attention(proj3, sel, jnp.asarray(expand_t, jnp.bfloat16), bias_sel, lay, seq)
    owin = _win_attention(proj3, bias_win, lay, seq)
    odil, lsedil = zip(*[_dil_attention(projd3, bias_dil[gi], seq, gi) for gi in range(len(DIL_CONFIGS))])

    x1, h2 = _post(ocmp.reshape(t, -1), osel.reshape(t, -1), owin.reshape(t, -1), proj2d,
                   [o.reshape(t, -1) for o in odil], [l.reshape(t, -1) for l in lsedil],
                   x2d, g1, sc2, sh2, norm2_g, w_br_nsa.astype(jnp.bfloat16), w_br_dil.astype(jnp.bfloat16),
                   w_out.astype(jnp.bfloat16), lay, seq)
    return x1, h2, g2


def kernel(x, c, w_ada, b_ada, norm1_g, w_in, nsa_q_gain, nsa_k_gain, cmp_pos, cmp_w1, cmp_w2,
           dil_q_gain, dil_k_gain, w_br_nsa, w_br_dil, w_out, norm2_g, peer_w_q, peer_q_gain,
           peer_sub_keys, peer_u, peer_v, rel_bias):
    b, seq, d = x.shape
    depth = w_ada.shape[0]
    for layer in range(depth):
        mod = _modulation(c, w_ada[layer], b_ada[layer])
        x1, h2, g2 = _token_mixer_and_norm2(
            x, mod, norm1_g[layer], w_in[layer], nsa_q_gain[layer], nsa_k_gain[layer], cmp_pos[layer],
            cmp_w1[layer], cmp_w2[layer], dil_q_gain[layer], dil_k_gain[layer], w_br_nsa[layer],
            w_br_dil[layer], w_out[layer], norm2_g[layer], rel_bias)
        idx, gate = _route(h2, peer_w_q[layer].astype(jnp.bfloat16), peer_q_gain[layer],
                           peer_sub_keys[layer].reshape(2 * PEER_HEADS, PEER_N_KEYS, PEER_HALF).astype(jnp.bfloat16))
        table = _pack_expert_table(peer_u[layer], peer_v[layer])
        out = _experts(idx.T, gate.T, h2, x1, g2, table, seq)
        x = out.reshape(b, seq, d)
    return x
```

```python
import functools
import math

import numpy as np
import jax
import jax.numpy as jnp
from jax import lax
from jax.experimental import pallas as pl
from jax.experimental.pallas import tpu as pltpu

HEAD_DIM = 128
LANES = 128
NSA_HEADS = 8
NSA_KV_GROUPS = 2
NSA_REP = NSA_HEADS // NSA_KV_GROUPS
NSA_CMP_LEN = 32
NSA_CMP_STRIDE = 16
NSA_SEL_LEN = 64
NSA_TOP_N = 16
NSA_WINDOW = 512
DIL_CONFIGS = ((128, 1), (512, 4), (2048, 16))
DIL_HEADS_PER_GROUP = 4
DIL_HEADS = DIL_HEADS_PER_GROUP * len(DIL_CONFIGS)
REL_BUCKETS = 32
REL_MAX_DIST = 2048
PEER_HEADS = 8
PEER_N_KEYS = 128
PEER_TOPK = 16
PEER_HALF = 128
EPS = 1e-6
NEG = -1e30
SEL_FORCE = 1e4
LOG2E = math.log2(math.e)
LN2 = math.log(2.0)
QK_SCALE = HEAD_DIM ** -0.5 * LOG2E
MASKED_BUCKET = REL_BUCKETS

QT = 128
VMEM_LIMIT = 48 * 1024 * 1024

def _layout(d_model):
    nmerge = 2 * d_model // LANES
    cb_q = nmerge
    cb_kv = cb_q + NSA_HEADS
    cb_gate = cb_kv + 6 * NSA_KV_GROUPS
    n_used = cb_gate + 1
    n_blocks = -(-n_used // 4) * 4
    return dict(q=cb_q, kv=cb_kv, gate=cb_gate, n=n_blocks)


def _cparams(sem):
    return pltpu.CompilerParams(dimension_semantics=sem, vmem_limit_bytes=VMEM_LIMIT)


def _t5_bucket_np(dist):
    n = np.maximum(dist, 0)
    max_exact = REL_BUCKETS // 2
    nf = np.maximum(n, max_exact).astype(np.float32)
    large = max_exact + (np.log(nf / np.float32(max_exact)) / np.float32(math.log(REL_MAX_DIST / max_exact))
                         * np.float32(REL_BUCKETS - max_exact)).astype(np.int32)
    large = np.minimum(large, REL_BUCKETS - 1)
    return np.where(n < max_exact, n, large).astype(np.int32)


def _gelu(x):
    return jax.nn.gelu(x)


def _sigmoid(x):
    return 1.0 / (1.0 + jnp.exp(-x))


def _mod_kernel(c_ref, w_ref, b_ref, o_ref):
    c = c_ref[...]
    cond = c * _sigmoid(c)
    o_ref[...] = jnp.dot(cond.astype(jnp.bfloat16), w_ref[...].astype(jnp.bfloat16),
                         preferred_element_type=jnp.float32) + b_ref[...]


def _modulation(c, w_ada, b_ada):
    b, d = c.shape
    n = w_ada.shape[1]
    rows = 8
    cp = jnp.zeros((rows, d), jnp.float32).at[:b].set(c)
    tn = 512
    out = pl.pallas_call(
        _mod_kernel,
        grid=(n // tn,),
        in_specs=[pl.BlockSpec((rows, d), lambda j: (0, 0)),
                  pl.BlockSpec((d, tn), lambda j: (0, j)),
                  pl.BlockSpec((1, tn), lambda j: (0, j))],
        out_specs=pl.BlockSpec((rows, tn), lambda j: (0, j)),
        out_shape=jax.ShapeDtypeStruct((rows, n), jnp.float32),
        compiler_params=_cparams(("arbitrary",)),
        name="adaln_mod",
    )(cp, w_ada, b_ada.reshape(1, n))
    return out[:b]


def _proj_kernel(x_ref, g_ref, sc_ref, sh_ref, w_ref, flag_ref, gain_ref, o_ref, h_ref, *, plain):
    j = pl.program_id(1)

    @pl.when(j == 0)
    def _():
        x = x_ref[...]
        y = x * lax.rsqrt(jnp.mean(x * x, axis=-1, keepdims=True) + EPS) * g_ref[...]
        h_ref[...] = (y * (1.0 + sc_ref[0]) + sh_ref[0]).astype(jnp.bfloat16)

    acc = jnp.dot(h_ref[...], w_ref[...], preferred_element_type=jnp.float32)
    is_plain = (j >= plain[0]) & (j < plain[1])

    @pl.when(is_plain)
    def _():
        o_ref[...] = acc.astype(o_ref.dtype)

    @pl.when(jnp.logical_not(is_plain))
    def _():
        for k in range(acc.shape[1] // LANES):
            a = acc[:, k * LANES:(k + 1) * LANES]
            f = flag_ref[:, k * LANES:(k + 1) * LANES]
            r = lax.rsqrt(jnp.mean(a * a, axis=-1, keepdims=True) + EPS)
            scale = f * r + (1.0 - f)
            o_ref[:, k * LANES:(k + 1) * LANES] = (a * scale * gain_ref[:, k * LANES:(k + 1) * LANES]).astype(o_ref.dtype)


def _projection(x2d, norm_g, sc, sh, w_bf16, flags, gains, seq, out_dtype, plain):
    t, d = x2d.shape
    n = w_bf16.shape[1]
    tm = min(1024, seq)
    tn = 512
    per_batch = seq // tm
    return pl.pallas_call(
        functools.partial(_proj_kernel, plain=plain),
        grid=(t // tm, n // tn),
        in_specs=[pl.BlockSpec((tm, d), lambda i, j: (i, 0)),
                  pl.BlockSpec((1, d), lambda i, j: (0, 0)),
                  pl.BlockSpec((1, 1, d), lambda i, j: (i // per_batch, 0, 0)),
                  pl.BlockSpec((1, 1, d), lambda i, j: (i // per_batch, 0, 0)),
                  pl.BlockSpec((d, tn), lambda i, j: (0, j)),
                  pl.BlockSpec((1, tn), lambda i, j: (0, j)),
                  pl.BlockSpec((1, tn), lambda i, j: (0, j))],
        out_specs=pl.BlockSpec((tm, tn), lambda i, j: (i, j)),
        out_shape=jax.ShapeDtypeStruct((t, n), out_dtype),
        scratch_shapes=[pltpu.VMEM((tm, d), jnp.bfloat16)],
        compiler_params=_cparams(("arbitrary", "arbitrary")),
        name="norm_in_proj",
    )(x2d, norm_g.reshape(1, d), sc, sh, w_bf16, flags, gains)


def _bias_kernel(tbl_ref, idx_ref, o_ref, *, head0):
    h = pl.program_id(0) + head0
    b = idx_ref[0]
    out = jnp.full(b.shape, NEG, jnp.float32)
    for k in range(REL_BUCKETS):
        out = jnp.where(b == k, tbl_ref[h * REL_BUCKETS + k], out)
    o_ref[0, 0] = out


def _bias_table(rel_bias_flat, bucket_idx, head0, n_heads):
    nt, r, c = bucket_idx.shape
    return pl.pallas_call(
        functools.partial(_bias_kernel, head0=head0),
        grid=(n_heads, nt),
        in_specs=[pl.BlockSpec(memory_space=pltpu.SMEM),
                  pl.BlockSpec((1, r, c), lambda h, t: (t, 0, 0))],
        out_specs=pl.BlockSpec((1, 1, r, c), lambda h, t: (h, t, 0, 0)),
        out_shape=jax.ShapeDtypeStruct((n_heads, nt, r, c), jnp.float32),
        compiler_params=_cparams(("arbitrary", "arbitrary")),
        name="rel_bias_table",
    )(rel_bias_flat, jnp.asarray(bucket_idx))


def _band_buckets(n_prev, span, dist_scale):
    i = np.arange(QT)[:, None]
    c = np.arange((n_prev + 1) * QT)[None, :]
    dist = n_prev * QT + i - c
    ok = (dist >= 0) & (dist <= span)
    return np.where(ok, _t5_bucket_np(dist * dist_scale), MASKED_BUCKET).astype(np.int32)[None]


def _causal_buckets(n_tiles):
    i = np.arange(QT)[None, :]
    j = np.arange(QT)[:, None]
    out = []
    for o in range(n_tiles):
        dist = QT * o + i - j
        out.append(np.where(dist >= 0, _t5_bucket_np(dist), MASKED_BUCKET))
    return np.stack(out).astype(np.int32)


def _n_causal_tiles(seq):
    first_last = int(np.argmax(_t5_bucket_np(np.arange(4 * REL_MAX_DIST)) == REL_BUCKETS - 1))
    o_const = -(-(first_last + QT - 1) // QT)
    return min(seq // QT, o_const + 1)


def _cmp_buckets(seq, n_cmp_pad):
    nq = seq // QT
    n_cmp = (seq - NSA_CMP_LEN) // NSA_CMP_STRIDE + 1
    i = np.arange(QT)[None, :, None]
    n = np.arange(nq)[:, None, None]
    c = np.arange(n_cmp_pad)[None, None, :]
    dist = n * QT + i - (c * NSA_CMP_STRIDE + NSA_CMP_LEN - 1)
    ok = (dist >= 0) & (c < n_cmp)
    return np.where(ok, _t5_bucket_np(dist), MASKED_BUCKET).astype(np.int32)


def _compress_kernel(t_ref, pos_ref, w1_ref, w2_ref, gain_ref, o_ref):
    half = t_ref.shape[3]
    a = t_ref[0, 0].astype(jnp.float32)
    lo = (a + pos_ref[0, :, :half]).astype(jnp.bfloat16)
    hi = (a + pos_ref[0, :, half:]).astype(jnp.bfloat16)
    p = jnp.dot(lo, w1_ref[0, :half, :], preferred_element_type=jnp.float32)
    q = jnp.dot(hi, w1_ref[0, half:, :], preferred_element_type=jnp.float32)
    n = q.shape[0]
    q_next = jnp.concatenate([q[1:], q[:1]], axis=0)
    hid = _gelu(p + q_next)
    out = jnp.dot(hid.astype(jnp.bfloat16), w2_ref[0], preferred_element_type=jnp.float32)
    is_key = pl.program_id(1) < NSA_KV_GROUPS
    r = lax.rsqrt(jnp.mean(out * out, axis=-1, keepdims=True) + EPS)
    normed = out * r * gain_ref[...]
    o_ref[0, 0] = jnp.where(is_key, normed, out).astype(o_ref.dtype)


def _compress(t_chunks, cmp_pos, cmp_w1, cmp_w2, k_gain0):
    b, nh, nchunk, width = t_chunks.shape
    dh = HEAD_DIM
    g = NSA_KV_GROUPS
    return pl.pallas_call(
        _compress_kernel,
        grid=(b, nh),
        in_specs=[pl.BlockSpec((1, 1, nchunk, width), lambda i, h: (i, h, 0, 0)),
                  pl.BlockSpec((1, 1, 2 * width), lambda i, h: (h // g, 0, 0)),
                  pl.BlockSpec((1, 2 * width, dh), lambda i, h: (h // g, 0, 0)),
                  pl.BlockSpec((1, dh, dh), lambda i, h: (h // g, 0, 0)),
                  pl.BlockSpec((1, dh), lambda i, h: (0, 0))],
        out_specs=pl.BlockSpec((1, 1, nchunk, dh), lambda i, h: (i, h, 0, 0)),
        out_shape=jax.ShapeDtypeStruct((b, nh, nchunk, dh), jnp.bfloat16),
        compiler_params=_cparams(("arbitrary", "arbitrary")),
        name="nsa_compress",
    )(t_chunks, cmp_pos.reshape(2, 1, 2 * width), cmp_w1.astype(jnp.bfloat16),
      cmp_w2.astype(jnp.bfloat16), k_gain0.reshape(1, dh))


def _stack_heads(q):
    return jnp.concatenate([q[:, r * HEAD_DIM:(r + 1) * HEAD_DIM] for r in range(NSA_REP)], axis=0)


def _cmp_select_kernel(q_ref, kc_ref, vc_ref, bias_ref, ov_ref, o_ref, sel_ref, *, n_sel):
    tq = q_ref.shape[1]
    n = pl.program_id(2)
    qs = _stack_heads(q_ref[0])
    s = lax.dot_general(qs, kc_ref[0, 0], (((1,), (1,)), ((), ())),
                        preferred_element_type=jnp.float32)
    bias = jnp.concatenate([bias_ref[r, 0] for r in range(NSA_REP)], axis=0)
    valid = bias > 0.5 * NEG
    s = jnp.where(valid, s + bias, NEG)
    m = jnp.max(s, axis=-1, keepdims=True)
    e = jnp.exp2(s - m)
    p = e / jnp.sum(e, axis=-1, keepdims=True) * valid.astype(jnp.float32)
    o = jnp.dot(p.astype(jnp.bfloat16), vc_ref[0, 0], preferred_element_type=jnp.float32)
    for r in range(NSA_REP):
        o_ref[0, :, r * HEAD_DIM:(r + 1) * HEAD_DIM] = o[r * tq:(r + 1) * tq].astype(o_ref.dtype)

    psum = p[0:tq]
    for r in range(1, NSA_REP):
        psum = psum + p[r * tq:(r + 1) * tq]
    imp = lax.dot_general(ov_ref[...], psum, (((1,), (1,)), ((), ())), preferred_element_type=jnp.float32,
                          precision=lax.Precision.HIGHEST)
    n_blocks = sel_ref.shape[2]
    jb = lax.broadcasted_iota(jnp.int32, (n_blocks, tq), 0).astype(jnp.float32)
    qpos = lax.broadcasted_iota(jnp.int32, (n_blocks, tq), 1)
    cur = ((n * tq + qpos) // NSA_SEL_LEN).astype(jnp.float32)
    forced = (jb == 0.0) | (jb == cur) | (jb == cur - 1.0)
    score = jnp.where(forced, SEL_FORCE, jnp.where(jb > cur, -SEL_FORCE, imp))
    chosen = jnp.zeros((n_blocks, tq), jnp.float32)
    for _ in range(n_sel):
        mx = jnp.max(score, axis=0, keepdims=True)
        first = jnp.min(jnp.where(score == mx, jb, float(n_blocks)), axis=0, keepdims=True)
        hit = jb == first
        chosen = jnp.where(hit, 1.0, chosen)
        score = jnp.where(hit, -jnp.inf, score)
    sel_ref[0, 0] = chosen


def _cmp_select(proj3, kcvc, cbias, overlap, lay, seq):
    b = proj3.shape[0]
    g = NSA_KV_GROUPS
    ncp = kcvc.shape[2]
    ns = seq // NSA_SEL_LEN
    n_sel = min(NSA_TOP_N, ns)
    qw = NSA_REP * HEAD_DIM
    return pl.pallas_call(
        functools.partial(_cmp_select_kernel, n_sel=n_sel),
        grid=(b, g, seq // QT),
        in_specs=[pl.BlockSpec((1, QT, qw), lambda i, j, n: (i, n, lay["q"] // NSA_REP + j)),
                  pl.BlockSpec((1, 1, ncp, HEAD_DIM), lambda i, j, n: (i, j, 0, 0)),
                  pl.BlockSpec((1, 1, ncp, HEAD_DIM), lambda i, j, n: (i, g + j, 0, 0)),
                  pl.BlockSpec((NSA_REP, 1, QT, ncp), lambda i, j, n: (j, n, 0, 0)),
                  pl.BlockSpec((ns, ncp), lambda i, j, n: (0, 0))],
        out_specs=[pl.BlockSpec((1, QT, qw), lambda i, j, n: (i, n, j)),
                   pl.BlockSpec((1, 1, ns, QT), lambda i, j, n: (i, j, 0, n))],
        out_shape=[jax.ShapeDtypeStruct((b, seq, NSA_HEADS * HEAD_DIM), jnp.bfloat16),
                   jax.ShapeDtypeStruct((b, g, ns, seq), jnp.float32)],
        compiler_params=_cparams(("arbitrary", "arbitrary", "arbitrary")),
        name="nsa_cmp_select",
    )(proj3, kcvc, kcvc, cbias, overlap)


def _sel_attn_kernel(q_ref, k_ref, vt_ref, sel_ref, exp_ref, bias_ref, o_ref, madd_ref):
    tq = q_ref.shape[1]
    n = pl.program_id(2)
    n_bias = bias_ref.shape[1]
    q = q_ref[0].astype(jnp.float32)
    qt = jnp.concatenate([q[:, r * HEAD_DIM:(r + 1) * HEAD_DIM].T for r in range(NSA_REP)],
                         axis=1).astype(jnp.bfloat16)
    km = jnp.dot(exp_ref[...], sel_ref[0, 0].astype(jnp.bfloat16), preferred_element_type=jnp.float32)
    madd_ref[...] = (km - 1.0) * (-NEG)

    last_tile = k_ref.shape[1] // QT - 1

    def tile_start(kt):
        return pl.multiple_of(jnp.minimum(kt, last_tile) * QT, QT)

    def logits(kt):
        start = tile_start(kt)
        s = jnp.dot(k_ref[0, pl.ds(start, QT), :], qt, preferred_element_type=jnp.float32)
        o = n - kt
        ob = jnp.clip(o, 0, n_bias - 1)
        bias = jnp.concatenate([bias_ref[r, ob] for r in range(NSA_REP)], axis=1)
        ma = jnp.where(o >= 0, madd_ref[pl.ds(start, QT), :], NEG)
        return s + bias + jnp.concatenate([ma] * NSA_REP, axis=1)

    def body(kp, carry):
        m, l, acc, s0, s1 = carry
        n0, n1 = logits(2 * kp + 2), logits(2 * kp + 3)
        m_new = jnp.maximum(m, jnp.maximum(jnp.max(s0, axis=0, keepdims=True),
                                           jnp.max(s1, axis=0, keepdims=True)))
        alpha = jnp.exp2(m - m_new)
        p0 = jnp.exp2(s0 - m_new)
        p1 = jnp.exp2(s1 - m_new)
        l = alpha * l + jnp.sum(p0, axis=0, keepdims=True) + jnp.sum(p1, axis=0, keepdims=True)
        acc = (alpha * acc
               + jnp.dot(vt_ref[0, 0, :, pl.ds(tile_start(2 * kp), QT)], p0.astype(jnp.bfloat16),
                         preferred_element_type=jnp.float32)
               + jnp.dot(vt_ref[0, 0, :, pl.ds(tile_start(2 * kp + 1), QT)], p1.astype(jnp.bfloat16),
                         preferred_element_type=jnp.float32))
        return m_new, l, acc, n0, n1

    cols = NSA_REP * tq
    init = (jnp.full((1, cols), NEG, jnp.float32), jnp.zeros((1, cols), jnp.float32),
            jnp.zeros((HEAD_DIM, cols), jnp.float32), logits(0), logits(1))
    m, l, acc, _, _ = lax.fori_loop(0, n // 2 + 1, body, init)
    out = acc / l
    for r in range(NSA_REP):
        o_ref[0, :, r * HEAD_DIM:(r + 1) * HEAD_DIM] = out[:, r * tq:(r + 1) * tq].T.astype(o_ref.dtype)


def _sel_attention(proj3, sel, expand_t, bias_sel_t, lay, seq):
    b = proj3.shape[0]
    g = NSA_KV_GROUPS
    ns = seq // NSA_SEL_LEN
    qw = NSA_REP * HEAD_DIM
    nb = bias_sel_t.shape[1]
    cb_k = lay["kv"] + 2 * g
    cb_v = lay["kv"] + 3 * g
    vt = proj3[:, :, cb_v * LANES:(cb_v + g) * LANES].reshape(b, seq, g, HEAD_DIM).transpose(0, 2, 3, 1)
    return pl.pallas_call(
        _sel_attn_kernel,
        grid=(b, g, seq // QT),
        in_specs=[pl.BlockSpec((1, QT, qw), lambda i, j, n: (i, n, lay["q"] // NSA_REP + j)),
                  pl.BlockSpec((1, seq, HEAD_DIM), lambda i, j, n: (i, 0, cb_k + j)),
                  pl.BlockSpec((1, 1, HEAD_DIM, seq), lambda i, j, n: (i, j, 0, 0)),
                  pl.BlockSpec((1, 1, ns, QT), lambda i, j, n: (i, j, 0, n)),
                  pl.BlockSpec((seq, ns), lambda i, j, n: (0, 0)),
                  pl.BlockSpec((NSA_REP, nb, QT, QT), lambda i, j, n: (j, 0, 0, 0))],
        out_specs=pl.BlockSpec((1, QT, qw), lambda i, j, n: (i, n, j)),
        out_shape=jax.ShapeDtypeStruct((b, seq, NSA_HEADS * HEAD_DIM), jnp.bfloat16),
        scratch_shapes=[pltpu.VMEM((seq, QT), jnp.float32)],
        compiler_params=_cparams(("arbitrary", "arbitrary", "arbitrary")),
        name="nsa_selected",
    )(proj3, proj3, vt, sel, expand_t, bias_sel_t)


def _band_softmax(qs, k_tile, v_tile, bias, n, n_prev):
    logits = []
    for j in range(n_prev + 1):
        kb = n - n_prev + j
        k = k_tile(jnp.maximum(kb, 0))
        s = lax.dot_general(qs, k, (((1,), (1,)), ((), ())), preferred_element_type=jnp.float32)
        bj = jnp.where(kb >= 0, bias[:, j * QT:(j + 1) * QT], NEG)
        logits.append(jnp.where(bj > 0.5 * NEG, s + bj, NEG))
    m = logits[0].max(axis=-1, keepdims=True)
    for s in logits[1:]:
        m = jnp.maximum(m, s.max(axis=-1, keepdims=True))
    l = jnp.zeros_like(m)
    acc = jnp.zeros((qs.shape[0], HEAD_DIM), jnp.float32)
    for j, s in enumerate(logits):
        p = jnp.exp2(s - m)
        l = l + jnp.sum(p, axis=-1, keepdims=True)
        acc = acc + jnp.dot(p.astype(jnp.bfloat16), v_tile(jnp.maximum(n - n_prev + j, 0)),
                            preferred_element_type=jnp.float32)
    return acc, m, l


def _win_attn_kernel(q_ref, k_ref, v_ref, bias_ref, o_ref, *, n_prev):
    tq = q_ref.shape[1]
    n = pl.program_id(2)
    qs = _stack_heads(q_ref[0])
    bias = jnp.concatenate([bias_ref[r, 0] for r in range(NSA_REP)], axis=0)
    tile = lambda ref: (lambda kb: ref[0, pl.ds(pl.multiple_of(kb * QT, QT), QT), :])
    acc, _, l = _band_softmax(qs, tile(k_ref), tile(v_ref), bias, n, n_prev)
    o = acc / l
    for r in range(NSA_REP):
        o_ref[0, :, r * HEAD_DIM:(r + 1) * HEAD_DIM] = o[r * tq:(r + 1) * tq].astype(o_ref.dtype)


def _win_attention(proj3, bias_win, lay, seq):
    b = proj3.shape[0]
    g = NSA_KV_GROUPS
    qw = NSA_REP * HEAD_DIM
    n_prev = -(-(NSA_WINDOW - 1) // QT)
    cb_k = lay["kv"] + 4 * g
    cb_v = lay["kv"] + 5 * g
    return pl.pallas_call(
        functools.partial(_win_attn_kernel, n_prev=n_prev),
        grid=(b, g, seq // QT),
        in_specs=[pl.BlockSpec((1, QT, qw), lambda i, j, n: (i, n, lay["q"] // NSA_REP + j)),
                  pl.BlockSpec((1, seq, HEAD_DIM), lambda i, j, n: (i, 0, cb_k + j)),
                  pl.BlockSpec((1, seq, HEAD_DIM), lambda i, j, n: (i, 0, cb_v + j)),
                  pl.BlockSpec((NSA_REP, 1, QT, (n_prev + 1) * QT), lambda i, j, n: (j, 0, 0, 0))],
        out_specs=pl.BlockSpec((1, QT, qw), lambda i, j, n: (i, n, j)),
        out_shape=jax.ShapeDtypeStruct((b, seq, NSA_HEADS * HEAD_DIM), jnp.bfloat16),
        compiler_params=_cparams(("arbitrary", "arbitrary", "arbitrary")),
        name="nsa_window",
    )(proj3, proj3, proj3, bias_win)


DIL_TILES_PER_STEP = 4


def _dil_attn_kernel(q_ref, k_ref, v_ref, bias_ref, o_ref, lse_ref, *, n_prev, dil):
    tiles = q_ref.shape[1] // (QT * dil)
    for sub in range(DIL_TILES_PER_STEP):
        t = pl.program_id(2) * DIL_TILES_PER_STEP + sub
        r = t // tiles
        n = t % tiles
        rows = lambda kb, r=r: pl.ds(kb * (QT * dil) + r, QT, stride=dil) if dil > 1 else pl.ds(
            pl.multiple_of(kb * QT, QT), QT)
        tile = lambda ref, rows=rows: (lambda kb: ref[0, rows(kb), :].astype(jnp.bfloat16))
        acc, m, l = _band_softmax(tile(q_ref)(n), tile(k_ref), tile(v_ref), bias_ref[0, 0], n, n_prev)
        o_ref[0, rows(n), :] = acc / l
        lse_ref[0, rows(n), :] = jnp.broadcast_to(m * LN2 + jnp.log(l), (QT, HEAD_DIM))


def _dil_attention(projd3, bias_dil, seq, gi):
    b = projd3.shape[0]
    window, dil = DIL_CONFIGS[gi]
    p_heads = DIL_HEADS_PER_GROUP
    n_prev = -(-(window // dil) // QT)
    cb_q = gi * p_heads
    cb_k = cb_q + DIL_HEADS
    cb_v = cb_k + DIL_HEADS
    out_w = p_heads * HEAD_DIM
    whole = lambda cb: pl.BlockSpec((1, seq, HEAD_DIM), lambda i, p, t: (i, 0, cb + p))
    return pl.pallas_call(
        functools.partial(_dil_attn_kernel, n_prev=n_prev, dil=dil),
        grid=(b, p_heads, seq // (QT * DIL_TILES_PER_STEP)),
        in_specs=[whole(cb_q), whole(cb_k), whole(cb_v),
                  pl.BlockSpec((1, 1, QT, (n_prev + 1) * QT), lambda i, p, t: (p, 0, 0, 0))],
        out_specs=[whole(0), whole(0)],
        out_shape=[jax.ShapeDtypeStruct((b, seq, out_w), jnp.float32),
                   jax.ShapeDtypeStruct((b, seq, out_w), jnp.float32)],
        compiler_params=_cparams(("arbitrary",) * 3),
        name=f"dilated_attn_{gi}",
    )(projd3, projd3, projd3, bias_dil)


def _post_kernel(ocmp_ref, osel_ref, owin_ref, gate_ref, od0_ref, od1_ref, od2_ref, l0_ref, l1_ref, l2_ref,
                 mg1_ref, mg2_ref, x_ref, g1_ref, sc2_ref, sh2_ref, n2_ref, wn_ref, wd_ref, wo_ref,
                 x1_ref, h2_ref):
    gates = _sigmoid(gate_ref[...].astype(jnp.float32))
    parts = []
    for h in range(NSA_HEADS):
        sl = slice(h * HEAD_DIM, (h + 1) * HEAD_DIM)
        y = (gates[:, 3 * h:3 * h + 1] * ocmp_ref[:, sl].astype(jnp.float32)
             + gates[:, 3 * h + 1:3 * h + 2] * osel_ref[:, sl].astype(jnp.float32)
             + gates[:, 3 * h + 2:3 * h + 3] * owin_ref[:, sl].astype(jnp.float32))
        parts.append(y.astype(jnp.bfloat16))
    y_nsa = jnp.concatenate(parts, axis=1)

    l0, l1, l2 = l0_ref[...], l1_ref[...], l2_ref[...]
    mx = jnp.maximum(jnp.maximum(l0, l1), l2)
    e0, e1, e2 = jnp.exp(l0 - mx), jnp.exp(l1 - mx), jnp.exp(l2 - mx)
    den = e0 + e1 + e2
    y_dil = ((e0 / den) * od0_ref[...].astype(jnp.float32) + (e1 / den) * od1_ref[...].astype(jnp.float32)
             + (e2 / den) * od2_ref[...].astype(jnp.float32)).astype(jnp.bfloat16)

    a = jnp.dot(y_nsa, wn_ref[...], preferred_element_type=jnp.float32)
    bb = jnp.dot(y_dil, wd_ref[...], preferred_element_type=jnp.float32)
    merged = (_sigmoid(mg1_ref[...].astype(jnp.float32)) * a
              + _sigmoid(mg2_ref[...].astype(jnp.float32)) * bb).astype(jnp.bfloat16)
    mix = jnp.dot(merged, wo_ref[...], preferred_element_type=jnp.float32)
    x1 = x_ref[...] + g1_ref[0] * mix
    x1_ref[...] = x1
    y = x1 * lax.rsqrt(jnp.mean(x1 * x1, axis=-1, keepdims=True) + EPS) * n2_ref[...]
    h2_ref[...] = y * (1.0 + sc2_ref[0]) + sh2_ref[0]


def _post(ocmp, osel, owin, proj2d, odil, lsedil, x2d, g1, sc2, sh2, norm2_g, wn, wd, wo, lay, seq):
    t, d = x2d.shape
    tm = 256
    per_batch = seq // tm
    nw = NSA_HEADS * HEAD_DIM
    dw = DIL_HEADS_PER_GROUP * HEAD_DIM
    row = lambda w: pl.BlockSpec((tm, w), lambda i: (i, 0))
    mod = pl.BlockSpec((1, 1, d), lambda i: (i // per_batch, 0, 0))
    full = lambda a: pl.BlockSpec(a.shape, lambda i: (0, 0))
    return pl.pallas_call(
        _post_kernel,
        grid=(t // tm,),
        in_specs=[row(nw), row(nw), row(nw),
                  pl.BlockSpec((tm, LANES), lambda i: (i, lay["gate"])),
                  row(dw), row(dw), row(dw), row(dw), row(dw), row(dw),
                  pl.BlockSpec((tm, d), lambda i: (i, 0)),
                  pl.BlockSpec((tm, d), lambda i: (i, 1)),
                  row(d), mod, mod, mod,
                  pl.BlockSpec((1, d), lambda i: (0, 0)),
                  full(wn), full(wd), full(wo)],
        out_specs=[row(d), row(d)],
        out_shape=[jax.ShapeDtypeStruct((t, d), jnp.float32), jax.ShapeDtypeStruct((t, d), jnp.float32)],
        compiler_params=_cparams(("arbitrary",)),
        name="merge_out_proj",
    )(ocmp, osel, owin, proj2d, odil[0], odil[1], odil[2], lsedil[0], lsedil[1], lsedil[2],
      proj2d, proj2d, x2d, g1, sc2, sh2, norm2_g.reshape(1, d), wn, wd, wo)


def _topk_rows(sc, k, payload=None):
    n = sc.shape[0]
    half = n // 2
    row = lax.broadcasted_iota(jnp.int32, (half, sc.shape[1]), 0).astype(jnp.float32)
    a, b = sc[:half], sc[half:]
    a_first = a >= b
    hi, lo = jnp.where(a_first, a, b), jnp.where(a_first, b, a)
    hi_id = jnp.where(a_first, row, row + float(half))
    lo_id = jnp.where(a_first, row + float(half), row)
    if payload is not None:
        hi_pay = jnp.where(a_first, payload[:half], payload[half:])
        lo_pay = jnp.where(a_first, payload[half:], payload[:half])
    vals, picks = [], []
    for _ in range(k):
        mx = jnp.max(hi, axis=0, keepdims=True)
        first = jnp.min(jnp.where(hi == mx, hi_id, float(n)), axis=0, keepdims=True)
        hit = hi_id == first
        vals.append(mx)
        if payload is None:
            picks.append(first)
        else:
            picks.append(jnp.sum(jnp.where(hit, hi_pay, 0.0), axis=0, keepdims=True))
            hi_pay = jnp.where(hit, lo_pay, hi_pay)
        hi = jnp.where(hit, lo, hi)
        hi_id = jnp.where(hit, lo_id, hi_id)
        lo = jnp.where(hit, -jnp.inf, lo)
    return jnp.concatenate(vals, axis=0), jnp.concatenate(picks, axis=0)


def _route_kernel(h_ref, wq_ref, gain_ref, keys_ref, idx_ref, gate_ref):
    q = jnp.dot(h_ref[...].astype(jnp.bfloat16), wq_ref[...], preferred_element_type=jnp.float32)
    k = PEER_TOPK
    for head in range(PEER_HEADS):
        tops = []
        for part in range(2):
            hp = head * 2 + part
            a = q[:, hp * PEER_HALF:(hp + 1) * PEER_HALF]
            qn = (a * lax.rsqrt(jnp.mean(a * a, axis=-1, keepdims=True) + EPS)
                  * gain_ref[hp:hp + 1, :]).astype(jnp.bfloat16)
            sc = lax.dot_general(keys_ref[hp], qn, (((1,), (1,)), ((), ())),
                                 preferred_element_type=jnp.float32)
            tops.append(_topk_rows(sc, k))
        (s1, i1), (s2, i2) = tops
        half = k // 2
        sub = lax.broadcasted_iota(jnp.int32, (half, s1.shape[1]), 0)
        vals = [s1[0:1] + s2]
        ids = [i1[0:1] * float(PEER_N_KEYS) + i2]
        for a in range(1, half):
            vals.append(jnp.where(sub < k // (a + 1), s1[a:a + 1] + s2[:half], -jnp.inf))
            ids.append(i1[a:a + 1] * float(PEER_N_KEYS) + i2[:half])
        vals.append(s1[half:] + s2[0:1])
        ids.append(i1[half:] * float(PEER_N_KEYS) + i2[0:1])
        top_s, expert = _topk_rows(jnp.concatenate(vals, axis=0), k, payload=jnp.concatenate(ids, axis=0))
        e = jnp.exp(top_s - top_s[0:1])
        gate = e / jnp.sum(e, axis=0, keepdims=True)
        idx_ref[head * k:(head + 1) * k, :] = expert.astype(jnp.int32)
        gate_ref[head * k:(head + 1) * k, :] = gate


def _route(h2, wq_bf16, q_gain, sub_keys_bf16):
    t, d = h2.shape
    tm = 2 * LANES
    hk = PEER_HEADS * PEER_TOPK
    nq = wq_bf16.shape[1]
    return pl.pallas_call(
        _route_kernel,
        grid=(t // tm,),
        in_specs=[pl.BlockSpec((tm, d), lambda i: (i, 0)),
                  pl.BlockSpec((d, nq), lambda i: (0, 0)),
                  pl.BlockSpec((2 * PEER_HEADS, PEER_HALF), lambda i: (0, 0)),
                  pl.BlockSpec((2 * PEER_HEADS, PEER_N_KEYS, PEER_HALF), lambda i: (0, 0, 0))],
        out_specs=[pl.BlockSpec((hk, tm), lambda i: (0, i)),
                   pl.BlockSpec((hk, tm), lambda i: (0, i))],
        out_shape=[jax.ShapeDtypeStruct((hk, t), jnp.int32), jax.ShapeDtypeStruct((hk, t), jnp.float32)],
        compiler_params=_cparams(("arbitrary",)),
        name="peer_route",
    )(h2, wq_bf16, q_gain.reshape(2 * PEER_HEADS, PEER_HALF), sub_keys_bf16)


PEER_TB = 32
SUBLANES = 8


PEER_STAGES = 4
PEER_LOOKAHEAD = 3


def _expert_kernel(idx_ref, nidx_ref, h_ref, gate_ref, x1_ref, g2_ref, tab_ref, o_ref, *scratch):
    stages, sem = scratch[:PEER_STAGES], scratch[PEER_STAGES]
    tb, d = h_ref.shape
    k = PEER_TOPK
    hk = PEER_HEADS * k
    nr, nc = tb // SUBLANES, d // LANES
    step = pl.program_id(0)
    hi_mask = jnp.uint32(0xFFFF0000)

    def start_rows(ids, head, j, r):
        for s in range(SUBLANES):
            e = ids[0, 0, (r * SUBLANES + s) * hk + head * k + j]
            pltpu.make_async_copy(tab_ref.at[pl.ds(pl.multiple_of(e * nc, nc), nc), :],
                                  stages[head % PEER_STAGES].at[j, r, :, s, :],
                                  sem.at[head % PEER_STAGES]).start(priority=s % 2)

    def words(stage, j, r, c):
        return stage[j, r, c]

    def start_head(ids, head):
        for j in range(k):
            for r in range(nr):
                start_rows(ids, head, j, r)

    def wait_head(head):
        buf = stages[head % PEER_STAGES]
        pltpu.make_async_copy(buf, buf, sem.at[head % PEER_STAGES]).wait()

    @pl.when(step == 0)
    def _():
        for head in range(PEER_LOOKAHEAD):
            start_head(idx_ref, head)

    lane = lax.broadcasted_iota(jnp.int32, (SUBLANES, LANES), 1)
    for head in range(PEER_HEADS):
        stage = stages[head % PEER_STAGES]
        wait_head(head)
        ahead = head + PEER_LOOKAHEAD
        ahead_ids, ahead_head = (idx_ref, ahead) if ahead < PEER_HEADS else (nidx_ref, ahead - PEER_HEADS)
        acts = []
        for r in range(nr):
            rows = slice(r * SUBLANES, (r + 1) * SUBLANES)
            pre = jnp.zeros((SUBLANES, LANES), jnp.float32)
            for j in range(k):
                start_rows(ahead_ids, ahead_head, j, r)
                acc = None
                for c in range(nc):
                    u = lax.bitcast_convert_type(words(stage, j, r, c) & hi_mask, jnp.float32)
                    prod = h_ref[rows, c * LANES:(c + 1) * LANES] * u
                    acc = prod if acc is None else acc + prod
                pre = jnp.where(lane == head * k + j, jnp.sum(acc, axis=-1, keepdims=True), pre)
            acts.append(gate_ref[rows, :] * _gelu(pre))
        for r in range(nr):
            rows = slice(r * SUBLANES, (r + 1) * SUBLANES)
            accs = [None] * nc
            for j in range(k):
                a = jnp.broadcast_to(acts[r][:, head * k + j:head * k + j + 1], (SUBLANES, LANES))
                for c in range(nc):
                    term = a * lax.bitcast_convert_type(words(stage, j, r, c) << 16, jnp.float32)
                    accs[c] = term if accs[c] is None else accs[c] + term
            for c in range(nc):
                cs = slice(c * LANES, (c + 1) * LANES)
                if head == 0:
                    o_ref[rows, cs] = accs[c]
                else:
                    o_ref[rows, cs] = o_ref[rows, cs] + accs[c]

    @pl.when(step + 1 == pl.num_programs(0))
    def _():
        for head in range(PEER_LOOKAHEAD):
            wait_head(head)
    o_ref[...] = x1_ref[...] + g2_ref[0] * o_ref[...]


def _experts(idx_tok, gate_tok, h2, x1, g2, table, seq):
    t, d = h2.shape
    tb = PEER_TB
    hk = PEER_HEADS * PEER_TOPK
    per_batch = seq // tb
    nsteps = t // tb
    ids = idx_tok.reshape(nsteps, 1, tb * hk)
    stage = pltpu.VMEM((PEER_TOPK, tb // SUBLANES, d // LANES, SUBLANES, LANES), jnp.uint32)
    return pl.pallas_call(
        _expert_kernel,
        grid=(nsteps,),
        in_specs=[pl.BlockSpec((1, 1, tb * hk), lambda i: (i, 0, 0), memory_space=pltpu.SMEM),
                  pl.BlockSpec((1, 1, tb * hk), lambda i: (jnp.minimum(i + 1, nsteps - 1), 0, 0),
                               memory_space=pltpu.SMEM),
                  pl.BlockSpec((tb, d), lambda i: (i, 0)),
                  pl.BlockSpec((tb, hk), lambda i: (i, 0)),
                  pl.BlockSpec((tb, d), lambda i: (i, 0)),
                  pl.BlockSpec((1, 1, d), lambda i: (i // per_batch, 0, 0)),
                  pl.BlockSpec(memory_space=pl.ANY)],
        out_specs=pl.BlockSpec((tb, d), lambda i: (i, 0)),
        out_shape=jax.ShapeDtypeStruct((t, d), jnp.float32),
        scratch_shapes=[stage] * PEER_STAGES + [pltpu.SemaphoreType.DMA((PEER_STAGES,))],
        compiler_params=_cparams(("arbitrary",)),
        name="peer_experts",
    )(ids, ids, h2, gate_tok, x1, g2, table)


PACK_ROWS = 256


def _pack_kernel(u_ref, v_ref, o_ref):
    nc = u_ref.shape[1] // LANES
    as_bits = lambda x: lax.bitcast_convert_type(x.astype(jnp.bfloat16).astype(jnp.float32), jnp.uint32)
    w = as_bits(u_ref[...]) | (as_bits(v_ref[...]) >> 16)
    for c in range(nc):
        o_ref[pl.ds(c, PACK_ROWS, stride=nc), :] = w[:, c * LANES:(c + 1) * LANES]


def _pack_expert_table(u, v):
    e, d = u.shape
    nc = d // LANES
    return pl.pallas_call(
        _pack_kernel,
        grid=(e // PACK_ROWS,),
        in_specs=[pl.BlockSpec((PACK_ROWS, d), lambda i: (i, 0)),
                  pl.BlockSpec((PACK_ROWS, d), lambda i: (i, 0))],
        out_specs=pl.BlockSpec((PACK_ROWS * nc, LANES), lambda i: (i, 0)),
        out_shape=jax.ShapeDtypeStruct((e * nc, LANES), jnp.uint32),
        compiler_params=_cparams(("arbitrary",)),
        name="peer_pack_table",
    )(u, v)


def _proj_weights(w_in, nsa_q_gain, nsa_k_gain, dil_q_gain, dil_k_gain, d_model, lay):
    dh = HEAD_DIM
    off_kv = NSA_HEADS * dh
    off_gate = off_kv + 6 * NSA_KV_GROUPS * dh
    off_dil = off_gate + 3 * NSA_HEADS
    off_merge = off_dil + 3 * DIL_HEADS * dh
    n_cols = lay["n"] * LANES
    pad = n_cols - (lay["gate"] * LANES + 3 * NSA_HEADS)
    w = jnp.concatenate([w_in[:, off_merge:], w_in[:, :off_kv], w_in[:, off_kv:off_gate],
                         w_in[:, off_gate:off_dil], jnp.zeros((d_model, pad), w_in.dtype)],
                        axis=1).astype(jnp.bfloat16)
    w_dil = w_in[:, off_dil:off_merge].astype(jnp.bfloat16)
    ones = jnp.ones((dh,), jnp.float32)

    def rows(spec):
        gains = jnp.concatenate([g for g, _, n in spec for _ in range(n)])
        flags = jnp.concatenate([jnp.full((dh,), f, jnp.float32) for _, f, n in spec for _ in range(n)])
        return flags.reshape(1, -1), gains.reshape(1, -1)

    g = NSA_KV_GROUPS
    main = [(ones, 0.0, lay["q"]), (nsa_q_gain * QK_SCALE, 1.0, NSA_HEADS), (ones, 0.0, 2 * g),
            (nsa_k_gain[1], 1.0, g), (ones, 0.0, g), (nsa_k_gain[2], 1.0, g), (ones, 0.0, g),
            (ones, 0.0, lay["n"] - lay["gate"])]
    dil = [(dil_q_gain * QK_SCALE, 1.0, DIL_HEADS), (dil_k_gain, 1.0, DIL_HEADS), (ones, 0.0, DIL_HEADS)]
    return (w,) + rows(main), (w_dil,) + rows(dil)


def _token_mixer_and_norm2(x, mod, norm1_g, w_in, nsa_q_gain, nsa_k_gain, cmp_pos, cmp_w1, cmp_w2,
                           dil_q_gain, dil_k_gain, w_br_nsa, w_br_dil, w_out, norm2_g, rel_bias):
    b, seq, d = x.shape
    t = b * seq
    lay = _layout(d)
    sh1, sc1, g1, sh2, sc2, g2 = [m.reshape(b, 1, d) for m in jnp.split(mod, 6, axis=-1)]
    x2d = x.reshape(t, d)

    main_w, dil_w = _proj_weights(w_in, nsa_q_gain, nsa_k_gain, dil_q_gain, dil_k_gain, d, lay)
    blocks_per_tile = 4
    proj2d = _projection(x2d, norm1_g, sc1, sh1, *main_w, seq, jnp.bfloat16, (0, lay["q"] // blocks_per_tile))
    proj3 = proj2d.reshape(b, seq, lay["n"] * LANES)
    projd3 = _projection(x2d, norm1_g, sc1, sh1, *dil_w, seq, jnp.float32,
                         (2 * DIL_HEADS // blocks_per_tile, 3 * DIL_HEADS // blocks_per_tile)).reshape(b, seq, -1)

    rel_flat = rel_bias.reshape(-1) * LOG2E
    nchunk = seq // NSA_CMP_STRIDE
    ns = seq // NSA_SEL_LEN
    bias_win = _bias_table(rel_flat, _band_buckets(-(-(NSA_WINDOW - 1) // QT), NSA_WINDOW - 1, 1), 0, NSA_HEADS)
    bias_sel = _bias_table(rel_flat, _causal_buckets(_n_causal_tiles(seq)), 0, NSA_HEADS)
    bias_cmp = _bias_table(rel_flat, _cmp_buckets(seq, nchunk), 0, NSA_HEADS)
    bias_dil = [_bias_table(rel_flat, _band_buckets(-(-(wnd // dil) // QT), wnd // dil, dil),
                            NSA_HEADS + gi * DIL_HEADS_PER_GROUP, DIL_HEADS_PER_GROUP)
                for gi, (wnd, dil) in enumerate(DIL_CONFIGS)]

    g = NSA_KV_GROUPS
    c0 = lay["kv"] * LANES
    tc = proj3[:, :, c0:c0 + 2 * g * HEAD_DIM].reshape(b, nchunk, NSA_CMP_STRIDE, 2 * g, HEAD_DIM)
    tc = tc.transpose(0, 3, 1, 2, 4).reshape(b, 2 * g, nchunk, NSA_CMP_STRIDE * HEAD_DIM)
    kcvc = _compress(tc, cmp_pos, cmp_w1, cmp_w2, nsa_k_gain[0])

    cstart = np.arange(nchunk) * NSA_CMP_STRIDE
    sstart = np.arange(ns) * NSA_SEL_LEN
    n_cmp = (seq - NSA_CMP_LEN) // NSA_CMP_STRIDE + 1
    overlap = ((cstart[:, None] < sstart[None, :] + NSA_SEL_LEN) & (cstart[:, None] + NSA_CMP_LEN > sstart[None, :])
               & (np.arange(nchunk)[:, None] < n_cmp)).astype(np.float32)
    expand_t = (np.arange(seq)[:, None] // NSA_SEL_LEN == np.arange(ns)[None, :]).astype(np.float32)

    ocmp, sel = _cmp_select(proj3, kcvc, bias_cmp, jnp.asarray(overlap.T), lay, seq)
    osel = _sel_attention(proj3, sel, jnp.asarray(expand_t, jnp.bfloat16), bias_sel, lay, seq)
    owin = _win_attention(proj3, bias_win, lay, seq)
    odil, lsedil = zip(*[_dil_attention(projd3, bias_dil[gi], seq, gi) for gi in range(len(DIL_CONFIGS))])

    x1, h2 = _post(ocmp.reshape(t, -1), osel.reshape(t, -1), owin.reshape(t, -1), proj2d,
                   [o.reshape(t, -1) for o in odil], [l.reshape(t, -1) for l in lsedil],
                   x2d, g1, sc2, sh2, norm2_g, w_br_nsa.astype(jnp.bfloat16), w_br_dil.astype(jnp.bfloat16),
                   w_out.astype(jnp.bfloat16), lay, seq)
    return x1, h2, g2


def kernel(x, c, w_ada, b_ada, norm1_g, w_in, nsa_q_gain, nsa_k_gain, cmp_pos, cmp_w1, cmp_w2,
           dil_q_gain, dil_k_gain, w_br_nsa, w_br_dil, w_out, norm2_g, peer_w_q, peer_q_gain,
           peer_sub_keys, peer_u, peer_v, rel_bias):
    b, seq, d = x.shape
    depth = w_ada.shape[0]
    for layer in range(depth):
        mod = _modulation(c, w_ada[layer], b_ada[layer])
        x1, h2, g2 = _token_mixer_and_norm2(
            x, mod, norm1_g[layer], w_in[layer], nsa_q_gain[layer], nsa_k_gain[layer], cmp_pos[layer],
            cmp_w1[layer], cmp_w2[layer], dil_q_gain[layer], dil_k_gain[layer], w_br_nsa[layer],
            w_br_dil[layer], w_out[layer], norm2_g[layer], rel_bias)
        idx, gate = _route(h2, peer_w_q[layer].astype(jnp.bfloat16), peer_q_gain[layer],
                           peer_sub_keys[layer].reshape(2 * PEER_HEADS, PEER_N_KEYS, PEER_HALF).astype(jnp.bfloat16))
        table = _pack_expert_table(peer_u[layer], peer_v[layer])
        out = _experts(idx.T, gate.T, h2, x1, g2, table, seq)
        x = out.reshape(b, seq, d)
    return x
```
